```python
import jax, jax.numpy as jnp
from jax import lax
import numpy as np

D_MODEL = 1024
BATCH = 32
SEQ = 256
DEPTH = 1
DEC_BATCH = 2
DEC_SEQ = 2048
PAST_LEN = 512

GRID_W = 64
Q_BLOCK = 128
RMS_EPS = 1e-6
ROPE_THETA = 10000.0
N_MOD = 6
H_A = 8
KV_A = 2
HD_A = 128
H_B = 8
Q_LORA = 256
KV_LORA = 128
NOPE_B = 128
ROPE_B = 64
VD_B = 128
N_EXPERTS = 64
TOP_K = 8
N_GROUPS = 8
TOPK_GROUPS = 4
D_EXPERT = 256
D_SHARED = 256
ROUTED_SCALE = 2.5
EXPERT_BLOCK = 128
IN_SIZES = (H_A * HD_A, KV_A * HD_A, KV_A * HD_A, Q_LORA, KV_LORA, ROPE_B, D_MODEL, D_MODEL)
D_IN = H_A * HD_A + 2 * KV_A * HD_A + Q_LORA + KV_LORA + ROPE_B + 2 * D_MODEL

kernel_name = 'hybrid_gqa_mla_moe_diffusion_step'


def rmsnorm(x, g):
    xf = x.astype(jnp.float32)
    y = xf * lax.rsqrt(jnp.mean(xf * xf, axis=-1, keepdims=True) + RMS_EPS)
    return (y * g.astype(jnp.float32)).astype(x.dtype)


def axial_rope(n_tokens, rot_dim):
    rows = n_tokens // GRID_W
    t = jnp.arange(rows * GRID_W)
    row = (t // GRID_W).astype(jnp.float32)
    col = (t % GRID_W).astype(jnp.float32)
    axis_dim = rot_dim // 2
    inv = ROPE_THETA ** (-jnp.arange(0, axis_dim, 2, dtype=jnp.float32) / axis_dim)
    ang = jnp.concatenate([row[:, None] * inv, col[:, None] * inv], axis=-1)
    return jnp.cos(ang), jnp.sin(ang)


def apply_rope(x, cos, sin):
    half = x.shape[-1] // 2
    xf = x.astype(jnp.float32)
    x1, x2 = xf[..., :half], xf[..., half:]
    return jnp.concatenate([x1 * cos - x2 * sin, x2 * cos + x1 * sin], axis=-1).astype(x.dtype)


def block_attention(q, k, v, scale):
    b, kv, g, s, dk = q.shape
    nb = s // Q_BLOCK
    qb = q.reshape(b, kv, g, nb, Q_BLOCK, dk).transpose(3, 0, 1, 2, 4, 5)

    def one_block(q_blk):
        sc = jnp.einsum('bkgqd,bksd->bkgqs', q_blk, k).astype(jnp.float32) * scale
        p = jax.nn.softmax(sc, axis=-1).astype(v.dtype)
        return jnp.einsum('bkgqs,bksd->bkgqd', p, v)

    o = lax.map(one_block, qb)
    return o.transpose(1, 2, 3, 0, 4, 5).reshape(b, kv, g, s, v.shape[-1])


def mixing_sublayer(h, lw, rope_a, rope_b, ctx):
    b, s, _ = h.shape
    z = h @ lw['w_in']
    cuts = np.cumsum(IN_SIZES)[:-1].tolist()
    q_a, k_a, v_a, c_q, c_kv, k_r, g_a, g_b = jnp.split(z, cuts, axis=-1)
    qa = rmsnorm(q_a.reshape(b, s, H_A, HD_A), lw['gqa_qn']).transpose(0, 2, 1, 3)
    ka = rmsnorm(k_a.reshape(b, s, KV_A, HD_A), lw['gqa_kn']).transpose(0, 2, 1, 3)
    va = v_a.reshape(b, s, KV_A, HD_A).transpose(0, 2, 1, 3)
    cq = rmsnorm(c_q, lw['mla_qa_g'])
    qb = (cq @ lw['w_qb']).reshape(b, s, H_B, NOPE_B + ROPE_B).transpose(0, 2, 1, 3)
    qb_nope = rmsnorm(qb[..., :NOPE_B], lw['mla_qn_nope'])
    qb_rope = rmsnorm(qb[..., NOPE_B:], lw['mla_qn_rope'])
    ckv = rmsnorm(c_kv, lw['mla_kva_g'])
    kr = rmsnorm(k_r, lw['mla_kn_rope'])
    if ctx is None:
        ctx_out = (ka, va, ckv, kr)
        ka_all, va_all, ckv_all, kr_all = ka, va, ckv, kr
    else:
        ctx_out = None
        qa = apply_rope(qa, *rope_a)
        ka = apply_rope(ka, *rope_a)
        qb_rope = apply_rope(qb_rope, *rope_b)
        kr = apply_rope(kr, *rope_b)
        ka_all = jnp.concatenate([ctx[0], ka], axis=2)
        va_all = jnp.concatenate([ctx[1], va], axis=2)
        ckv_all = jnp.concatenate([ctx[2], ckv], axis=1)
        kr_all = jnp.concatenate([ctx[3], kr], axis=1)
    o_a = block_attention(qa.reshape(b, KV_A, H_A // KV_A, s, HD_A), ka_all, va_all, HD_A ** -0.5)
    o_a = o_a.reshape(b, H_A, s, HD_A).transpose(0, 2, 1, 3).reshape(b, s, H_A * HD_A)
    n_keys = ckv_all.shape[1]
    kvb = (ckv_all @ lw['w_kvb']).reshape(b, n_keys, H_B, NOPE_B + VD_B).transpose(0, 2, 1, 3)
    kb_nope = rmsnorm(kvb[..., :NOPE_B], lw['mla_kn_nope'])
    vb = kvb[..., NOPE_B:]
    kb = jnp.concatenate([kb_nope, jnp.broadcast_to(kr_all[:, None], (b, H_B, n_keys, ROPE_B))], axis=-1)
    qb_full = jnp.concatenate([qb_nope, qb_rope], axis=-1)
    o_b = block_attention(qb_full[:, :, None], kb, vb, (NOPE_B + ROPE_B) ** -0.5)
    o_b = o_b.reshape(b, H_B, s, VD_B).transpose(0, 2, 1, 3).reshape(b, s, H_B * VD_B)
    merged = jax.nn.sigmoid(g_a) * (o_a @ lw['w_oa']) + jax.nn.sigmoid(g_b) * (o_b @ lw['w_ob'])
    return merged @ lw['w_out'], ctx_out


def swiglu(x, w1, w3, w2):
    return (jax.nn.silu(x @ w1) * (x @ w3)) @ w2


def routed_experts(h, idx, w, w_e1, w_e3, w_e2):
    t, d = h.shape
    n = t * TOP_K
    flat_e = idx.reshape(n)
    flat_tok = jnp.repeat(jnp.arange(t, dtype=jnp.int32), TOP_K)
    flat_w = w.reshape(n)
    order = jnp.argsort(flat_e)
    se, stok, sw = flat_e[order], flat_tok[order], flat_w[order]
    counts = jnp.bincount(flat_e, length=N_EXPERTS)
    padded = (counts + EXPERT_BLOCK - 1) // EXPERT_BLOCK * EXPERT_BLOCK
    pad_end = jnp.cumsum(padded)
    pad_start = pad_end - padded
    start = jnp.cumsum(counts) - counts
    dest = pad_start[se] + (jnp.arange(n) - start[se])
    n_blocks = -(-n // EXPERT_BLOCK) + N_EXPERTS
    n_rows = n_blocks * EXPERT_BLOCK
    xd = jnp.zeros((n_rows, d), h.dtype).at[dest].set(h[stok])
    wd = jnp.zeros((n_rows,), h.dtype).at[dest].set(sw)
    td = jnp.zeros((n_rows,), jnp.int32).at[dest].set(stok)
    blk_e = jnp.minimum(jnp.searchsorted(pad_end, jnp.arange(n_blocks) * EXPERT_BLOCK, side='right'), N_EXPERTS - 1)

    def run_block(args):
        xb, e = args
        return swiglu(xb, w_e1[e], w_e3[e], w_e2[e])

    yd = lax.map(run_block, (xd.reshape(n_blocks, EXPERT_BLOCK, d), blk_e)).reshape(n_rows, d)
    return jax.ops.segment_sum(yd * wd[:, None], td, num_segments=t)


def moe_ffn(h, lw):
    t = h.shape[0]
    scores = jax.nn.sigmoid(h.astype(jnp.float32) @ lw['w_router'].astype(jnp.float32))
    biased = scores + lw['router_bias'].astype(jnp.float32)
    grp = biased.reshape(t, N_GROUPS, N_EXPERTS // N_GROUPS)
    grp_score = jnp.sum(lax.top_k(grp, 2)[0], axis=-1)
    _, top_g = lax.top_k(grp_score, TOPK_GROUPS)
    gmask = jnp.sum(jax.nn.one_hot(top_g, N_GROUPS, dtype=jnp.int32), axis=-2) > 0
    emask = jnp.repeat(gmask, N_EXPERTS // N_GROUPS, axis=-1)
    _, idx = lax.top_k(jnp.where(emask, biased, -jnp.inf), TOP_K)
    w = jnp.take_along_axis(scores, idx, axis=-1)
    w = (w / jnp.sum(w, axis=-1, keepdims=True) * ROUTED_SCALE).astype(h.dtype)
    routed = routed_experts(h, idx, w, lw['w_e1'], lw['w_e3'], lw['w_e2'])
    return routed + swiglu(h, lw['w_s1'], lw['w_s3'], lw['w_s2'])


def trunk_layer(x, cond, lw, rope_a, rope_b, ctx):
    mod = (jax.nn.silu(cond) @ lw['w_mod'] + lw['b_mod'])[:, None, :]
    sh1, sc1, g1, sh2, sc2, g2 = jnp.split(mod, N_MOD, axis=-1)
    h = rmsnorm(x, lw['norm1_g']) * (1 + sc1) + sh1
    a, ctx_out = mixing_sublayer(h, lw, rope_a, rope_b, ctx)
    x = x + g1 * a
    h = rmsnorm(x, lw['norm2_g']) * (1 + sc2) + sh2
    b, s, d = h.shape
    x = x + g2 * moe_ffn(h.reshape(b * s, d), lw).reshape(b, s, d)
    return x, ctx_out


def _normal(k, shape, scale):
    return jax.random.normal(k, shape, jnp.float32) * scale


def _gain(k, shape):
    return 1.0 + 0.02 * jax.random.normal(k, shape, jnp.float32)


def setup_inputs(seed: int = 0) -> dict:
    key = jax.random.key(seed)
    ks = iter(jax.random.split(key, 40))
    L, D, E = DEPTH, D_MODEL, N_EXPERTS
    return {
        'x_prompt': _normal(next(ks), (BATCH, SEQ, D), 1.0),
        'x_sample': _normal(next(ks), (DEC_BATCH, DEC_SEQ, D), 1.0),
        'c': _normal(next(ks), (DEC_BATCH, D), 1.0),
        'cache_gqa_k': _normal(next(ks), (DEC_BATCH, L, KV_A, PAST_LEN, HD_A), 1.0),
        'cache_gqa_v': _normal(next(ks), (DEC_BATCH, L, KV_A, PAST_LEN, HD_A), 1.0),
        'cache_mla_ckv': _normal(next(ks), (DEC_BATCH, L, PAST_LEN, KV_LORA), 1.0),
        'cache_mla_krope': _normal(next(ks), (DEC_BATCH, L, PAST_LEN, ROPE_B), 1.0),
        'c_ctx': _normal(next(ks), (D,), 1.0),
        'w_mod': _normal(next(ks), (L, D, N_MOD * D), 0.5 * D ** -0.5),
        'b_mod': _normal(next(ks), (L, N_MOD * D), 0.02),
        'norm1_g': _gain(next(ks), (L, D)),
        'norm2_g': _gain(next(ks), (L, D)),
        'w_in': _normal(next(ks), (L, D, D_IN), D ** -0.5),
        'gqa_qn': _gain(next(ks), (L, HD_A)),
        'gqa_kn': _gain(next(ks), (L, HD_A)),
        'mla_qa_g': _gain(next(ks), (L, Q_LORA)),
        'mla_kva_g': _gain(next(ks), (L, KV_LORA)),
        'w_qb': _normal(next(ks), (L, Q_LORA, H_B * (NOPE_B + ROPE_B)), Q_LORA ** -0.5),
        'w_kvb': _normal(next(ks), (L, KV_LORA, H_B * (NOPE_B + VD_B)), KV_LORA ** -0.5),
        'mla_qn_nope': _gain(next(ks), (L, NOPE_B)),
        'mla_qn_rope': _gain(next(ks), (L, ROPE_B)),
        'mla_kn_nope': _gain(next(ks), (L, NOPE_B)),
        'mla_kn_rope': _gain(next(ks), (L, ROPE_B)),
        'w_oa': _normal(next(ks), (L, H_A * HD_A, D), (H_A * HD_A) ** -0.5),
        'w_ob': _normal(next(ks), (L, H_B * VD_B, D), (H_B * VD_B) ** -0.5),
        'w_out': _normal(next(ks), (L, D, D), D ** -0.5),
        'w_router': _normal(next(ks), (L, D, E), D ** -0.5),
        'router_bias': _normal(next(ks), (L, E), 0.01),
        'w_e1': _normal(next(ks), (L, E, D, D_EXPERT), D ** -0.5),
        'w_e3': _normal(next(ks), (L, E, D, D_EXPERT), D ** -0.5),
        'w_e2': _normal(next(ks), (L, E, D_EXPERT, D), D_EXPERT ** -0.5),
        'w_s1': _normal(next(ks), (L, D, D_SHARED), D ** -0.5),
        'w_s3': _normal(next(ks), (L, D, D_SHARED), D ** -0.5),
        'w_s2': _normal(next(ks), (L, D_SHARED, D), D_SHARED ** -0.5),
    }


def reference(x_prompt, x_sample, c, cache_gqa_k, cache_gqa_v, cache_mla_ckv, cache_mla_krope, c_ctx,
              w_mod, b_mod, norm1_g, norm2_g, w_in, gqa_qn, gqa_kn, mla_qa_g, mla_kva_g, w_qb, w_kvb,
              mla_qn_nope, mla_qn_rope, mla_kn_nope, mla_kn_rope, w_oa, w_ob, w_out, w_router, router_bias,
              w_e1, w_e3, w_e2, w_s1, w_s3, w_s2):
    n_lat = x_sample.shape[1]
    rope_a = axial_rope(n_lat, HD_A)
    rope_b = axial_rope(n_lat, ROPE_B)
    cond_ctx = c_ctx[None, :]
    xp, xs = x_prompt, x_sample
    ks, vs, ckvs, krs = [], [], [], []
    for l in range(DEPTH):
        lw = {
            'w_mod': w_mod[l], 'b_mod': b_mod[l], 'norm1_g': norm1_g[l], 'norm2_g': norm2_g[l],
            'w_in': w_in[l], 'gqa_qn': gqa_qn[l], 'gqa_kn': gqa_kn[l],
            'mla_qa_g': mla_qa_g[l], 'mla_kva_g': mla_kva_g[l], 'w_qb': w_qb[l], 'w_kvb': w_kvb[l],
            'mla_qn_nope': mla_qn_nope[l], 'mla_qn_rope': mla_qn_rope[l],
            'mla_kn_nope': mla_kn_nope[l], 'mla_kn_rope': mla_kn_rope[l],
            'w_oa': w_oa[l], 'w_ob': w_ob[l], 'w_out': w_out[l],
            'w_router': w_router[l], 'router_bias': router_bias[l],
            'w_e1': w_e1[l], 'w_e3': w_e3[l], 'w_e2': w_e2[l],
            'w_s1': w_s1[l], 'w_s3': w_s3[l], 'w_s2': w_s2[l],
        }
        xp, ctx_l = trunk_layer(xp, cond_ctx, lw, None, None, None)
        ks.append(ctx_l[0])
        vs.append(ctx_l[1])
        ckvs.append(ctx_l[2])
        krs.append(ctx_l[3])
        cached = (cache_gqa_k[:, l], cache_gqa_v[:, l], cache_mla_ckv[:, l], cache_mla_krope[:, l])
        xs, _ = trunk_layer(xs, c, lw, rope_a, rope_b, cached)
    new_gqa_k = jnp.stack(ks, axis=1)
    new_gqa_v = jnp.stack(vs, axis=1)
    new_mla_ckv = jnp.stack(ckvs, axis=1)
    new_mla_krope = jnp.stack(krs, axis=1)
    return (xp, xs, new_gqa_k, new_gqa_v, new_mla_ckv, new_mla_krope)
```

```python
import functools

import jax
import jax.numpy as jnp
from jax import lax
from jax.experimental import pallas as pl
from jax.experimental.pallas import tpu as pltpu

F32 = jnp.float32
BF16 = jnp.bfloat16
I32 = jnp.int32
HIGHEST = lax.Precision.HIGHEST

D = 1024
GRID_W = 64
RMS_EPS = 1e-6
ROPE_THETA = 10000.0
N_MOD = 6
H_A, KV_A, HD_A = 8, 2, 128
H_B, Q_LORA, KV_LORA, NOPE_B, ROPE_B, VD_B = 8, 256, 128, 128, 64, 128
N_EXPERTS, TOP_K, N_GROUPS, TOPK_GROUPS = 64, 8, 8, 4
GROUP_SIZE = N_EXPERTS // N_GROUPS
D_EXPERT = 256
ROUTED_SCALE = 2.5

LANE = 128
C_QA, C_KA, C_VA, C_CQ, C_CKV, C_KR, C_GA, C_GB, C_END = 0, 1024, 1280, 1536, 1792, 1920, 2048, 3072, 4096
VMEM_LIMIT = 52 * 1024 * 1024

TM = 256
TQ = 256
TL = 256
TMO = 1024


def _dot(a, b):
    return jnp.dot(a, b, preferred_element_type=F32)


def _rms(t, n_valid):
    ms = jnp.sum(t * t, axis=-1, keepdims=True) * (1.0 / n_valid)
    return t * lax.rsqrt(ms + RMS_EPS)


def _silu(t):
    return t * jax.nn.sigmoid(t)


def _pad64(a):
    z = jnp.zeros(a.shape[:-1] + (32,), a.dtype)
    return jnp.concatenate([a[..., :32], z, a[..., 32:], z], axis=-1)


def _unpad64(a):
    return jnp.concatenate([a[..., :32], a[..., 64:96]], axis=-1)


def _const_spec(shape):
    nd = len(shape)
    return pl.BlockSpec(shape, lambda *_: (0,) * nd)


def _mod_kernel(c_ref, w_ref, b_ref, o_ref):
    s = _silu(c_ref[...])
    o_ref[...] = jnp.dot(s, w_ref[...], preferred_element_type=F32, precision=HIGHEST) + b_ref[...]


def _modulation(cond8, w_mod, b_mod):
    tn = 1024
    n = w_mod.shape[1]
    return pl.pallas_call(
        _mod_kernel,
        grid=(n // tn,),
        in_specs=[pl.BlockSpec((8, D), lambda j: (0, 0)),
                  pl.BlockSpec((D, tn), lambda j: (0, j)),
                  pl.BlockSpec((1, tn), lambda j: (0, j))],
        out_specs=pl.BlockSpec((8, tn), lambda j: (0, j)),
        out_shape=jax.ShapeDtypeStruct((8, n), F32),
        compiler_params=pltpu.CompilerParams(dimension_semantics=("arbitrary",), vmem_limit_bytes=VMEM_LIMIT),
        name="adaln_mod",
    )(cond8, w_mod, b_mod)


def _proj_kernel(use_rope, *refs):
    (x_ref, mod_ref, g1_ref, win_ref, wqn_ref, wqr_ref,
     gqn_ref, gkn_ref, gcq_ref, gckv_ref, gnope_ref, gqr_ref, gkr_ref) = refs[:13]
    if use_rope:
        ca_ref, sa_ref, cb_ref, sb_ref = refs[13:17]
        outs = refs[17:]
    else:
        outs = refs[13:]
    qa_ref, ka_ref, va_ref, qb_ref, ckv_ref, kr_ref, ga_ref, gb_ref = outs

    def rope(t, c_ref, s_ref):
        return t * c_ref[...] + pltpu.roll(t, LANE // 2, 1) * s_ref[...]

    rope_a = (lambda t: rope(t, ca_ref, sa_ref)) if use_rope else (lambda t: t)
    rope_b = (lambda t: rope(t, cb_ref, sb_ref)) if use_rope else (lambda t: t)

    x = x_ref[0]
    m = mod_ref[0]
    sh1, sc1 = m[0:1], m[1:2]
    h = (_rms(x, D) * g1_ref[...]) * (1.0 + sc1) + sh1
    hb = h.astype(BF16)

    def proj(lo, hi):
        return _dot(hb, win_ref[:, lo:hi])

    zq = proj(C_QA, C_KA)
    for hh in range(H_A):
        t = _rms(zq[:, hh * HD_A:(hh + 1) * HD_A], HD_A) * gqn_ref[...]
        qa_ref[0, hh] = rope_a(t).astype(BF16)
    zk = proj(C_KA, C_VA)
    zv = proj(C_VA, C_CQ)
    for j in range(KV_A):
        t = _rms(zk[:, j * HD_A:(j + 1) * HD_A], HD_A) * gkn_ref[...]
        ka_ref[0, j] = rope_a(t)
        va_ref[0, j] = zv[:, j * HD_A:(j + 1) * HD_A]

    cq = (_rms(proj(C_CQ, C_CKV), Q_LORA) * gcq_ref[...]).astype(BF16)
    qn = _dot(cq, wqn_ref[...])
    qr = _dot(cq, wqr_ref[...])
    for hh in range(H_B):
        t = _rms(qn[:, hh * LANE:(hh + 1) * LANE], NOPE_B) * gnope_ref[...]
        qb_ref[0, hh, :, 0:LANE] = t.astype(BF16)
        t = _rms(qr[:, hh * LANE:(hh + 1) * LANE], ROPE_B) * gqr_ref[...]
        qb_ref[0, hh, :, LANE:2 * LANE] = rope_b(t).astype(BF16)

    ckv_ref[0] = _rms(proj(C_CKV, C_KR), KV_LORA) * gckv_ref[...]
    kr_ref[0] = rope_b(_rms(proj(C_KR, C_GA), ROPE_B) * gkr_ref[...])
    ga_ref[0] = jax.nn.sigmoid(proj(C_GA, C_GB))
    gb_ref[0] = jax.nn.sigmoid(proj(C_GB, C_END))


def _project(x, modsel, per_batch, w, rope_tabs):
    B, S, _ = x.shape
    use_rope = rope_tabs is not None
    mod_idx = (lambda b, i: (b, 0, 0)) if per_batch else (lambda b, i: (0, 0, 0))
    in_specs = [pl.BlockSpec((1, TM, D), lambda b, i: (b, i, 0)),
                pl.BlockSpec((1, N_MOD, D), mod_idx),
                _const_spec((1, D)), _const_spec((D, C_END)), _const_spec((Q_LORA, H_B * LANE)),
                _const_spec((Q_LORA, H_B * LANE)),
                _const_spec((1, LANE)), _const_spec((1, LANE)), _const_spec((1, Q_LORA)), _const_spec((1, LANE)),
                _const_spec((1, LANE)), _const_spec((1, LANE)), _const_spec((1, LANE))]
    args = [x, modsel, w["norm1_g"], w["w_in"], w["w_qn"], w["w_qr"], w["gqa_qn"], w["gqa_kn"], w["mla_qa_g"],
            w["mla_kva_g"], w["mla_qn_nope"], w["mla_qn_rope"], w["mla_kn_rope"]]
    if use_rope:
        in_specs += [pl.BlockSpec((TM, LANE), lambda b, i: (i, 0))] * 4
        args += list(rope_tabs)
    tok4 = lambda h, d: pl.BlockSpec((1, h, TM, d), lambda b, i: (b, 0, i, 0))
    tok3 = lambda d: pl.BlockSpec((1, TM, d), lambda b, i: (b, i, 0))
    out_specs = [tok4(H_A, HD_A), tok4(KV_A, HD_A), tok4(KV_A, HD_A), tok4(H_B, 2 * LANE),
                 tok3(KV_LORA), tok3(LANE), tok3(D), tok3(D)]
    out_shape = [jax.ShapeDtypeStruct((B, H_A, S, HD_A), BF16),
                 jax.ShapeDtypeStruct((B, KV_A, S, HD_A), F32),
                 jax.ShapeDtypeStruct((B, KV_A, S, HD_A), F32),
                 jax.ShapeDtypeStruct((B, H_B, S, 2 * LANE), BF16),
                 jax.ShapeDtypeStruct((B, S, KV_LORA), F32),
                 jax.ShapeDtypeStruct((B, S, LANE), F32),
                 jax.ShapeDtypeStruct((B, S, D), F32),
                 jax.ShapeDtypeStruct((B, S, D), F32)]
    return pl.pallas_call(
        functools.partial(_proj_kernel, use_rope),
        grid=(B, S // TM),
        in_specs=in_specs, out_specs=out_specs, out_shape=out_shape,
        compiler_params=pltpu.CompilerParams(dimension_semantics=("arbitrary", "arbitrary"),
                                             vmem_limit_bytes=VMEM_LIMIT),
        name="proj_rope" if use_rope else "proj",
    )(*args)


def _kvb_kernel(ckv_ref, kr_ref, wk_ref, wv_ref, gk_ref, kb_ref, vb_ref):
    c = ckv_ref[0].astype(BF16)
    k = _dot(c, wk_ref[...])
    v = _dot(c, wv_ref[...])
    krb = kr_ref[0].astype(BF16)
    for hh in range(H_B):
        t = _rms(k[:, hh * LANE:(hh + 1) * LANE], NOPE_B) * gk_ref[...]
        kb_ref[0, hh, :, 0:LANE] = t.astype(BF16)
        kb_ref[0, hh, :, LANE:2 * LANE] = krb
        vb_ref[0, hh] = v[:, hh * LANE:(hh + 1) * LANE].astype(BF16)


def _kv_up(ckv_all, krp_all, w):
    B, L, _ = ckv_all.shape
    return pl.pallas_call(
        _kvb_kernel,
        grid=(B, L // TL),
        in_specs=[pl.BlockSpec((1, TL, KV_LORA), lambda b, i: (b, i, 0)),
                  pl.BlockSpec((1, TL, LANE), lambda b, i: (b, i, 0)),
                  _const_spec((KV_LORA, H_B * LANE)), _const_spec((KV_LORA, H_B * LANE)), _const_spec((1, LANE))],
        out_specs=[pl.BlockSpec((1, H_B, TL, 2 * LANE), lambda b, i: (b, 0, i, 0)),
                   pl.BlockSpec((1, H_B, TL, VD_B), lambda b, i: (b, 0, i, 0))],
        out_shape=[jax.ShapeDtypeStruct((B, H_B, L, 2 * LANE), BF16),
                   jax.ShapeDtypeStruct((B, H_B, L, VD_B), BF16)],
        compiler_params=pltpu.CompilerParams(dimension_semantics=("arbitrary", "arbitrary"),
                                             vmem_limit_bytes=VMEM_LIMIT),
        name="kv_up",
    )(ckv_all, krp_all, w["w_kvk"], w["w_kvv"], w["mla_kn_nope"])


def _attn_kernel(n_heads, group, scale, q_ref, k_ref, v_ref, o_ref):
    dv = v_ref.shape[-1]
    for hh in range(n_heads):
        q = q_ref[0, hh]
        k = k_ref[0, hh // group]
        v = v_ref[0, hh // group]
        s = lax.dot_general(q, k, (((1,), (1,)), ((), ())), preferred_element_type=F32) * scale
        p = jnp.exp(s - jnp.max(s, axis=-1, keepdims=True))
        den = jnp.sum(p, axis=-1, keepdims=True)
        o = _dot(p.astype(BF16), v)
        o_ref[0, :, hh * dv:(hh + 1) * dv] = (o / den).astype(BF16)


def _attention(q, k, v, scale, name):
    B, H, S, dk = q.shape
    _, Hk, L, dv = v.shape
    return pl.pallas_call(
        functools.partial(_attn_kernel, H, H // Hk, scale),
        grid=(B, S // TQ),
        in_specs=[pl.BlockSpec((1, H, TQ, dk), lambda b, i: (b, 0, i, 0)),
                  pl.BlockSpec((1, Hk, L, dk), lambda b, i: (b, 0, 0, 0)),
                  pl.BlockSpec((1, Hk, L, dv), lambda b, i: (b, 0, 0, 0))],
        out_specs=pl.BlockSpec((1, TQ, H * dv), lambda b, i: (b, i, 0)),
        out_shape=jax.ShapeDtypeStruct((B, S, H * dv), BF16),
        compiler_params=pltpu.CompilerParams(dimension_semantics=("arbitrary", "arbitrary"),
                                             vmem_limit_bytes=VMEM_LIMIT),
        name=name,
    )(q, k, v)


def _route(bi, sc):
    n_tok = bi.shape[1]
    iota8 = lax.broadcasted_iota(I32, (GROUP_SIZE, n_tok), 0)
    neg = -jnp.inf
    grp = [bi[j * GROUP_SIZE:(j + 1) * GROUP_SIZE] for j in range(N_GROUPS)]
    gscore = []
    for g in grp:
        m1 = jnp.max(g, axis=0, keepdims=True)
        first = jnp.min(jnp.where(g == m1, iota8, GROUP_SIZE), axis=0, keepdims=True)
        m2 = jnp.max(jnp.where(iota8 == first, neg, g), axis=0, keepdims=True)
        gscore.append(m1 + m2)
    masked = []
    for j in range(N_GROUPS):
        cnt = jnp.zeros((1, n_tok), I32)
        for j2 in range(N_GROUPS):
            if j2 == j:
                continue
            beats = (gscore[j2] > gscore[j]) if j2 > j else (gscore[j2] >= gscore[j])
            cnt = cnt + beats.astype(I32)
        keep = jnp.broadcast_to(cnt, grp[j].shape) < TOPK_GROUPS
        masked.append(jnp.where(keep, grp[j], neg))
    cnts = [jnp.zeros((GROUP_SIZE, n_tok), I32) for _ in range(N_GROUPS)]
    for e2 in range(N_EXPERTS):
        j2, r2 = divmod(e2, GROUP_SIZE)
        row = masked[j2][r2:r2 + 1]
        for j in range(N_GROUPS):
            if j > j2:
                beats = (row >= masked[j]).astype(I32)
            elif j < j2:
                beats = (row > masked[j]).astype(I32)
            else:
                beats = jnp.where(iota8 > r2, (row >= masked[j]).astype(I32), (row > masked[j]).astype(I32))
            cnts[j] = cnts[j] + beats
    sel = [c < TOP_K for c in cnts]
    picked = [jnp.where(sel[j], sc[j * GROUP_SIZE:(j + 1) * GROUP_SIZE], 0.0) for j in range(N_GROUPS)]
    total = picked[0].sum(axis=0, keepdims=True)
    for j in range(1, N_GROUPS):
        total = total + picked[j].sum(axis=0, keepdims=True)
    return [picked[j] / total * ROUTED_SCALE for j in range(N_GROUPS)]


def _post_kernel(oa_ref, ob_ref, ga_ref, gb_ref, x_ref, mod_ref, g2n_ref, woa_ref, wob_ref, wout_ref,
                 wr_ref, rb_ref, ws1_ref, ws3_ref, ws2_ref, base_ref, h2_ref, wd_ref):
    m = mod_ref[0]
    g1, sh2, sc2, g2 = m[2:3], m[3:4], m[4:5], m[5:6]
    merged = ga_ref[0] * _dot(oa_ref[0], woa_ref[...]) + gb_ref[0] * _dot(ob_ref[0], wob_ref[...])
    x1 = x_ref[0] + g1 * _dot(merged.astype(BF16), wout_ref[...])
    h2 = (_rms(x1, D) * g2n_ref[...]) * (1.0 + sc2) + sh2
    h2b = h2.astype(BF16)
    h2_ref[0] = h2b
    act = _silu(_dot(h2b, ws1_ref[...])) * _dot(h2b, ws3_ref[...])
    base_ref[0] = x1 + g2 * _dot(act.astype(BF16), ws2_ref[...])
    logits = jnp.dot(h2, wr_ref[...], preferred_element_type=F32, precision=HIGHEST)
    sc = jax.nn.sigmoid(logits.T[0:N_EXPERTS])
    wts = _route(sc + rb_ref[...], sc)
    wts.append(jnp.zeros((LANE - N_EXPERTS, sc.shape[1]), F32))
    wd_ref[0] = jnp.concatenate(wts, axis=0).T


def _post(oa, ob, ga, gb, x, modsel, per_batch, w):
    B, S, _ = x.shape
    mod_idx = (lambda b, i: (b, 0, 0)) if per_batch else (lambda b, i: (0, 0, 0))
    tok = lambda d: pl.BlockSpec((1, TM, d), lambda b, i: (b, i, 0))
    return pl.pallas_call(
        _post_kernel,
        grid=(B, S // TM),
        in_specs=[tok(D), tok(D), tok(D), tok(D), tok(D), pl.BlockSpec((1, N_MOD, D), mod_idx),
                  _const_spec((1, D)), _const_spec((D, D)), _const_spec((D, D)), _const_spec((D, D)),
                  _const_spec((D, LANE)), _const_spec((N_EXPERTS, 1)),
                  _const_spec((D, D_EXPERT)), _const_spec((D, D_EXPERT)), _const_spec((D_EXPERT, D))],
        out_specs=[tok(D), tok(D), tok(LANE)],
        out_shape=[jax.ShapeDtypeStruct((B, S, D), F32), jax.ShapeDtypeStruct((B, S, D), BF16),
                   jax.ShapeDtypeStruct((B, S, LANE), F32)],
        compiler_params=pltpu.CompilerParams(dimension_semantics=("arbitrary", "arbitrary"),
                                             vmem_limit_bytes=VMEM_LIMIT),
        name="post",
    )(oa, ob, ga, gb, x, modsel, w["norm2_g"], w["w_oa"], w["w_ob"], w["w_out"], w["w_router"],
      w["router_bias"], w["w_s1"], w["w_s3"], w["w_s2"])


def _moe_kernel(h_ref, wd_ref, base_ref, mod_ref, w1_ref, w3_ref, w2_ref, o_ref):
    e = pl.program_id(1)

    @pl.when(e == 0)
    def _():
        o_ref[...] = base_ref[...]

    h = h_ref[...]
    act = _silu(_dot(h, w1_ref[0].astype(BF16))) * _dot(h, w3_ref[0].astype(BF16))
    y = _dot(act.astype(BF16), w2_ref[0].astype(BF16))
    wd = wd_ref[...]
    lane = lax.broadcasted_iota(I32, wd.shape, 1)
    wcol = jnp.sum(jnp.where(lane == e, wd, 0.0), axis=-1, keepdims=True)
    g2 = mod_ref[0][5:6]
    o_ref[...] += g2 * (wcol * y)


def _moe(h2, wd, base, modsel, per_batch, seq, w):
    T = h2.shape[0]
    assert T % TMO == 0 and (not per_batch or seq % TMO == 0)
    mod_idx = (lambda i, e: ((i * TMO) // seq, 0, 0)) if per_batch else (lambda i, e: (0, 0, 0))
    return pl.pallas_call(
        _moe_kernel,
        grid=(T // TMO, N_EXPERTS),
        in_specs=[pl.BlockSpec((TMO, D), lambda i, e: (i, 0)),
                  pl.BlockSpec((TMO, LANE), lambda i, e: (i, 0)),
                  pl.BlockSpec((TMO, D), lambda i, e: (i, 0)),
                  pl.BlockSpec((1, N_MOD, D), mod_idx),
                  pl.BlockSpec((1, D, D_EXPERT), lambda i, e: (e, 0, 0)),
                  pl.BlockSpec((1, D, D_EXPERT), lambda i, e: (e, 0, 0)),
                  pl.BlockSpec((1, D_EXPERT, D), lambda i, e: (e, 0, 0))],
        out_specs=pl.BlockSpec((TMO, D), lambda i, e: (i, 0)),
        out_shape=jax.ShapeDtypeStruct((T, D), F32),
        compiler_params=pltpu.CompilerParams(dimension_semantics=("arbitrary", "arbitrary"),
                                             vmem_limit_bytes=VMEM_LIMIT),
        name="moe_dense",
    )(h2, wd, base, modsel, w["w_e1"], w["w_e3"], w["w_e2"])


def _trunk_pass(x, modsel, per_batch, w, rope_tabs, cache):
    B, S, _ = x.shape
    qa, ka, va, qb, ckv, krp, ga, gb = _project(x, modsel, per_batch, w, rope_tabs)
    if cache is None:
        ka_all, va_all, ckv_all, krp_all = ka, va, ckv, krp
    else:
        ck, cv, cckv, ckr = cache
        ka_all = jnp.concatenate([ck, ka], axis=2)
        va_all = jnp.concatenate([cv, va], axis=2)
        ckv_all = jnp.concatenate([cckv, ckv], axis=1)
        krp_all = jnp.concatenate([_pad64(ckr), krp], axis=1)
    kb, vb = _kv_up(ckv_all, krp_all, w)
    oa = _attention(qa, ka_all.astype(BF16), va_all.astype(BF16), HD_A ** -0.5, "attn_gqa")
    ob = _attention(qb, kb, vb, (NOPE_B + ROPE_B) ** -0.5, "attn_mla")
    base, h2, wd = _post(oa, ob, ga, gb, x, modsel, per_batch, w)
    T = B * S
    y = _moe(h2.reshape(T, D), wd.reshape(T, LANE), base.reshape(T, D), modsel, per_batch, S, w)
    return y.reshape(B, S, D), (ka, va, ckv, _unpad64(krp))


def _rope_tables(n_tokens):
    t = jnp.arange(n_tokens)
    row = (t // GRID_W).astype(F32)[:, None]
    col = (t % GRID_W).astype(F32)[:, None]

    def tabs(rot_dim):
        axis_dim = rot_dim // 2
        inv = ROPE_THETA ** (-jnp.arange(0, axis_dim, 2, dtype=F32) / axis_dim)
        ang = jnp.concatenate([row * inv, col * inv], axis=-1)
        return jnp.cos(ang), jnp.sin(ang)

    cos_a, sin_a = tabs(HD_A)
    cos_b, sin_b = tabs(ROPE_B)
    return (jnp.concatenate([cos_a, cos_a], -1), jnp.concatenate([-sin_a, sin_a], -1),
            _pad64(jnp.concatenate([cos_b, cos_b], -1)), _pad64(jnp.concatenate([-sin_b, sin_b], -1)))


def _layer_weights(l, w_in, w_qb, w_kvb, w_router, router_bias, named):
    w = {k: v[l] for k, v in named.items()}
    wi = w_in[l]
    w["w_in"] = jnp.concatenate([wi[:, :C_KR], _pad64(wi[:, C_KR:C_KR + ROPE_B]), wi[:, C_KR + ROPE_B:]],
                                axis=1).astype(BF16)
    wq = w_qb[l].reshape(Q_LORA, H_B, NOPE_B + ROPE_B)
    w["w_qn"] = wq[:, :, :NOPE_B].reshape(Q_LORA, H_B * LANE).astype(BF16)
    w["w_qr"] = _pad64(wq[:, :, NOPE_B:]).reshape(Q_LORA, H_B * LANE).astype(BF16)
    wkv = w_kvb[l].reshape(KV_LORA, H_B, NOPE_B + VD_B)
    w["w_kvk"] = wkv[:, :, :NOPE_B].reshape(KV_LORA, H_B * LANE).astype(BF16)
    w["w_kvv"] = wkv[:, :, NOPE_B:].reshape(KV_LORA, H_B * LANE).astype(BF16)
    w["w_router"] = jnp.pad(w_router[l], ((0, 0), (0, LANE - N_EXPERTS)))
    w["router_bias"] = router_bias[l].reshape(N_EXPERTS, 1)
    for k in ("norm1_g", "norm2_g", "gqa_qn", "gqa_kn", "mla_qa_g", "mla_kva_g", "mla_qn_nope", "mla_kn_nope"):
        w[k] = w[k].reshape(1, -1)
    for k in ("mla_qn_rope", "mla_kn_rope"):
        w[k] = _pad64(w[k]).reshape(1, LANE)
    for k in ("w_oa", "w_ob", "w_out", "w_s1", "w_s3", "w_s2"):
        w[k] = w[k].astype(BF16)
    return w


def kernel(x_prompt, x_sample, c, cache_gqa_k, cache_gqa_v, cache_mla_ckv, cache_mla_krope, c_ctx, w_mod, b_mod, norm1_g, norm2_g, w_in, gqa_qn, gqa_kn, mla_qa_g, mla_kva_g, w_qb, w_kvb, mla_qn_nope, mla_qn_rope, mla_kn_nope, mla_kn_rope, w_oa, w_ob, w_out, w_router, router_bias, w_e1, w_e3, w_e2, w_s1, w_s3, w_s2):
    depth = w_mod.shape[0]
    n_dec = x_sample.shape[0]
    rope_tabs = _rope_tables(x_sample.shape[1])
    cond8 = jnp.concatenate([c_ctx[None, :], c, jnp.zeros((8 - 1 - n_dec, D), F32)], axis=0)
    named = dict(norm1_g=norm1_g, norm2_g=norm2_g, gqa_qn=gqa_qn, gqa_kn=gqa_kn, mla_qa_g=mla_qa_g,
                 mla_kva_g=mla_kva_g, mla_qn_nope=mla_qn_nope, mla_qn_rope=mla_qn_rope, mla_kn_nope=mla_kn_nope,
                 mla_kn_rope=mla_kn_rope, w_oa=w_oa, w_ob=w_ob, w_out=w_out, w_e1=w_e1, w_e3=w_e3, w_e2=w_e2,
                 w_s1=w_s1, w_s3=w_s3, w_s2=w_s2)
    xp, xs = x_prompt, x_sample
    ks, vs, ckvs, krs = [], [], [], []
    for l in range(depth):
        w = _layer_weights(l, w_in, w_qb, w_kvb, w_router, router_bias, named)
        mod = _modulation(cond8, w_mod[l], b_mod[l].reshape(1, -1))
        mod_ctx = mod[0:1].reshape(1, N_MOD, D)
        mod_dec = mod[1:1 + n_dec].reshape(n_dec, N_MOD, D)
        xp, ctx = _trunk_pass(xp, mod_ctx, False, w, None, None)
        ks.append(ctx[0])
        vs.append(ctx[1])
        ckvs.append(ctx[2])
        krs.append(ctx[3])
        cache = (cache_gqa_k[:, l], cache_gqa_v[:, l], cache_mla_ckv[:, l], cache_mla_krope[:, l])
        xs, _ = _trunk_pass(xs, mod_dec, True, w, rope_tabs, cache)
    return (xp, xs, jnp.stack(ks, axis=1), jnp.stack(vs, axis=1), jnp.stack(ckvs, axis=1), jnp.stack(krs, axis=1))
```

```python
import functools
import math

import jax
import jax.numpy as jnp
from jax import lax
from jax.experimental import pallas as pl
from jax.experimental.pallas import tpu as pltpu

F32 = jnp.float32
BF16 = jnp.bfloat16
I32 = jnp.int32
HIGHEST = lax.Precision.HIGHEST

D = 1024
GRID_W = 64
RMS_EPS = 1e-6
ROPE_THETA = 10000.0
N_MOD = 6
H_A, KV_A, HD_A = 8, 2, 128
H_B, Q_LORA, KV_LORA, NOPE_B, ROPE_B, VD_B = 8, 256, 128, 128, 64, 128
N_EXPERTS, TOP_K, N_GROUPS, TOPK_GROUPS = 64, 8, 8, 4
GROUP_SIZE = N_EXPERTS // N_GROUPS
D_EXPERT = 256
ROUTED_SCALE = 2.5
QSCALE_A = HD_A ** -0.5 * math.log2(math.e)
QSCALE_B = (NOPE_B + ROPE_B) ** -0.5 * math.log2(math.e)

LANE = 128
C_QA, C_KA, C_VA, C_CQ, C_CKV, C_KR, C_GA, C_GB, C_END = 0, 1024, 1280, 1536, 1792, 1920, 2048, 3072, 4096
VMEM_LIMIT = 52 * 1024 * 1024

TM = 256
TQ = 256
TL = 256
GROUP_TOKENS = 4096
MB, MB_LOG2 = 128, 7
HP_ROWS = D // LANE
OUT_ROWS = D // LANE
S_IN = MB + 1
S_OUT = MB + 1
RMW_BATCH = 8


def _dot(a, b):
    return jnp.dot(a, b, preferred_element_type=F32)


def _rms(t, n_valid):
    ms = jnp.sum(t * t, axis=-1, keepdims=True) * (1.0 / n_valid)
    return t * lax.rsqrt(ms + RMS_EPS)


def _silu(t):
    return t * jax.nn.sigmoid(t)


def _pad64(a):
    z = jnp.zeros(a.shape[:-1] + (32,), a.dtype)
    return jnp.concatenate([a[..., :32], z, a[..., 32:], z], axis=-1)


def _unpad64(a):
    return jnp.concatenate([a[..., :32], a[..., 64:96]], axis=-1)


def _const_spec(shape):
    nd = len(shape)
    return pl.BlockSpec(shape, lambda *_: (0,) * nd)


def _mod_kernel(c_ref, w_ref, b_ref, o_ref):
    s = _silu(c_ref[...])
    o_ref[...] = jnp.dot(s, w_ref[...], preferred_element_type=F32, precision=HIGHEST) + b_ref[...]


def _modulation(cond8, w_mod, b_mod):
    tn = 1024
    n = w_mod.shape[1]
    return pl.pallas_call(
        _mod_kernel,
        grid=(n // tn,),
        in_specs=[pl.BlockSpec((8, D), lambda j: (0, 0)),
                  pl.BlockSpec((D, tn), lambda j: (0, j)),
                  pl.BlockSpec((1, tn), lambda j: (0, j))],
        out_specs=pl.BlockSpec((8, tn), lambda j: (0, j)),
        out_shape=jax.ShapeDtypeStruct((8, n), F32),
        compiler_params=pltpu.CompilerParams(dimension_semantics=("arbitrary",), vmem_limit_bytes=VMEM_LIMIT),
        name="adaln_mod",
    )(cond8, w_mod, b_mod)


def _proj_kernel(use_rope, *refs):
    (x_ref, mod_ref, g1_ref, win_ref, wqn_ref, wqr_ref,
     gqn_ref, gkn_ref, gcq_ref, gckv_ref, gnope_ref, gqr_ref, gkr_ref) = refs[:13]
    if use_rope:
        ca_ref, sa_ref, cb_ref, sb_ref = refs[13:17]
        outs = refs[17:]
    else:
        outs = refs[13:]
    qa_ref, ka_ref, va_ref, qb_ref, ckv_ref, kr_ref, ga_ref, gb_ref = outs

    def rope(t, c_ref, s_ref):
        return t * c_ref[...] + pltpu.roll(t, LANE // 2, 1) * s_ref[...]

    rope_a = (lambda t: rope(t, ca_ref, sa_ref)) if use_rope else (lambda t: t)
    rope_b = (lambda t: rope(t, cb_ref, sb_ref)) if use_rope else (lambda t: t)

    x = x_ref[0]
    m = mod_ref[0]
    sh1, sc1 = m[0:1], m[1:2]
    h = (_rms(x, D) * g1_ref[...]) * (1.0 + sc1) + sh1
    hb = h.astype(BF16)

    def proj(lo, hi):
        return _dot(hb, win_ref[:, lo:hi])

    zq = proj(C_QA, C_KA)
    for hh in range(H_A):
        t = _rms(zq[:, hh * HD_A:(hh + 1) * HD_A], HD_A) * gqn_ref[...]
        qa_ref[0, hh] = (rope_a(t) * QSCALE_A).astype(BF16)
    zk = proj(C_KA, C_VA)
    zv = proj(C_VA, C_CQ)
    for j in range(KV_A):
        t = _rms(zk[:, j * HD_A:(j + 1) * HD_A], HD_A) * gkn_ref[...]
        ka_ref[0, j] = rope_a(t)
        va_ref[0, j] = zv[:, j * HD_A:(j + 1) * HD_A]

    cq = (_rms(proj(C_CQ, C_CKV), Q_LORA) * gcq_ref[...]).astype(BF16)
    qn = _dot(cq, wqn_ref[...])
    qr = _dot(cq, wqr_ref[...])
    for hh in range(H_B):
        t = _rms(qn[:, hh * LANE:(hh + 1) * LANE], NOPE_B) * gnope_ref[...]
        qb_ref[0, hh, :, 0:LANE] = (t * QSCALE_B).astype(BF16)
        t = _rms(qr[:, hh * LANE:(hh + 1) * LANE], ROPE_B) * gqr_ref[...]
        qb_ref[0, hh, :, LANE:2 * LANE] = (rope_b(t) * QSCALE_B).astype(BF16)

    ckv_ref[0] = _rms(proj(C_CKV, C_KR), KV_LORA) * gckv_ref[...]
    kr_ref[0] = rope_b(_rms(proj(C_KR, C_GA), ROPE_B) * gkr_ref[...])
    ga_ref[0] = jax.nn.sigmoid(proj(C_GA, C_GB))
    gb_ref[0] = jax.nn.sigmoid(proj(C_GB, C_END))


def _project(x, modsel, per_batch, w, rope_tabs):
    B, S, _ = x.shape
    use_rope = rope_tabs is not None
    mod_idx = (lambda b, i: (b, 0, 0)) if per_batch else (lambda b, i: (0, 0, 0))
    in_specs = [pl.BlockSpec((1, TM, D), lambda b, i: (b, i, 0)),
                pl.BlockSpec((1, N_MOD, D), mod_idx),
                _const_spec((1, D)), _const_spec((D, C_END)), _const_spec((Q_LORA, H_B * LANE)),
                _const_spec((Q_LORA, H_B * LANE)),
                _const_spec((1, LANE)), _const_spec((1, LANE)), _const_spec((1, Q_LORA)), _const_spec((1, LANE)),
                _const_spec((1, LANE)), _const_spec((1, LANE)), _const_spec((1, LANE))]
    args = [x, modsel, w["norm1_g"], w["w_in"], w["w_qn"], w["w_qr"], w["gqa_qn"], w["gqa_kn"], w["mla_qa_g"],
            w["mla_kva_g"], w["mla_qn_nope"], w["mla_qn_rope"], w["mla_kn_rope"]]
    if use_rope:
        in_specs += [pl.BlockSpec((TM, LANE), lambda b, i: (i, 0))] * 4
        args += list(rope_tabs)
    tok4 = lambda h, d: pl.BlockSpec((1, h, TM, d), lambda b, i: (b, 0, i, 0))
    tok3 = lambda d: pl.BlockSpec((1, TM, d), lambda b, i: (b, i, 0))
    out_specs = [tok4(H_A, HD_A), tok4(KV_A, HD_A), tok4(KV_A, HD_A), tok4(H_B, 2 * LANE),
                 tok3(KV_LORA), tok3(LANE), tok3(D), tok3(D)]
    out_shape = [jax.ShapeDtypeStruct((B, H_A, S, HD_A), BF16),
                 jax.ShapeDtypeStruct((B, KV_A, S, HD_A), F32),
                 jax.ShapeDtypeStruct((B, KV_A, S, HD_A), F32),
                 jax.ShapeDtypeStruct((B, H_B, S, 2 * LANE), BF16),
                 jax.ShapeDtypeStruct((B, S, KV_LORA), F32),
                 jax.ShapeDtypeStruct((B, S, LANE), F32),
                 jax.ShapeDtypeStruct((B, S, D), F32),
                 jax.ShapeDtypeStruct((B, S, D), F32)]
    return pl.pallas_call(
        functools.partial(_proj_kernel, use_rope),
        grid=(B, S // TM),
        in_specs=in_specs, out_specs=out_specs, out_shape=out_shape,
        compiler_params=pltpu.CompilerParams(dimension_semantics=("arbitrary", "arbitrary"),
                                             vmem_limit_bytes=VMEM_LIMIT),
        name="proj_rope" if use_rope else "proj",
    )(*args)


def _kvb_kernel(ckv_ref, kr_ref, wk_ref, wv_ref, gk_ref, kb_ref, vb_ref):
    c = ckv_ref[0].astype(BF16)
    k = _dot(c, wk_ref[...])
    v = _dot(c, wv_ref[...])
    krb = kr_ref[0].astype(BF16)
    for hh in range(H_B):
        t = _rms(k[:, hh * LANE:(hh + 1) * LANE], NOPE_B) * gk_ref[...]
        kb_ref[0, hh, :, 0:LANE] = t.astype(BF16)
        kb_ref[0, hh, :, LANE:2 * LANE] = krb
        vb_ref[0, hh, :, 0:VD_B] = v[:, hh * LANE:(hh + 1) * LANE].astype(BF16)
        vb_ref[0, hh, :, VD_B:2 * VD_B] = jnp.ones((v.shape[0], VD_B), BF16)


def _kv_up(ckv_all, krp_all, w):
    B, L, _ = ckv_all.shape
    return pl.pallas_call(
        _kvb_kernel,
        grid=(B, L // TL),
        in_specs=[pl.BlockSpec((1, TL, KV_LORA), lambda b, i: (b, i, 0)),
                  pl.BlockSpec((1, TL, LANE), lambda b, i: (b, i, 0)),
                  _const_spec((KV_LORA, H_B * LANE)), _const_spec((KV_LORA, H_B * LANE)), _const_spec((1, LANE))],
        out_specs=[pl.BlockSpec((1, H_B, TL, 2 * LANE), lambda b, i: (b, 0, i, 0)),
                   pl.BlockSpec((1, H_B, TL, 2 * VD_B), lambda b, i: (b, 0, i, 0))],
        out_shape=[jax.ShapeDtypeStruct((B, H_B, L, 2 * LANE), BF16),
                   jax.ShapeDtypeStruct((B, H_B, L, 2 * VD_B), BF16)],
        compiler_params=pltpu.CompilerParams(dimension_semantics=("arbitrary", "arbitrary"),
                                             vmem_limit_bytes=VMEM_LIMIT),
        name="kv_up",
    )(ckv_all, krp_all, w["w_kvk"], w["w_kvv"], w["mla_kn_nope"])


def _attn_kernel(n_heads, group, q_ref, k_ref, v_ref, o_ref):
    dv = v_ref.shape[-1] // 2
    for hh in range(n_heads):
        q = q_ref[0, hh]
        k = k_ref[0, hh // group]
        v = v_ref[0, hh // group]
        s = lax.dot_general(q, k, (((1,), (1,)), ((), ())), preferred_element_type=F32)
        p = jnp.exp2(s - jnp.max(s, axis=-1, keepdims=True))
        o = _dot(p.astype(BF16), v)
        o_ref[0, :, hh * dv:(hh + 1) * dv] = (o[:, :dv] / o[:, dv:]).astype(BF16)


def _attention(q, k, v, name):
    B, H, S, dk = q.shape
    _, Hk, L, dv2 = v.shape
    dv = dv2 // 2
    return pl.pallas_call(
        functools.partial(_attn_kernel, H, H // Hk),
        grid=(B, S // TQ),
        in_specs=[pl.BlockSpec((1, H, TQ, dk), lambda b, i: (b, 0, i, 0)),
                  pl.BlockSpec((1, Hk, L, dk), lambda b, i: (b, 0, 0, 0)),
                  pl.BlockSpec((1, Hk, L, dv2), lambda b, i: (b, 0, 0, 0))],
        out_specs=pl.BlockSpec((1, TQ, H * dv), lambda b, i: (b, i, 0)),
        out_shape=jax.ShapeDtypeStruct((B, S, H * dv), BF16),
        compiler_params=pltpu.CompilerParams(dimension_semantics=("arbitrary", "arbitrary"),
                                             vmem_limit_bytes=VMEM_LIMIT),
        name=name,
    )(q, k, v)


def _route(bi, sc):
    n_tok = bi.shape[1]
    iota8 = lax.broadcasted_iota(I32, (GROUP_SIZE, n_tok), 0)
    neg = -jnp.inf
    grp = [bi[j * GROUP_SIZE:(j + 1) * GROUP_SIZE] for j in range(N_GROUPS)]
    gscore = []
    for g in grp:
        m1 = jnp.max(g, axis=0, keepdims=True)
        first = jnp.min(jnp.where(g == m1, iota8, GROUP_SIZE), axis=0, keepdims=True)
        m2 = jnp.max(jnp.where(iota8 == first, neg, g), axis=0, keepdims=True)
        gscore.append(m1 + m2)
    masked = []
    for j in range(N_GROUPS):
        cnt = jnp.zeros((1, n_tok), I32)
        for j2 in range(N_GROUPS):
            if j2 == j:
                continue
            beats = (gscore[j2] > gscore[j]) if j2 > j else (gscore[j2] >= gscore[j])
            cnt = cnt + beats.astype(I32)
        keep = jnp.broadcast_to(cnt, grp[j].shape) < TOPK_GROUPS
        masked.append(jnp.where(keep, grp[j], neg))
    cnts = [jnp.zeros((GROUP_SIZE, n_tok), I32) for _ in range(N_GROUPS)]
    for e2 in range(N_EXPERTS):
        j2, r2 = divmod(e2, GROUP_SIZE)
        row = masked[j2][r2:r2 + 1]
        for j in range(N_GROUPS):
            if j > j2:
                beats = (row >= masked[j]).astype(I32)
            elif j < j2:
                beats = (row > masked[j]).astype(I32)
            else:
                beats = jnp.where(iota8 > r2, (row >= masked[j]).astype(I32), (row > masked[j]).astype(I32))
            cnts[j] = cnts[j] + beats
    picked = [jnp.where(cnts[j] < TOP_K, sc[j * GROUP_SIZE:(j + 1) * GROUP_SIZE], 0.0) for j in range(N_GROUPS)]
    total = picked[0].sum(axis=0, keepdims=True)
    for j in range(1, N_GROUPS):
        total = total + picked[j].sum(axis=0, keepdims=True)
    wts = [picked[j] / total * ROUTED_SCALE for j in range(N_GROUPS)]
    idx_rows = jnp.zeros((TOP_K, n_tok), F32)
    w_rows = jnp.zeros((TOP_K, n_tok), F32)
    for k in range(TOP_K):
        ia = jnp.zeros((GROUP_SIZE, n_tok), F32)
        wa = jnp.zeros((GROUP_SIZE, n_tok), F32)
        for j in range(N_GROUPS):
            hit = cnts[j] == k
            ia = ia + jnp.where(hit, (iota8 + j * GROUP_SIZE).astype(F32), 0.0)
            wa = wa + jnp.where(hit, wts[j], 0.0)
        idx_rows = jnp.where(iota8 == k, jnp.sum(ia, axis=0, keepdims=True), idx_rows)
        w_rows = jnp.where(iota8 == k, jnp.sum(wa, axis=0, keepdims=True), w_rows)
    return idx_rows, w_rows


def _post_kernel(oa_ref, ob_ref, ga_ref, gb_ref, x_ref, mod_ref, g2n_ref, woa_ref, wob_ref, wout_ref,
                 wr_ref, rb_ref, ws1_ref, ws3_ref, ws2_ref, base_ref, hp_ref, idx_ref, wk_ref):
    m = mod_ref[0]
    g1, sh2, sc2, g2 = m[2:3], m[3:4], m[4:5], m[5:6]
    merged = ga_ref[0] * _dot(oa_ref[0], woa_ref[...]) + gb_ref[0] * _dot(ob_ref[0], wob_ref[...])
    x1 = x_ref[0] + g1 * _dot(merged.astype(BF16), wout_ref[...])
    h2 = (_rms(x1, D) * g2n_ref[...]) * (1.0 + sc2) + sh2
    h2b = h2.astype(BF16)
    act = _silu(_dot(h2b, ws1_ref[...])) * _dot(h2b, ws3_ref[...])
    base_ref[0] = x1 + g2 * _dot(act.astype(BF16), ws2_ref[...])
    n_tok = h2.shape[0]
    for j in range(HP_ROWS):
        hp_ref[pl.ds(j, n_tok, stride=HP_ROWS), :] = h2[:, j * LANE:(j + 1) * LANE]
    logits = jnp.dot(h2, wr_ref[...], preferred_element_type=F32, precision=HIGHEST)
    sc = jax.nn.sigmoid(logits.T[0:N_EXPERTS])
    idx_rows, w_rows = _route(sc + rb_ref[...], sc)
    idx_ref[...] = idx_rows.astype(I32)
    wk_ref[...] = w_rows


def _post(oa, ob, ga, gb, x, modsel, per_batch, w):
    B, S, _ = x.shape
    nt = S // TM
    mod_idx = (lambda b, i: (b, 0, 0)) if per_batch else (lambda b, i: (0, 0, 0))
    tok = lambda d: pl.BlockSpec((1, TM, d), lambda b, i: (b, i, 0))
    return pl.pallas_call(
        _post_kernel,
        grid=(B, nt),
        in_specs=[tok(D), tok(D), tok(D), tok(D), tok(D), pl.BlockSpec((1, N_MOD, D), mod_idx),
                  _const_spec((1, D)), _const_spec((D, D)), _const_spec((D, D)), _const_spec((D, D)),
                  _const_spec((D, LANE)), _const_spec((N_EXPERTS, 1)),
                  _const_spec((D, D_EXPERT)), _const_spec((D, D_EXPERT)), _const_spec((D_EXPERT, D))],
        out_specs=[tok(D),
                   pl.BlockSpec((TM * HP_ROWS, LANE), lambda b, i: (b * nt + i, 0)),
                   pl.BlockSpec((TOP_K, TM), lambda b, i: (0, b * nt + i)),
                   pl.BlockSpec((TOP_K, TM), lambda b, i: (0, b * nt + i))],
        out_shape=[jax.ShapeDtypeStruct((B, S, D), F32),
                   jax.ShapeDtypeStruct((B * S * HP_ROWS, LANE), F32),
                   jax.ShapeDtypeStruct((TOP_K, B * S), I32),
                   jax.ShapeDtypeStruct((TOP_K, B * S), F32)],
        compiler_params=pltpu.CompilerParams(dimension_semantics=("arbitrary", "arbitrary"),
                                             vmem_limit_bytes=VMEM_LIMIT),
        name="post",
    )(oa, ob, ga, gb, x, modsel, w["norm2_g"], w["w_oa"], w["w_ob"], w["w_out"], w["w_router"],
      w["router_bias"], w["w_s1"], w["w_s3"], w["w_s2"])


def _sorted_dispatch(idx, wk, n_groups):
    T = idx.shape[1]
    gt = T // n_groups
    nr = TOP_K * gt
    stride = 2 * gt
    experts = jnp.arange(N_EXPERTS, dtype=I32)
    regroup = lambda a: a.reshape(TOP_K, n_groups, gt).transpose(1, 0, 2).reshape(n_groups, nr)
    eid = regroup(idx)
    keys = eid * stride + regroup(jnp.broadcast_to(jnp.arange(T, dtype=I32) % gt, idx.shape))
    counts = jnp.sum((eid[:, None, :] == experts[None, :, None]).astype(I32), axis=-1)
    n_spare = (-counts) % MB
    j = jnp.arange(MB, dtype=I32)
    unused = N_EXPERTS * stride
    spare = jnp.where(j[None, None, :] < n_spare[:, :, None], experts[None, :, None] * stride + gt + j, unused)
    keys = jnp.concatenate([keys, spare.reshape(n_groups, N_EXPERTS * MB)], axis=1)
    wts = jnp.concatenate([regroup(wk), jnp.zeros((n_groups, N_EXPERTS * MB), F32)], axis=1)
    skeys, swts = lax.sort((keys, wts), dimension=1, num_keys=1)
    bounds = jnp.concatenate([jnp.zeros((n_groups, 1), I32), jnp.cumsum(counts + n_spare, axis=1)], axis=1)
    return bounds, jnp.minimum(skeys % stride, gt), swts


def _moe_kernel(gt, seg_ref, tok_ref, wts_ref, hp_ref, w1_ref, w3_ref, w2_ref, o_ref,
                tin0_ref, tin1_ref, tout0_ref, tout1_ref, w1b_ref, w3b_ref, w2b_ref):
    tin = (tin0_ref, tin1_ref)
    tout = (tout0_ref, tout1_ref)
    e = pl.program_id(0)
    n_total = lax.shift_right_logical(seg_ref[N_EXPERTS], MB_LOG2)

    def gather(b, slot):
        r0 = b * MB
        for m in range(MB):
            t = tok_ref[r0 + m] & (gt - 1)
            slab = hp_ref[pl.ds(pl.multiple_of(t * HP_ROWS, HP_ROWS), HP_ROWS), :]
            tin[slot][pl.ds(m, HP_ROWS, stride=S_IN), :] = slab

    def scatter(b, slot, live):
        r0 = b * MB
        for b0 in range(0, MB, RMW_BATCH):
            pending = []
            for m in range(b0, b0 + RMW_BATCH):
                off = pl.multiple_of(tok_ref[r0 + m] * OUT_ROWS, OUT_ROWS)
                upd = tout[slot][pl.ds(m, OUT_ROWS, stride=S_OUT), :] * wts_ref[r0 + m]
                if live is not None:
                    upd = jnp.where(live, upd, 0.0)
                pending.append((off, o_ref[pl.ds(off, OUT_ROWS), :] + upd))
            for off, v in pending:
                o_ref[pl.ds(off, OUT_ROWS), :] = v

    @pl.when(e == 0)
    def _():
        o_ref[...] = jnp.zeros_like(o_ref)
        tout1_ref[...] = jnp.zeros_like(tout1_ref)
        gather(0, 0)

    w1b_ref[...] = w1_ref[0].astype(BF16)
    w3b_ref[...] = w3_ref[0].astype(BF16)
    w2b_ref[...] = w2_ref[0].astype(BF16)
    first = lax.shift_right_logical(seg_ref[e], MB_LOG2)
    n_blocks = lax.shift_right_logical(seg_ref[e + 1], MB_LOG2) - first

    def stages(b, slot):
        gather(jnp.minimum(b + 1, n_total - 1), 1 - slot)
        x = jnp.concatenate([tin[slot][pl.ds(j * S_IN, MB), :] for j in range(HP_ROWS)], axis=-1).astype(BF16)
        act = _silu(_dot(x, w1b_ref[...])) * _dot(x, w3b_ref[...])
        y = _dot(act.astype(BF16), w2b_ref[...])
        for j in range(OUT_ROWS):
            tout[slot][pl.ds(j * S_OUT, MB), :] = y[:, j * LANE:(j + 1) * LANE]
        scatter(jnp.maximum(b - 1, 0), 1 - slot, b > 0)

    def block(i, carry):
        b = first + i
        for slot in range(2):
            @pl.when(b & 1 == slot)
            def _(slot=slot):
                stages(b, slot)
        return carry

    lax.fori_loop(0, n_blocks, block, 0)

    @pl.when(e == N_EXPERTS - 1)
    def _():
        for slot in range(2):
            @pl.when((n_total - 1) & 1 == slot)
            def _(slot=slot):
                scatter(n_total - 1, slot, None)


def _moe(hp, group, gt, seg, tok, wts, w):
    grid_spec = pltpu.PrefetchScalarGridSpec(
        num_scalar_prefetch=3,
        grid=(N_EXPERTS,),
        in_specs=[pl.BlockSpec((gt * HP_ROWS, LANE), lambda e, *_: (group, 0), pipeline_mode=pl.Buffered(1)),
                  pl.BlockSpec((1, D, D_EXPERT), lambda e, *_: (e, 0, 0)),
                  pl.BlockSpec((1, D, D_EXPERT), lambda e, *_: (e, 0, 0)),
                  pl.BlockSpec((1, D_EXPERT, D), lambda e, *_: (e, 0, 0))],
        out_specs=pl.BlockSpec(((gt + 1) * OUT_ROWS, LANE), lambda e, *_: (0, 0), pipeline_mode=pl.Buffered(1)),
        scratch_shapes=[pltpu.VMEM((HP_ROWS * S_IN, LANE), F32), pltpu.VMEM((HP_ROWS * S_IN, LANE), F32),
                        pltpu.VMEM((OUT_ROWS * S_OUT, LANE), F32), pltpu.VMEM((OUT_ROWS * S_OUT, LANE), F32),
                        pltpu.VMEM((D, D_EXPERT), BF16), pltpu.VMEM((D, D_EXPERT), BF16),
                        pltpu.VMEM((D_EXPERT, D), BF16)])
    return pl.pallas_call(
        functools.partial(_moe_kernel, gt),
        grid_spec=grid_spec,
        out_shape=jax.ShapeDtypeStruct(((gt + 1) * OUT_ROWS, LANE), F32),
        compiler_params=pltpu.CompilerParams(dimension_semantics=("arbitrary",), vmem_limit_bytes=VMEM_LIMIT),
        name="moe_experts",
    )(seg, tok, wts, hp, w["w_e1"], w["w_e3"], w["w_e2"])


def _combine_kernel(n_groups, tiles_per_group, *refs):
    r_refs = refs[:n_groups]
    base_ref, mod_ref, o_ref = refs[n_groups:]
    tile = pl.program_id(0) * pl.num_programs(1) + pl.program_id(1)
    g2 = mod_ref[0][5:6]
    n_tok = base_ref.shape[1]
    for g, r_ref in enumerate(r_refs):
        @pl.when(tile // tiles_per_group == g)
        def _(r_ref=r_ref):
            for s in range(OUT_ROWS):
                cols = slice(s * LANE, (s + 1) * LANE)
                o_ref[0, :, cols] = base_ref[0, :, cols] + g2[:, cols] * r_ref[pl.ds(s, n_tok, stride=OUT_ROWS), :]


def _combine(routed, base, modsel, per_batch, gt):
    B, S, _ = base.shape
    nt = S // TM
    tiles_per_group = gt // TM
    mod_idx = (lambda b, i: (b, 0, 0)) if per_batch else (lambda b, i: (0, 0, 0))

    def r_spec(g):
        def idx(b, i):
            tile = b * nt + i
            return (jnp.where(tile // tiles_per_group == g, tile % tiles_per_group, 0), 0)
        return pl.BlockSpec((TM * OUT_ROWS, LANE), idx)

    return pl.pallas_call(
        functools.partial(_combine_kernel, len(routed), tiles_per_group),
        grid=(B, nt),
        in_specs=[r_spec(g) for g in range(len(routed))] + [
            pl.BlockSpec((1, TM, D), lambda b, i: (b, i, 0)),
            pl.BlockSpec((1, N_MOD, D), mod_idx)],
        out_specs=pl.BlockSpec((1, TM, D), lambda b, i: (b, i, 0)),
        out_shape=jax.ShapeDtypeStruct((B, S, D), F32),
        compiler_params=pltpu.CompilerParams(dimension_semantics=("arbitrary", "arbitrary"),
                                             vmem_limit_bytes=VMEM_LIMIT),
        name="moe_combine",
    )(*routed, base, modsel)


def _trunk_pass(x, modsel, per_batch, w, rope_tabs, cache):
    B, S, _ = x.shape
    qa, ka, va, qb, ckv, krp, ga, gb = _project(x, modsel, per_batch, w, rope_tabs)
    if cache is None:
        ka_all, va_all, ckv_all, krp_all = ka, va, ckv, krp
    else:
        ck, cv, cckv, ckr = cache
        ka_all = jnp.concatenate([ck, ka], axis=2)
        va_all = jnp.concatenate([cv, va], axis=2)
        ckv_all = jnp.concatenate([cckv, ckv], axis=1)
        krp_all = jnp.concatenate([_pad64(ckr), krp], axis=1)
    kb, vb = _kv_up(ckv_all, krp_all, w)
    va_ones = jnp.concatenate([va_all.astype(BF16), jnp.ones(va_all.shape, BF16)], axis=-1)
    oa = _attention(qa, ka_all.astype(BF16), va_ones, "attn_gqa")
    ob = _attention(qb, kb, vb, "attn_mla")
    base, hp, idx, wk = _post(oa, ob, ga, gb, x, modsel, per_batch, w)
    T = B * S
    gt = min(GROUP_TOKENS, T)
    assert T % gt == 0 and gt % TM == 0 and gt & (gt - 1) == 0 and gt >= MB
    seg, tok, wts = _sorted_dispatch(idx, wk, T // gt)
    routed = [_moe(hp, g, gt, seg[g], tok[g], wts[g], w) for g in range(T // gt)]
    y = _combine(routed, base, modsel, per_batch, gt)
    return y, (ka, va, ckv, _unpad64(krp))


def _rope_tables(n_tokens):
    t = jnp.arange(n_tokens)
    row = (t // GRID_W).astype(F32)[:, None]
    col = (t % GRID_W).astype(F32)[:, None]

    def tabs(rot_dim):
        axis_dim = rot_dim // 2
        inv = ROPE_THETA ** (-jnp.arange(0, axis_dim, 2, dtype=F32) / axis_dim)
        ang = jnp.concatenate([row * inv, col * inv], axis=-1)
        return jnp.cos(ang), jnp.sin(ang)

    cos_a, sin_a = tabs(HD_A)
    cos_b, sin_b = tabs(ROPE_B)
    return (jnp.concatenate([cos_a, cos_a], -1), jnp.concatenate([-sin_a, sin_a], -1),
            _pad64(jnp.concatenate([cos_b, cos_b], -1)), _pad64(jnp.concatenate([-sin_b, sin_b], -1)))


def _layer_weights(l, w_in, w_qb, w_kvb, w_router, router_bias, named):
    w = {k: v[l] for k, v in named.items()}
    wi = w_in[l]
    w["w_in"] = jnp.concatenate([wi[:, :C_KR], _pad64(wi[:, C_KR:C_KR + ROPE_B]), wi[:, C_KR + ROPE_B:]],
                                axis=1).astype(BF16)
    wq = w_qb[l].reshape(Q_LORA, H_B, NOPE_B + ROPE_B)
    w["w_qn"] = wq[:, :, :NOPE_B].reshape(Q_LORA, H_B * LANE).astype(BF16)
    w["w_qr"] = _pad64(wq[:, :, NOPE_B:]).reshape(Q_LORA, H_B * LANE).astype(BF16)
    wkv = w_kvb[l].reshape(KV_LORA, H_B, NOPE_B + VD_B)
    w["w_kvk"] = wkv[:, :, :NOPE_B].reshape(KV_LORA, H_B * LANE).astype(BF16)
    w["w_kvv"] = wkv[:, :, NOPE_B:].reshape(KV_LORA, H_B * LANE).astype(BF16)
    w["w_router"] = jnp.pad(w_router[l], ((0, 0), (0, LANE - N_EXPERTS)))
    w["router_bias"] = router_bias[l].reshape(N_EXPERTS, 1)
    for k in ("norm1_g", "norm2_g", "gqa_qn", "gqa_kn", "mla_qa_g", "mla_kva_g", "mla_qn_nope", "mla_kn_nope"):
        w[k] = w[k].reshape(1, -1)
    for k in ("mla_qn_rope", "mla_kn_rope"):
        w[k] = _pad64(w[k]).reshape(1, LANE)
    for k in ("w_oa", "w_ob", "w_out", "w_s1", "w_s3", "w_s2"):
        w[k] = w[k].astype(BF16)
    return w


def kernel(x_prompt, x_sample, c, cache_gqa_k, cache_gqa_v, cache_mla_ckv, cache_mla_krope, c_ctx, w_mod, b_mod, norm1_g, norm2_g, w_in, gqa_qn, gqa_kn, mla_qa_g, mla_kva_g, w_qb, w_kvb, mla_qn_nope, mla_qn_rope, mla_kn_nope, mla_kn_rope, w_oa, w_ob, w_out, w_router, router_bias, w_e1, w_e3, w_e2, w_s1, w_s3, w_s2):
    depth = w_mod.shape[0]
    n_dec = x_sample.shape[0]
    rope_tabs = _rope_tables(x_sample.shape[1])
    cond8 = jnp.concatenate([c_ctx[None, :], c, jnp.zeros((8 - 1 - n_dec, D), F32)], axis=0)
    named = dict(norm1_g=norm1_g, norm2_g=norm2_g, gqa_qn=gqa_qn, gqa_kn=gqa_kn, mla_qa_g=mla_qa_g,
                 mla_kva_g=mla_kva_g, mla_qn_nope=mla_qn_nope, mla_qn_rope=mla_qn_rope, mla_kn_nope=mla_kn_nope,
                 mla_kn_rope=mla_kn_rope, w_oa=w_oa, w_ob=w_ob, w_out=w_out, w_e1=w_e1, w_e3=w_e3, w_e2=w_e2,
                 w_s1=w_s1, w_s3=w_s3, w_s2=w_s2)
    xp, xs = x_prompt, x_sample
    ks, vs, ckvs, krs = [], [], [], []
    for l in range(depth):
        w = _layer_weights(l, w_in, w_qb, w_kvb, w_router, router_bias, named)
        mod = _modulation(cond8, w_mod[l], b_mod[l].reshape(1, -1))
        mod_ctx = mod[0:1].reshape(1, N_MOD, D)
        mod_dec = mod[1:1 + n_dec].reshape(n_dec, N_MOD, D)
        xp, ctx = _trunk_pass(xp, mod_ctx, False, w, None, None)
        ks.append(ctx[0])
        vs.append(ctx[1])
        ckvs.append(ctx[2])
        krs.append(ctx[3])
        cache = (cache_gqa_k[:, l], cache_gqa_v[:, l], cache_mla_ckv[:, l], cache_mla_krope[:, l])
        xs, _ = _trunk_pass(xs, mod_dec, True, w, rope_tabs, cache)
    return (xp, xs, jnp.stack(ks, axis=1), jnp.stack(vs, axis=1), jnp.stack(ckvs, axis=1), jnp.stack(krs, axis=1))
```

```python
import functools
import math

import jax
import jax.numpy as jnp
from jax import lax
from jax.experimental import pallas as pl
from jax.experimental.pallas import tpu as pltpu

F32 = jnp.float32
BF16 = jnp.bfloat16
I32 = jnp.int32
HIGHEST = lax.Precision.HIGHEST

D = 1024
GRID_W = 64
RMS_EPS = 1e-6
ROPE_THETA = 10000.0
N_MOD = 6
H_A, KV_A, HD_A = 8, 2, 128
H_B, Q_LORA, KV_LORA, NOPE_B, ROPE_B, VD_B = 8, 256, 128, 128, 64, 128
N_EXPERTS, TOP_K, N_GROUPS, TOPK_GROUPS = 64, 8, 8, 4
GROUP_SIZE = N_EXPERTS // N_GROUPS
D_EXPERT = 256
ROUTED_SCALE = 2.5
QSCALE_A = HD_A ** -0.5 * math.log2(math.e)
QSCALE_B = (NOPE_B + ROPE_B) ** -0.5 * math.log2(math.e)

LANE = 128
C_QA, C_KA, C_VA, C_CQ, C_CKV, C_KR, C_GA, C_GB, C_END = 0, 1024, 1280, 1536, 1792, 1920, 2048, 3072, 4096
VMEM_LIMIT = 52 * 1024 * 1024

TM = 256
TQ = 256
TL = 256
GROUP_TOKENS = 4096
MB, MB_LOG2 = 128, 7
HP_ROWS = D // LANE
OUT_ROWS = D // LANE
S_IN = MB + 1
S_OUT = MB + 1
RMW_BATCH = 8
LIST_CHUNK = 16


def _dot(a, b):
    return jnp.dot(a, b, preferred_element_type=F32)


def _rms(t, n_valid):
    ms = jnp.sum(t * t, axis=-1, keepdims=True) * (1.0 / n_valid)
    return t * lax.rsqrt(ms + RMS_EPS)


def _silu(t):
    return t * jax.nn.sigmoid(t)


def _pad64(a):
    z = jnp.zeros(a.shape[:-1] + (32,), a.dtype)
    return jnp.concatenate([a[..., :32], z, a[..., 32:], z], axis=-1)


def _unpad64(a):
    return jnp.concatenate([a[..., :32], a[..., 64:96]], axis=-1)


def _const_spec(shape):
    nd = len(shape)
    return pl.BlockSpec(shape, lambda *_: (0,) * nd)


def _mod_kernel(c_ref, w_ref, b_ref, o_ref):
    s = _silu(c_ref[...])
    o_ref[...] = jnp.dot(s, w_ref[...], preferred_element_type=F32, precision=HIGHEST) + b_ref[...]


def _modulation(cond8, w_mod, b_mod):
    tn = 1024
    n = w_mod.shape[1]
    return pl.pallas_call(
        _mod_kernel,
        grid=(n // tn,),
        in_specs=[pl.BlockSpec((8, D), lambda j: (0, 0)),
                  pl.BlockSpec((D, tn), lambda j: (0, j)),
                  pl.BlockSpec((1, tn), lambda j: (0, j))],
        out_specs=pl.BlockSpec((8, tn), lambda j: (0, j)),
        out_shape=jax.ShapeDtypeStruct((8, n), F32),
        compiler_params=pltpu.CompilerParams(dimension_semantics=("arbitrary",), vmem_limit_bytes=VMEM_LIMIT),
        name="adaln_mod",
    )(cond8, w_mod, b_mod)


def _proj_kernel(use_rope, *refs):
    (x_ref, mod_ref, g1_ref, win_ref, wqn_ref, wqr_ref,
     gqn_ref, gkn_ref, gcq_ref, gckv_ref, gnope_ref, gqr_ref, gkr_ref) = refs[:13]
    if use_rope:
        ca_ref, sa_ref, cb_ref, sb_ref = refs[13:17]
        outs = refs[17:]
    else:
        outs = refs[13:]
    qa_ref, ka_ref, va_ref, qb_ref, ckv_ref, kr_ref, ga_ref, gb_ref = outs

    def rope(t, c_ref, s_ref):
        return t * c_ref[...] + pltpu.roll(t, LANE // 2, 1) * s_ref[...]

    rope_a = (lambda t: rope(t, ca_ref, sa_ref)) if use_rope else (lambda t: t)
    rope_b = (lambda t: rope(t, cb_ref, sb_ref)) if use_rope else (lambda t: t)

    x = x_ref[0]
    m = mod_ref[0]
    sh1, sc1 = m[0:1], m[1:2]
    h = (_rms(x, D) * g1_ref[...]) * (1.0 + sc1) + sh1
    hb = h.astype(BF16)

    def proj(lo, hi):
        return _dot(hb, win_ref[:, lo:hi])

    zq = proj(C_QA, C_KA)
    for hh in range(H_A):
        t = _rms(zq[:, hh * HD_A:(hh + 1) * HD_A], HD_A) * gqn_ref[...]
        qa_ref[0, hh] = (rope_a(t) * QSCALE_A).astype(BF16)
    zk = proj(C_KA, C_VA)
    zv = proj(C_VA, C_CQ)
    for j in range(KV_A):
        t = _rms(zk[:, j * HD_A:(j + 1) * HD_A], HD_A) * gkn_ref[...]
        ka_ref[0, j] = rope_a(t)
        va_ref[0, j] = zv[:, j * HD_A:(j + 1) * HD_A]

    cq = (_rms(proj(C_CQ, C_CKV), Q_LORA) * gcq_ref[...]).astype(BF16)
    qn = _dot(cq, wqn_ref[...])
    qr = _dot(cq, wqr_ref[...])
    for hh in range(H_B):
        t = _rms(qn[:, hh * LANE:(hh + 1) * LANE], NOPE_B) * gnope_ref[...]
        qb_ref[0, hh, :, 0:LANE] = (t * QSCALE_B).astype(BF16)
        t = _rms(qr[:, hh * LANE:(hh + 1) * LANE], ROPE_B) * gqr_ref[...]
        qb_ref[0, hh, :, LANE:2 * LANE] = (rope_b(t) * QSCALE_B).astype(BF16)

    ckv_ref[0] = _rms(proj(C_CKV, C_KR), KV_LORA) * gckv_ref[...]
    kr_ref[0] = rope_b(_rms(proj(C_KR, C_GA), ROPE_B) * gkr_ref[...])
    ga_ref[0] = jax.nn.sigmoid(proj(C_GA, C_GB))
    gb_ref[0] = jax.nn.sigmoid(proj(C_GB, C_END))


def _project(x, modsel, per_batch, w, rope_tabs):
    B, S, _ = x.shape
    use_rope = rope_tabs is not None
    mod_idx = (lambda b, i: (b, 0, 0)) if per_batch else (lambda b, i: (0, 0, 0))
    in_specs = [pl.BlockSpec((1, TM, D), lambda b, i: (b, i, 0)),
                pl.BlockSpec((1, N_MOD, D), mod_idx),
                _const_spec((1, D)), _const_spec((D, C_END)), _const_spec((Q_LORA, H_B * LANE)),
                _const_spec((Q_LORA, H_B * LANE)),
                _const_spec((1, LANE)), _const_spec((1, LANE)), _const_spec((1, Q_LORA)), _const_spec((1, LANE)),
                _const_spec((1, LANE)), _const_spec((1, LANE)), _const_spec((1, LANE))]
    args = [x, modsel, w["norm1_g"], w["w_in"], w["w_qn"], w["w_qr"], w["gqa_qn"], w["gqa_kn"], w["mla_qa_g"],
            w["mla_kva_g"], w["mla_qn_nope"], w["mla_qn_rope"], w["mla_kn_rope"]]
    if use_rope:
        in_specs += [pl.BlockSpec((TM, LANE), lambda b, i: (i, 0))] * 4
        args += list(rope_tabs)
    tok4 = lambda h, d: pl.BlockSpec((1, h, TM, d), lambda b, i: (b, 0, i, 0))
    tok3 = lambda d: pl.BlockSpec((1, TM, d), lambda b, i: (b, i, 0))
    out_specs = [tok4(H_A, HD_A), tok4(KV_A, HD_A), tok4(KV_A, HD_A), tok4(H_B, 2 * LANE),
                 tok3(KV_LORA), tok3(LANE), tok3(D), tok3(D)]
    out_shape = [jax.ShapeDtypeStruct((B, H_A, S, HD_A), BF16),
                 jax.ShapeDtypeStruct((B, KV_A, S, HD_A), F32),
                 jax.ShapeDtypeStruct((B, KV_A, S, HD_A), F32),
                 jax.ShapeDtypeStruct((B, H_B, S, 2 * LANE), BF16),
                 jax.ShapeDtypeStruct((B, S, KV_LORA), F32),
                 jax.ShapeDtypeStruct((B, S, LANE), F32),
                 jax.ShapeDtypeStruct((B, S, D), F32),
                 jax.ShapeDtypeStruct((B, S, D), F32)]
    return pl.pallas_call(
        functools.partial(_proj_kernel, use_rope),
        grid=(B, S // TM),
        in_specs=in_specs, out_specs=out_specs, out_shape=out_shape,
        compiler_params=pltpu.CompilerParams(dimension_semantics=("arbitrary", "arbitrary"),
                                             vmem_limit_bytes=VMEM_LIMIT),
        name="proj_rope" if use_rope else "proj",
    )(*args)


def _kvb_kernel(ckv_ref, kr_ref, wk_ref, wv_ref, gk_ref, kb_ref, vb_ref):
    c = ckv_ref[0].astype(BF16)
    k = _dot(c, wk_ref[...])
    v = _dot(c, wv_ref[...])
    krb = kr_ref[0].astype(BF16)
    for hh in range(H_B):
        t = _rms(k[:, hh * LANE:(hh + 1) * LANE], NOPE_B) * gk_ref[...]
        kb_ref[0, hh, :, 0:LANE] = t.astype(BF16)
        kb_ref[0, hh, :, LANE:2 * LANE] = krb
        vb_ref[0, hh, :, 0:VD_B] = v[:, hh * LANE:(hh + 1) * LANE].astype(BF16)
        vb_ref[0, hh, :, VD_B:2 * VD_B] = jnp.ones((v.shape[0], VD_B), BF16)


def _kv_up(ckv_all, krp_all, w):
    B, L, _ = ckv_all.shape
    return pl.pallas_call(
        _kvb_kernel,
        grid=(B, L // TL),
        in_specs=[pl.BlockSpec((1, TL, KV_LORA), lambda b, i: (b, i, 0)),
                  pl.BlockSpec((1, TL, LANE), lambda b, i: (b, i, 0)),
                  _const_spec((KV_LORA, H_B * LANE)), _const_spec((KV_LORA, H_B * LANE)), _const_spec((1, LANE))],
        out_specs=[pl.BlockSpec((1, H_B, TL, 2 * LANE), lambda b, i: (b, 0, i, 0)),
                   pl.BlockSpec((1, H_B, TL, 2 * VD_B), lambda b, i: (b, 0, i, 0))],
        out_shape=[jax.ShapeDtypeStruct((B, H_B, L, 2 * LANE), BF16),
                   jax.ShapeDtypeStruct((B, H_B, L, 2 * VD_B), BF16)],
        compiler_params=pltpu.CompilerParams(dimension_semantics=("arbitrary", "arbitrary"),
                                             vmem_limit_bytes=VMEM_LIMIT),
        name="kv_up",
    )(ckv_all, krp_all, w["w_kvk"], w["w_kvv"], w["mla_kn_nope"])


def _attn_kernel(n_heads, group, q_ref, k_ref, v_ref, o_ref):
    dv = v_ref.shape[-1] // 2
    for hh in range(n_heads):
        q = q_ref[0, hh]
        k = k_ref[0, hh // group]
        v = v_ref[0, hh // group]
        s = lax.dot_general(q, k, (((1,), (1,)), ((), ())), preferred_element_type=F32)
        p = jnp.exp2(s - jnp.max(s, axis=-1, keepdims=True))
        o = _dot(p.astype(BF16), v)
        o_ref[0, :, hh * dv:(hh + 1) * dv] = (o[:, :dv] / o[:, dv:]).astype(BF16)


def _attention(q, k, v, name):
    B, H, S, dk = q.shape
    _, Hk, L, dv2 = v.shape
    dv = dv2 // 2
    return pl.pallas_call(
        functools.partial(_attn_kernel, H, H // Hk),
        grid=(B, S // TQ),
        in_specs=[pl.BlockSpec((1, H, TQ, dk), lambda b, i: (b, 0, i, 0)),
                  pl.BlockSpec((1, Hk, L, dk), lambda b, i: (b, 0, 0, 0)),
                  pl.BlockSpec((1, Hk, L, dv2), lambda b, i: (b, 0, 0, 0))],
        out_specs=pl.BlockSpec((1, TQ, H * dv), lambda b, i: (b, i, 0)),
        out_shape=jax.ShapeDtypeStruct((B, S, H * dv), BF16),
        compiler_params=pltpu.CompilerParams(dimension_semantics=("arbitrary", "arbitrary"),
                                             vmem_limit_bytes=VMEM_LIMIT),
        name=name,
    )(q, k, v)


def _route(bi, sc, carry):
    n_tok = bi.shape[1]
    iota8 = lax.broadcasted_iota(I32, (GROUP_SIZE, n_tok), 0)
    neg = -jnp.inf
    grp = [bi[j * GROUP_SIZE:(j + 1) * GROUP_SIZE] for j in range(N_GROUPS)]
    gscore = []
    for g in grp:
        m1 = jnp.max(g, axis=0, keepdims=True)
        first = jnp.min(jnp.where(g == m1, iota8, GROUP_SIZE), axis=0, keepdims=True)
        m2 = jnp.max(jnp.where(iota8 == first, neg, g), axis=0, keepdims=True)
        gscore.append(m1 + m2)
    masked = []
    for j in range(N_GROUPS):
        cnt = jnp.zeros((1, n_tok), I32)
        for j2 in range(N_GROUPS):
            if j2 == j:
                continue
            beats = (gscore[j2] > gscore[j]) if j2 > j else (gscore[j2] >= gscore[j])
            cnt = cnt + beats.astype(I32)
        keep = jnp.broadcast_to(cnt, grp[j].shape) < TOPK_GROUPS
        masked.append(jnp.where(keep, grp[j], neg))
    cnts = [jnp.zeros((GROUP_SIZE, n_tok), I32) for _ in range(N_GROUPS)]
    for e2 in range(N_EXPERTS):
        j2, r2 = divmod(e2, GROUP_SIZE)
        row = masked[j2][r2:r2 + 1]
        for j in range(N_GROUPS):
            if j > j2:
                beats = (row >= masked[j]).astype(I32)
            elif j < j2:
                beats = (row > masked[j]).astype(I32)
            else:
                beats = jnp.where(iota8 > r2, (row >= masked[j]).astype(I32), (row > masked[j]).astype(I32))
            cnts[j] = cnts[j] + beats
    picked = [jnp.where(cnts[j] < TOP_K, sc[j * GROUP_SIZE:(j + 1) * GROUP_SIZE], 0.0) for j in range(N_GROUPS)]
    total = picked[0].sum(axis=0, keepdims=True)
    for j in range(1, N_GROUPS):
        total = total + picked[j].sum(axis=0, keepdims=True)
    wts = [picked[j] / total * ROUTED_SCALE for j in range(N_GROUPS)]
    kept = jnp.concatenate([(c < TOP_K).astype(F32) for c in cnts], axis=0)
    earlier = (lax.broadcasted_iota(I32, (n_tok, n_tok), 0) < lax.broadcasted_iota(I32, (n_tok, n_tok), 1))
    rank = _dot(kept.astype(BF16), earlier.astype(F32).astype(BF16)) + carry
    new_carry = carry + jnp.sum(kept, axis=1, keepdims=True)
    idx_rows = jnp.zeros((TOP_K, n_tok), F32)
    w_rows = jnp.zeros((TOP_K, n_tok), F32)
    rank_rows = jnp.zeros((TOP_K, n_tok), F32)
    for k in range(TOP_K):
        ia = jnp.zeros((GROUP_SIZE, n_tok), F32)
        wa = jnp.zeros((GROUP_SIZE, n_tok), F32)
        ra = jnp.zeros((GROUP_SIZE, n_tok), F32)
        for j in range(N_GROUPS):
            hit = cnts[j] == k
            ia = ia + jnp.where(hit, (iota8 + j * GROUP_SIZE).astype(F32), 0.0)
            wa = wa + jnp.where(hit, wts[j], 0.0)
            ra = ra + jnp.where(hit, rank[j * GROUP_SIZE:(j + 1) * GROUP_SIZE], 0.0)
        idx_rows = jnp.where(iota8 == k, jnp.sum(ia, axis=0, keepdims=True), idx_rows)
        w_rows = jnp.where(iota8 == k, jnp.sum(wa, axis=0, keepdims=True), w_rows)
        rank_rows = jnp.where(iota8 == k, jnp.sum(ra, axis=0, keepdims=True), rank_rows)
    return idx_rows, w_rows, rank_rows, new_carry


def _post_kernel(tiles_per_group, oa_ref, ob_ref, ga_ref, gb_ref, x_ref, mod_ref, g2n_ref, woa_ref, wob_ref,
                 wout_ref, wr_ref, rb_ref, ws1_ref, ws3_ref, ws2_ref, base_ref, hp_ref, idx_ref, wk_ref, rank_ref,
                 carry_ref):
    tile = pl.program_id(0) * pl.num_programs(1) + pl.program_id(1)

    @pl.when(tile % tiles_per_group == 0)
    def _():
        carry_ref[...] = jnp.zeros_like(carry_ref)

    m = mod_ref[0]
    g1, sh2, sc2, g2 = m[2:3], m[3:4], m[4:5], m[5:6]
    merged = ga_ref[0] * _dot(oa_ref[0], woa_ref[...]) + gb_ref[0] * _dot(ob_ref[0], wob_ref[...])
    x1 = x_ref[0] + g1 * _dot(merged.astype(BF16), wout_ref[...])
    h2 = (_rms(x1, D) * g2n_ref[...]) * (1.0 + sc2) + sh2
    h2b = h2.astype(BF16)
    act = _silu(_dot(h2b, ws1_ref[...])) * _dot(h2b, ws3_ref[...])
    base_ref[0] = x1 + g2 * _dot(act.astype(BF16), ws2_ref[...])
    n_tok = h2.shape[0]
    for j in range(HP_ROWS):
        hp_ref[pl.ds(j, n_tok, stride=HP_ROWS), :] = h2[:, j * LANE:(j + 1) * LANE]
    logits = jnp.dot(h2, wr_ref[...], preferred_element_type=F32, precision=HIGHEST)
    sc = jax.nn.sigmoid(logits.T[0:N_EXPERTS])
    idx_rows, w_rows, rank_rows, carry = _route(sc + rb_ref[...], sc, carry_ref[...])
    carry_ref[...] = carry
    idx_ref[...] = idx_rows.astype(I32)
    wk_ref[...] = w_rows
    rank_ref[...] = rank_rows.astype(I32)


def _post(oa, ob, ga, gb, x, modsel, per_batch, w, gt):
    B, S, _ = x.shape
    nt = S // TM
    mod_idx = (lambda b, i: (b, 0, 0)) if per_batch else (lambda b, i: (0, 0, 0))
    tok = lambda d: pl.BlockSpec((1, TM, d), lambda b, i: (b, i, 0))
    slot = lambda: pl.BlockSpec((TOP_K, TM), lambda b, i: (0, b * nt + i))
    return pl.pallas_call(
        functools.partial(_post_kernel, gt // TM),
        grid=(B, nt),
        in_specs=[tok(D), tok(D), tok(D), tok(D), tok(D), pl.BlockSpec((1, N_MOD, D), mod_idx),
                  _const_spec((1, D)), _const_spec((D, D)), _const_spec((D, D)), _const_spec((D, D)),
                  _const_spec((D, LANE)), _const_spec((N_EXPERTS, 1)),
                  _const_spec((D, D_EXPERT)), _const_spec((D, D_EXPERT)), _const_spec((D_EXPERT, D))],
        out_specs=[tok(D),
                   pl.BlockSpec((TM * HP_ROWS, LANE), lambda b, i: (b * nt + i, 0)),
                   slot(), slot(), slot()],
        out_shape=[jax.ShapeDtypeStruct((B, S, D), F32),
                   jax.ShapeDtypeStruct((B * S * HP_ROWS, LANE), F32),
                   jax.ShapeDtypeStruct((TOP_K, B * S), I32),
                   jax.ShapeDtypeStruct((TOP_K, B * S), F32),
                   jax.ShapeDtypeStruct((TOP_K, B * S), I32)],
        scratch_shapes=[pltpu.VMEM((N_EXPERTS, 1), F32)],
        compiler_params=pltpu.CompilerParams(dimension_semantics=("arbitrary", "arbitrary"),
                                             vmem_limit_bytes=VMEM_LIMIT),
        name="post",
    )(oa, ob, ga, gb, x, modsel, w["norm2_g"], w["w_oa"], w["w_ob"], w["w_out"], w["w_router"],
      w["router_bias"], w["w_s1"], w["w_s3"], w["w_s2"])


def _dispatch_rows(idx, wk, rank, n_groups):
    T = idx.shape[1]
    gt = T // n_groups
    experts = jnp.arange(N_EXPERTS, dtype=I32)
    regroup = lambda a: a.reshape(TOP_K, n_groups, gt).transpose(1, 0, 2)
    eid = regroup(idx)
    onehot = (eid[:, None] == experts[None, :, None, None]).astype(I32)
    counts = jnp.sum(onehot, axis=(2, 3))
    padded = (counts + (MB - 1)) // MB * MB
    bounds = jnp.concatenate([jnp.zeros((n_groups, 1), I32), jnp.cumsum(padded, axis=1)], axis=1)
    row = jnp.sum(onehot * bounds[:, :N_EXPERTS, None, None], axis=1) + regroup(rank)
    flat = lambda a: a.reshape(n_groups, TOP_K * gt)
    return bounds, counts, flat(row), flat(regroup(wk))


def _moe_kernel(gt, seg_ref, cnt_ref, row_ref, wk_ref, hp_ref, w1_ref, w3_ref, w2_ref, o_ref,
                tin0_ref, tin1_ref, tout0_ref, tout1_ref, w1b_ref, w3b_ref, w2b_ref, tok_ref, wts_ref):
    tin = (tin0_ref, tin1_ref)
    tout = (tout0_ref, tout1_ref)

    def build_row_lists():
        def spare_tail(ex, carry):
            def one(p, c):
                tok_ref[p] = gt
                wts_ref[p] = 0.0
                return c
            return lax.fori_loop(seg_ref[ex] + cnt_ref[ex], seg_ref[ex + 1], one, carry)

        lax.fori_loop(0, N_EXPERTS, spare_tail, 0)

        def chunk(i, carry):
            t0 = i * LIST_CHUNK
            for u in range(LIST_CHUNK):
                for k in range(TOP_K):
                    p = row_ref[k * gt + t0 + u]
                    tok_ref[p] = t0 + u
                    wts_ref[p] = wk_ref[k * gt + t0 + u]
            return carry

        lax.fori_loop(0, gt // LIST_CHUNK, chunk, 0)

    e = pl.program_id(0)
    n_total = lax.shift_right_logical(seg_ref[N_EXPERTS], MB_LOG2)

    def gather(b, slot):
        r0 = b * MB
        for m in range(MB):
            t = tok_ref[r0 + m] & (gt - 1)
            slab = hp_ref[pl.ds(pl.multiple_of(t * HP_ROWS, HP_ROWS), HP_ROWS), :]
            tin[slot][pl.ds(m, HP_ROWS, stride=S_IN), :] = slab

    def scatter(b, slot, live):
        r0 = b * MB
        for b0 in range(0, MB, RMW_BATCH):
            pending = []
            for m in range(b0, b0 + RMW_BATCH):
                off = pl.multiple_of(tok_ref[r0 + m] * OUT_ROWS, OUT_ROWS)
                upd = tout[slot][pl.ds(m, OUT_ROWS, stride=S_OUT), :] * wts_ref[r0 + m]
                if live is not None:
                    upd = jnp.where(live, upd, 0.0)
                pending.append((off, o_ref[pl.ds(off, OUT_ROWS), :] + upd))
            for off, v in pending:
                o_ref[pl.ds(off, OUT_ROWS), :] = v

    @pl.when(e == 0)
    def _():
        build_row_lists()
        o_ref[...] = jnp.zeros_like(o_ref)
        tout1_ref[...] = jnp.zeros_like(tout1_ref)
        gather(0, 0)

    w1b_ref[...] = w1_ref[0].astype(BF16)
    w3b_ref[...] = w3_ref[0].astype(BF16)
    w2b_ref[...] = w2_ref[0].astype(BF16)
    first = lax.shift_right_logical(seg_ref[e], MB_LOG2)
    n_blocks = lax.shift_right_logical(seg_ref[e + 1], MB_LOG2) - first

    def stages(b, slot):
        gather(jnp.minimum(b + 1, n_total - 1), 1 - slot)
        x = jnp.concatenate([tin[slot][pl.ds(j * S_IN, MB), :] for j in range(HP_ROWS)], axis=-1).astype(BF16)
        act = _silu(_dot(x, w1b_ref[...])) * _dot(x, w3b_ref[...])
        y = _dot(act.astype(BF16), w2b_ref[...])
        for j in range(OUT_ROWS):
            tout[slot][pl.ds(j * S_OUT, MB), :] = y[:, j * LANE:(j + 1) * LANE]
        scatter(jnp.maximum(b - 1, 0), 1 - slot, b > 0)

    def block(i, carry):
        b = first + i
        for slot in range(2):
            @pl.when(b & 1 == slot)
            def _(slot=slot):
                stages(b, slot)
        return carry

    lax.fori_loop(0, n_blocks, block, 0)

    @pl.when(e == N_EXPERTS - 1)
    def _():
        for slot in range(2):
            @pl.when((n_total - 1) & 1 == slot)
            def _(slot=slot):
                scatter(n_total - 1, slot, None)


def _moe(hp, group, gt, seg, counts, rows, wk, w):
    n_rows = TOP_K * gt + N_EXPERTS * MB
    grid_spec = pltpu.PrefetchScalarGridSpec(
        num_scalar_prefetch=4,
        grid=(N_EXPERTS,),
        in_specs=[pl.BlockSpec((gt * HP_ROWS, LANE), lambda e, *_: (group, 0), pipeline_mode=pl.Buffered(1)),
                  pl.BlockSpec((1, D, D_EXPERT), lambda e, *_: (e, 0, 0)),
                  pl.BlockSpec((1, D, D_EXPERT), lambda e, *_: (e, 0, 0)),
                  pl.BlockSpec((1, D_EXPERT, D), lambda e, *_: (e, 0, 0))],
        out_specs=pl.BlockSpec(((gt + 1) * OUT_ROWS, LANE), lambda e, *_: (0, 0), pipeline_mode=pl.Buffered(1)),
        scratch_shapes=[pltpu.VMEM((HP_ROWS * S_IN, LANE), F32), pltpu.VMEM((HP_ROWS * S_IN, LANE), F32),
                        pltpu.VMEM((OUT_ROWS * S_OUT, LANE), F32), pltpu.VMEM((OUT_ROWS * S_OUT, LANE), F32),
                        pltpu.VMEM((D, D_EXPERT), BF16), pltpu.VMEM((D, D_EXPERT), BF16),
                        pltpu.VMEM((D_EXPERT, D), BF16),
                        pltpu.SMEM((n_rows,), I32), pltpu.SMEM((n_rows,), F32)])
    return pl.pallas_call(
        functools.partial(_moe_kernel, gt),
        grid_spec=grid_spec,
        out_shape=jax.ShapeDtypeStruct(((gt + 1) * OUT_ROWS, LANE), F32),
        compiler_params=pltpu.CompilerParams(dimension_semantics=("arbitrary",), vmem_limit_bytes=VMEM_LIMIT),
        name="moe_experts",
    )(seg, counts, rows, wk, hp, w["w_e1"], w["w_e3"], w["w_e2"])


def _combine_kernel(n_groups, tiles_per_group, *refs):
    r_refs = refs[:n_groups]
    base_ref, mod_ref, o_ref = refs[n_groups:]
    tile = pl.program_id(0) * pl.num_programs(1) + pl.program_id(1)
    g2 = mod_ref[0][5:6]
    n_tok = base_ref.shape[1]
    for g, r_ref in enumerate(r_refs):
        @pl.when(tile // tiles_per_group == g)
        def _(r_ref=r_ref):
            for s in range(OUT_ROWS):
                cols = slice(s * LANE, (s + 1) * LANE)
                o_ref[0, :, cols] = base_ref[0, :, cols] + g2[:, cols] * r_ref[pl.ds(s, n_tok, stride=OUT_ROWS), :]


def _combine(routed, base, modsel, per_batch, gt):
    B, S, _ = base.shape
    nt = S // TM
    tiles_per_group = gt // TM
    mod_idx = (lambda b, i: (b, 0, 0)) if per_batch else (lambda b, i: (0, 0, 0))

    def r_spec(g):
        def idx(b, i):
            tile = b * nt + i
            return (jnp.where(tile // tiles_per_group == g, tile % tiles_per_group, 0), 0)
        return pl.BlockSpec((TM * OUT_ROWS, LANE), idx)

    return pl.pallas_call(
        functools.partial(_combine_kernel, len(routed), tiles_per_group),
        grid=(B, nt),
        in_specs=[r_spec(g) for g in range(len(routed))] + [
            pl.BlockSpec((1, TM, D), lambda b, i: (b, i, 0)),
            pl.BlockSpec((1, N_MOD, D), mod_idx)],
        out_specs=pl.BlockSpec((1, TM, D), lambda b, i: (b, i, 0)),
        out_shape=jax.ShapeDtypeStruct((B, S, D), F32),
        compiler_params=pltpu.CompilerParams(dimension_semantics=("arbitrary", "arbitrary"),
                                             vmem_limit_bytes=VMEM_LIMIT),
        name="moe_combine",
    )(*routed, base, modsel)


def _trunk_pass(x, modsel, per_batch, w, rope_tabs, cache):
    B, S, _ = x.shape
    qa, ka, va, qb, ckv, krp, ga, gb = _project(x, modsel, per_batch, w, rope_tabs)
    if cache is None:
        ka_all, va_all, ckv_all, krp_all = ka, va, ckv, krp
    else:
        ck, cv, cckv, ckr = cache
        ka_all = jnp.concatenate([ck, ka], axis=2)
        va_all = jnp.concatenate([cv, va], axis=2)
        ckv_all = jnp.concatenate([cckv, ckv], axis=1)
        krp_all = jnp.concatenate([_pad64(ckr), krp], axis=1)
    kb, vb = _kv_up(ckv_all, krp_all, w)
    va_ones = jnp.concatenate([va_all.astype(BF16), jnp.ones(va_all.shape, BF16)], axis=-1)
    oa = _attention(qa, ka_all.astype(BF16), va_ones, "attn_gqa")
    ob = _attention(qb, kb, vb, "attn_mla")
    T = B * S
    gt = min(GROUP_TOKENS, T)
    assert T % gt == 0 and gt % TM == 0 and gt & (gt - 1) == 0 and gt >= MB and gt % LIST_CHUNK == 0
    base, hp, idx, wk, rank = _post(oa, ob, ga, gb, x, modsel, per_batch, w, gt)
    seg, counts, rows, wks = _dispatch_rows(idx, wk, rank, T // gt)
    routed = [_moe(hp, g, gt, seg[g], counts[g], rows[g], wks[g], w) for g in range(T // gt)]
    y = _combine(routed, base, modsel, per_batch, gt)
    return y, (ka, va, ckv, _unpad64(krp))


def _rope_tables(n_tokens):
    t = jnp.arange(n_tokens)
    row = (t // GRID_W).astype(F32)[:, None]
    col = (t % GRID_W).astype(F32)[:, None]

    def tabs(rot_dim):
        axis_dim = rot_dim // 2
        inv = ROPE_THETA ** (-jnp.arange(0, axis_dim, 2, dtype=F32) / axis_dim)
        ang = jnp.concatenate([row * inv, col * inv], axis=-1)
        return jnp.cos(ang), jnp.sin(ang)

    cos_a, sin_a = tabs(HD_A)
    cos_b, sin_b = tabs(ROPE_B)
    return (jnp.concatenate([cos_a, cos_a], -1), jnp.concatenate([-sin_a, sin_a], -1),
            _pad64(jnp.concatenate([cos_b, cos_b], -1)), _pad64(jnp.concatenate([-sin_b, sin_b], -1)))


def _layer_weights(l, w_in, w_qb, w_kvb, w_router, router_bias, named):
    w = {k: v[l] for k, v in named.items()}
    wi = w_in[l]
    w["w_in"] = jnp.concatenate([wi[:, :C_KR], _pad64(wi[:, C_KR:C_KR + ROPE_B]), wi[:, C_KR + ROPE_B:]],
                                axis=1).astype(BF16)
    wq = w_qb[l].reshape(Q_LORA, H_B, NOPE_B + ROPE_B)
    w["w_qn"] = wq[:, :, :NOPE_B].reshape(Q_LORA, H_B * LANE).astype(BF16)
    w["w_qr"] = _pad64(wq[:, :, NOPE_B:]).reshape(Q_LORA, H_B * LANE).astype(BF16)
    wkv = w_kvb[l].reshape(KV_LORA, H_B, NOPE_B + VD_B)
    w["w_kvk"] = wkv[:, :, :NOPE_B].reshape(KV_LORA, H_B * LANE).astype(BF16)
    w["w_kvv"] = wkv[:, :, NOPE_B:].reshape(KV_LORA, H_B * LANE).astype(BF16)
    w["w_router"] = jnp.pad(w_router[l], ((0, 0), (0, LANE - N_EXPERTS)))
    w["router_bias"] = router_bias[l].reshape(N_EXPERTS, 1)
    for k in ("norm1_g", "norm2_g", "gqa_qn", "gqa_kn", "mla_qa_g", "mla_kva_g", "mla_qn_nope", "mla_kn_nope"):
        w[k] = w[k].reshape(1, -1)
    for k in ("mla_qn_rope", "mla_kn_rope"):
        w[k] = _pad64(w[k]).reshape(1, LANE)
    for k in ("w_oa", "w_ob", "w_out", "w_s1", "w_s3", "w_s2"):
        w[k] = w[k].astype(BF16)
    return w


def kernel(x_prompt, x_sample, c, cache_gqa_k, cache_gqa_v, cache_mla_ckv, cache_mla_krope, c_ctx, w_mod, b_mod, norm1_g, norm2_g, w_in, gqa_qn, gqa_kn, mla_qa_g, mla_kva_g, w_qb, w_kvb, mla_qn_nope, mla_qn_rope, mla_kn_nope, mla_kn_rope, w_oa, w_ob, w_out, w_router, router_bias, w_e1, w_e3, w_e2, w_s1, w_s3, w_s2):
    depth = w_mod.shape[0]
    n_dec = x_sample.shape[0]
    rope_tabs = _rope_tables(x_sample.shape[1])
    cond8 = jnp.concatenate([c_ctx[None, :], c, jnp.zeros((8 - 1 - n_dec, D), F32)], axis=0)
    named = dict(norm1_g=norm1_g, norm2_g=norm2_g, gqa_qn=gqa_qn, gqa_kn=gqa_kn, mla_qa_g=mla_qa_g,
                 mla_kva_g=mla_kva_g, mla_qn_nope=mla_qn_nope, mla_qn_rope=mla_qn_rope, mla_kn_nope=mla_kn_nope,
                 mla_kn_rope=mla_kn_rope, w_oa=w_oa, w_ob=w_ob, w_out=w_out, w_e1=w_e1, w_e3=w_e3, w_e2=w_e2,
                 w_s1=w_s1, w_s3=w_s3, w_s2=w_s2)
    xp, xs = x_prompt, x_sample
    ks, vs, ckvs, krs = [], [], [], []
    for l in range(depth):
        w = _layer_weights(l, w_in, w_qb, w_kvb, w_router, router_bias, named)
        mod = _modulation(cond8, w_mod[l], b_mod[l].reshape(1, -1))
        mod_ctx = mod[0:1].reshape(1, N_MOD, D)
        mod_dec = mod[1:1 + n_dec].reshape(n_dec, N_MOD, D)
        xp, ctx = _trunk_pass(xp, mod_ctx, False, w, None, None)
        ks.append(ctx[0])
        vs.append(ctx[1])
        ckvs.append(ctx[2])
        krs.append(ctx[3])
        cache = (cache_gqa_k[:, l], cache_gqa_v[:, l], cache_mla_ckv[:, l], cache_mla_krope[:, l])
        xs, _ = _trunk_pass(xs, mod_dec, True, w, rope_tabs, cache)
    return (xp, xs, jnp.stack(ks, axis=1), jnp.stack(vs, axis=1), jnp.stack(ckvs, axis=1), jnp.stack(krs, axis=1))
```

```python
import functools
import math

import jax
import jax.numpy as jnp
from jax import lax
from jax.experimental import pallas as pl
from jax.experimental.pallas import tpu as pltpu

F32 = jnp.float32
BF16 = jnp.bfloat16
I32 = jnp.int32
HIGHEST = lax.Precision.HIGHEST

D = 1024
GRID_W = 64
RMS_EPS = 1e-6
ROPE_THETA = 10000.0
N_MOD = 6
H_A, KV_A, HD_A = 8, 2, 128
H_B, Q_LORA, KV_LORA, NOPE_B, ROPE_B, VD_B = 8, 256, 128, 128, 64, 128
N_EXPERTS, TOP_K, N_GROUPS, TOPK_GROUPS = 64, 8, 8, 4
GROUP_SIZE = N_EXPERTS // N_GROUPS
D_EXPERT = 256
ROUTED_SCALE = 2.5
QSCALE_A = HD_A ** -0.5 * math.log2(math.e)
QSCALE_B = (NOPE_B + ROPE_B) ** -0.5 * math.log2(math.e)

LANE = 128
C_QA, C_KA, C_VA, C_CQ, C_CKV, C_KR, C_GA, C_GB, C_END = 0, 1024, 1280, 1536, 1792, 1920, 2048, 3072, 4096
VMEM_LIMIT = 52 * 1024 * 1024

TM = 256
TQ = 256
TL = 256
GROUP_TOKENS = 4096
MB, MB_LOG2 = 128, 7
HP_ROWS = D // LANE
OUT_ROWS = D // LANE
S_IN = MB + 1
S_OUT = MB + 1
RMW_BATCH = 8
LIST_CHUNK = 16


def _dot(a, b):
    return jnp.dot(a, b, preferred_element_type=F32)


def _rms(t, n_valid):
    ms = jnp.sum(t * t, axis=-1, keepdims=True) * (1.0 / n_valid)
    return t * lax.rsqrt(ms + RMS_EPS)


def _silu(t):
    return t * jax.nn.sigmoid(t)


def _pad64(a):
    z = jnp.zeros(a.shape[:-1] + (32,), a.dtype)
    return jnp.concatenate([a[..., :32], z, a[..., 32:], z], axis=-1)


def _unpad64(a):
    return jnp.concatenate([a[..., :32], a[..., 64:96]], axis=-1)


def _const_spec(shape):
    nd = len(shape)
    return pl.BlockSpec(shape, lambda *_: (0,) * nd)


def _mod_kernel(c_ref, w_ref, b_ref, o_ref):
    s = _silu(c_ref[...])
    o_ref[...] = jnp.dot(s, w_ref[...], preferred_element_type=F32, precision=HIGHEST) + b_ref[...]


def _modulation(cond8, w_mod, b_mod):
    tn = 1024
    n = w_mod.shape[1]
    return pl.pallas_call(
        _mod_kernel,
        grid=(n // tn,),
        in_specs=[pl.BlockSpec((8, D), lambda j: (0, 0)),
                  pl.BlockSpec((D, tn), lambda j: (0, j)),
                  pl.BlockSpec((1, tn), lambda j: (0, j))],
        out_specs=pl.BlockSpec((8, tn), lambda j: (0, j)),
        out_shape=jax.ShapeDtypeStruct((8, n), F32),
        compiler_params=pltpu.CompilerParams(dimension_semantics=("arbitrary",), vmem_limit_bytes=VMEM_LIMIT),
        name="adaln_mod",
    )(cond8, w_mod, b_mod)


def _proj_kernel(use_rope, *refs):
    (x_ref, mod_ref, g1_ref, win_ref, wqn_ref, wqr_ref,
     gqn_ref, gkn_ref, gcq_ref, gckv_ref, gnope_ref, gqr_ref, gkr_ref) = refs[:13]
    n_in = 17 if use_rope else 13
    if use_rope:
        ca_ref, sa_ref, cb_ref, sb_ref = refs[13:17]
    qa_ref, ka_ref, va_ref, qb_ref, ckv_ref, kr_ref, ga_ref, gb_ref = refs[n_in:n_in + 8]
    zbuf = refs[n_in + 8:n_in + 10]
    qbuf = refs[n_in + 10:n_in + 12]
    step = pl.program_id(0)

    def rope(t, c_ref, s_ref):
        return t * c_ref[...] + pltpu.roll(t, LANE // 2, 1) * s_ref[...]

    rope_a = (lambda t: rope(t, ca_ref, sa_ref)) if use_rope else (lambda t: t)
    rope_b = (lambda t: rope(t, cb_ref, sb_ref)) if use_rope else (lambda t: t)

    @pl.when(step == 0)
    def _():
        zbuf[1][...] = jnp.zeros_like(zbuf[1])
        qbuf[1][...] = jnp.zeros_like(qbuf[1])

    def matmuls(z_ref, q_ref):
        x = x_ref[0]
        m = mod_ref[0]
        sh1, sc1 = m[0:1], m[1:2]
        hb = ((_rms(x, D) * g1_ref[...]) * (1.0 + sc1) + sh1).astype(BF16)
        for lo, hi in ((C_QA, C_KA), (C_KA, C_CQ), (C_CKV, C_GA), (C_GA, C_GB), (C_GB, C_END)):
            z_ref[:, lo:hi] = _dot(hb, win_ref[:, lo:hi])
        cq = (_rms(_dot(hb, win_ref[:, C_CQ:C_CKV]), Q_LORA) * gcq_ref[...]).astype(BF16)
        q_ref[:, 0:H_B * LANE] = _dot(cq, wqn_ref[...])
        q_ref[:, H_B * LANE:2 * H_B * LANE] = _dot(cq, wqr_ref[...])

    def epilogue(z_ref, q_ref):
        for hh in range(H_A):
            t = _rms(z_ref[:, C_QA + hh * HD_A:C_QA + (hh + 1) * HD_A], HD_A) * gqn_ref[...]
            qa_ref[0, hh] = (rope_a(t) * QSCALE_A).astype(BF16)
        for j in range(KV_A):
            t = _rms(z_ref[:, C_KA + j * HD_A:C_KA + (j + 1) * HD_A], HD_A) * gkn_ref[...]
            ka_ref[0, j] = rope_a(t)
            va_ref[0, j] = z_ref[:, C_VA + j * HD_A:C_VA + (j + 1) * HD_A]
        for hh in range(H_B):
            t = _rms(q_ref[:, hh * LANE:(hh + 1) * LANE], NOPE_B) * gnope_ref[...]
            qb_ref[0, hh, :, 0:LANE] = (t * QSCALE_B).astype(BF16)
            t = _rms(q_ref[:, (H_B + hh) * LANE:(H_B + hh + 1) * LANE], ROPE_B) * gqr_ref[...]
            qb_ref[0, hh, :, LANE:2 * LANE] = (rope_b(t) * QSCALE_B).astype(BF16)
        ckv_ref[0] = _rms(z_ref[:, C_CKV:C_KR], KV_LORA) * gckv_ref[...]
        kr_ref[0] = rope_b(_rms(z_ref[:, C_KR:C_GA], ROPE_B) * gkr_ref[...])
        ga_ref[0] = jax.nn.sigmoid(z_ref[:, C_GA:C_GB])
        gb_ref[0] = jax.nn.sigmoid(z_ref[:, C_GB:C_END])

    for parity in range(2):
        @pl.when(step % 2 == parity)
        def _(parity=parity):
            matmuls(zbuf[parity], qbuf[parity])
            epilogue(zbuf[1 - parity], qbuf[1 - parity])


def _project(x, modsel, per_batch, w, rope_tabs):
    B, S, _ = x.shape
    use_rope = rope_tabs is not None
    nt = S // TM
    n_tiles = B * nt
    cur = lambda s: jnp.minimum(s, n_tiles - 1)
    prev = lambda s: jnp.maximum(s - 1, 0)
    mod_idx = (lambda s: (cur(s) // nt, 0, 0)) if per_batch else (lambda s: (0, 0, 0))
    in_specs = [pl.BlockSpec((1, TM, D), lambda s: (cur(s) // nt, cur(s) % nt, 0)),
                pl.BlockSpec((1, N_MOD, D), mod_idx),
                _const_spec((1, D)), _const_spec((D, C_END)), _const_spec((Q_LORA, H_B * LANE)),
                _const_spec((Q_LORA, H_B * LANE)),
                _const_spec((1, LANE)), _const_spec((1, LANE)), _const_spec((1, Q_LORA)), _const_spec((1, LANE)),
                _const_spec((1, LANE)), _const_spec((1, LANE)), _const_spec((1, LANE))]
    args = [x, modsel, w["norm1_g"], w["w_in"], w["w_qn"], w["w_qr"], w["gqa_qn"], w["gqa_kn"], w["mla_qa_g"],
            w["mla_kva_g"], w["mla_qn_nope"], w["mla_qn_rope"], w["mla_kn_rope"]]
    if use_rope:
        in_specs += [pl.BlockSpec((TM, LANE), lambda s: (prev(s) % nt, 0))] * 4
        args += list(rope_tabs)
    tok4 = lambda h, d: pl.BlockSpec((1, h, TM, d), lambda s: (prev(s) // nt, 0, prev(s) % nt, 0))
    tok3 = lambda d: pl.BlockSpec((1, TM, d), lambda s: (prev(s) // nt, prev(s) % nt, 0))
    out_specs = [tok4(H_A, HD_A), tok4(KV_A, HD_A), tok4(KV_A, HD_A), tok4(H_B, 2 * LANE),
                 tok3(KV_LORA), tok3(LANE), tok3(D), tok3(D)]
    out_shape = [jax.ShapeDtypeStruct((B, H_A, S, HD_A), BF16),
                 jax.ShapeDtypeStruct((B, KV_A, S, HD_A), F32),
                 jax.ShapeDtypeStruct((B, KV_A, S, HD_A), F32),
                 jax.ShapeDtypeStruct((B, H_B, S, 2 * LANE), BF16),
                 jax.ShapeDtypeStruct((B, S, KV_LORA), F32),
                 jax.ShapeDtypeStruct((B, S, LANE), F32),
                 jax.ShapeDtypeStruct((B, S, D), F32),
                 jax.ShapeDtypeStruct((B, S, D), F32)]
    return pl.pallas_call(
        functools.partial(_proj_kernel, use_rope),
        grid=(n_tiles + 1,),
        in_specs=in_specs, out_specs=out_specs, out_shape=out_shape,
        scratch_shapes=[pltpu.VMEM((TM, C_END), F32), pltpu.VMEM((TM, C_END), F32),
                        pltpu.VMEM((TM, 2 * H_B * LANE), F32), pltpu.VMEM((TM, 2 * H_B * LANE), F32)],
        compiler_params=pltpu.CompilerParams(dimension_semantics=("arbitrary",), vmem_limit_bytes=VMEM_LIMIT),
        name="proj_rope" if use_rope else "proj",
    )(*args)


def _kvb_kernel(ckv_ref, kr_ref, wk_ref, wv_ref, gk_ref, kb_ref, vb_ref):
    c = ckv_ref[0].astype(BF16)
    k = _dot(c, wk_ref[...])
    v = _dot(c, wv_ref[...])
    krb = kr_ref[0].astype(BF16)
    for hh in range(H_B):
        t = _rms(k[:, hh * LANE:(hh + 1) * LANE], NOPE_B) * gk_ref[...]
        kb_ref[0, hh, :, 0:LANE] = t.astype(BF16)
        kb_ref[0, hh, :, LANE:2 * LANE] = krb
        vb_ref[0, hh, :, 0:VD_B] = v[:, hh * LANE:(hh + 1) * LANE].astype(BF16)
        vb_ref[0, hh, :, VD_B:2 * VD_B] = jnp.ones((v.shape[0], VD_B), BF16)


def _kv_up(ckv_all, krp_all, w):
    B, L, _ = ckv_all.shape
    return pl.pallas_call(
        _kvb_kernel,
        grid=(B, L // TL),
        in_specs=[pl.BlockSpec((1, TL, KV_LORA), lambda b, i: (b, i, 0)),
                  pl.BlockSpec((1, TL, LANE), lambda b, i: (b, i, 0)),
                  _const_spec((KV_LORA, H_B * LANE)), _const_spec((KV_LORA, H_B * LANE)), _const_spec((1, LANE))],
        out_specs=[pl.BlockSpec((1, H_B, TL, 2 * LANE), lambda b, i: (b, 0, i, 0)),
                   pl.BlockSpec((1, H_B, TL, 2 * VD_B), lambda b, i: (b, 0, i, 0))],
        out_shape=[jax.ShapeDtypeStruct((B, H_B, L, 2 * LANE), BF16),
                   jax.ShapeDtypeStruct((B, H_B, L, 2 * VD_B), BF16)],
        compiler_params=pltpu.CompilerParams(dimension_semantics=("arbitrary", "arbitrary"),
                                             vmem_limit_bytes=VMEM_LIMIT),
        name="kv_up",
    )(ckv_all, krp_all, w["w_kvk"], w["w_kvv"], w["mla_kn_nope"])


def _attn_kernel(n_heads, group, q_ref, k_ref, v_ref, o_ref):
    dv = v_ref.shape[-1] // 2
    for hh in range(n_heads):
        q = q_ref[0, hh]
        k = k_ref[0, hh // group]
        v = v_ref[0, hh // group]
        s = lax.dot_general(q, k, (((1,), (1,)), ((), ())), preferred_element_type=F32)
        p = jnp.exp2(s - jnp.max(s, axis=-1, keepdims=True))
        o = _dot(p.astype(BF16), v)
        o_ref[0, :, hh * dv:(hh + 1) * dv] = (o[:, :dv] / o[:, dv:]).astype(BF16)


def _attention(q, k, v, name):
    B, H, S, dk = q.shape
    _, Hk, L, dv2 = v.shape
    dv = dv2 // 2
    return pl.pallas_call(
        functools.partial(_attn_kernel, H, H // Hk),
        grid=(B, S // TQ),
        in_specs=[pl.BlockSpec((1, H, TQ, dk), lambda b, i: (b, 0, i, 0)),
                  pl.BlockSpec((1, Hk, L, dk), lambda b, i: (b, 0, 0, 0)),
                  pl.BlockSpec((1, Hk, L, dv2), lambda b, i: (b, 0, 0, 0))],
        out_specs=pl.BlockSpec((1, TQ, H * dv), lambda b, i: (b, i, 0)),
        out_shape=jax.ShapeDtypeStruct((B, S, H * dv), BF16),
        compiler_params=pltpu.CompilerParams(dimension_semantics=("arbitrary", "arbitrary"),
                                             vmem_limit_bytes=VMEM_LIMIT),
        name=name,
    )(q, k, v)


def _route(bi, sc, carry):
    n_tok = bi.shape[1]
    iota8 = lax.broadcasted_iota(I32, (GROUP_SIZE, n_tok), 0)
    neg = -jnp.inf
    grp = [bi[j * GROUP_SIZE:(j + 1) * GROUP_SIZE] for j in range(N_GROUPS)]
    gscore = []
    for g in grp:
        m1 = jnp.max(g, axis=0, keepdims=True)
        first = jnp.min(jnp.where(g == m1, iota8, GROUP_SIZE), axis=0, keepdims=True)
        m2 = jnp.max(jnp.where(iota8 == first, neg, g), axis=0, keepdims=True)
        gscore.append(m1 + m2)
    masked = []
    for j in range(N_GROUPS):
        cnt = jnp.zeros((1, n_tok), I32)
        for j2 in range(N_GROUPS):
            if j2 == j:
                continue
            beats = (gscore[j2] > gscore[j]) if j2 > j else (gscore[j2] >= gscore[j])
            cnt = cnt + beats.astype(I32)
        keep = jnp.broadcast_to(cnt, grp[j].shape) < TOPK_GROUPS
        masked.append(jnp.where(keep, grp[j], neg))
    cnts = [jnp.zeros((GROUP_SIZE, n_tok), I32) for _ in range(N_GROUPS)]
    for e2 in range(N_EXPERTS):
        j2, r2 = divmod(e2, GROUP_SIZE)
        row = masked[j2][r2:r2 + 1]
        for j in range(N_GROUPS):
            if j > j2:
                beats = (row >= masked[j]).astype(I32)
            elif j < j2:
                beats = (row > masked[j]).astype(I32)
            else:
                beats = jnp.where(iota8 > r2, (row >= masked[j]).astype(I32), (row > masked[j]).astype(I32))
            cnts[j] = cnts[j] + beats
    picked = [jnp.where(cnts[j] < TOP_K, sc[j * GROUP_SIZE:(j + 1) * GROUP_SIZE], 0.0) for j in range(N_GROUPS)]
    total = picked[0].sum(axis=0, keepdims=True)
    for j in range(1, N_GROUPS):
        total = total + picked[j].sum(axis=0, keepdims=True)
    wts = [picked[j] / total * ROUTED_SCALE for j in range(N_GROUPS)]
    kept = jnp.concatenate([(c < TOP_K).astype(F32) for c in cnts], axis=0)
    earlier = (lax.broadcasted_iota(I32, (n_tok, n_tok), 0) < lax.broadcasted_iota(I32, (n_tok, n_tok), 1))
    rank = _dot(kept.astype(BF16), earlier.astype(F32).astype(BF16)) + carry
    new_carry = carry + jnp.sum(kept, axis=1, keepdims=True)
    idx_rows = jnp.zeros((TOP_K, n_tok), F32)
    w_rows = jnp.zeros((TOP_K, n_tok), F32)
    rank_rows = jnp.zeros((TOP_K, n_tok), F32)
    for k in range(TOP_K):
        ia = jnp.zeros((GROUP_SIZE, n_tok), F32)
        wa = jnp.zeros((GROUP_SIZE, n_tok), F32)
        ra = jnp.zeros((GROUP_SIZE, n_tok), F32)
        for j in range(N_GROUPS):
            hit = cnts[j] == k
            ia = ia + jnp.where(hit, (iota8 + j * GROUP_SIZE).astype(F32), 0.0)
            wa = wa + jnp.where(hit, wts[j], 0.0)
            ra = ra + jnp.where(hit, rank[j * GROUP_SIZE:(j + 1) * GROUP_SIZE], 0.0)
        idx_rows = jnp.where(iota8 == k, jnp.sum(ia, axis=0, keepdims=True), idx_rows)
        w_rows = jnp.where(iota8 == k, jnp.sum(wa, axis=0, keepdims=True), w_rows)
        rank_rows = jnp.where(iota8 == k, jnp.sum(ra, axis=0, keepdims=True), rank_rows)
    return idx_rows, w_rows, rank_rows, new_carry


def _post_kernel(tiles_per_group, oa_ref, ob_ref, ga_ref, gb_ref, x_ref, mod_ref, g2n_ref, woa_ref, wob_ref,
                 wout_ref, wr_ref, rb_ref, ws1_ref, ws3_ref, ws2_ref, base_ref, hp_ref, idx_ref, wk_ref, rank_ref,
                 carry_ref):
    tile = pl.program_id(0) * pl.num_programs(1) + pl.program_id(1)

    @pl.when(tile % tiles_per_group == 0)
    def _():
        carry_ref[...] = jnp.zeros_like(carry_ref)

    m = mod_ref[0]
    g1, sh2, sc2, g2 = m[2:3], m[3:4], m[4:5], m[5:6]
    merged = ga_ref[0] * _dot(oa_ref[0], woa_ref[...]) + gb_ref[0] * _dot(ob_ref[0], wob_ref[...])
    x1 = x_ref[0] + g1 * _dot(merged.astype(BF16), wout_ref[...])
    h2 = (_rms(x1, D) * g2n_ref[...]) * (1.0 + sc2) + sh2
    h2b = h2.astype(BF16)
    act = _silu(_dot(h2b, ws1_ref[...])) * _dot(h2b, ws3_ref[...])
    base_ref[0] = x1 + g2 * _dot(act.astype(BF16), ws2_ref[...])
    n_tok = h2.shape[0]
    for j in range(HP_ROWS):
        hp_ref[pl.ds(j, n_tok, stride=HP_ROWS), :] = h2[:, j * LANE:(j + 1) * LANE]
    logits = jnp.dot(h2, wr_ref[...], preferred_element_type=F32, precision=HIGHEST)
    sc = jax.nn.sigmoid(logits.T[0:N_EXPERTS])
    idx_rows, w_rows, rank_rows, carry = _route(sc + rb_ref[...], sc, carry_ref[...])
    carry_ref[...] = carry
    idx_ref[...] = idx_rows.astype(I32)
    wk_ref[...] = w_rows
    rank_ref[...] = rank_rows.astype(I32)


def _post(oa, ob, ga, gb, x, modsel, per_batch, w, gt):
    B, S, _ = x.shape
    nt = S // TM
    mod_idx = (lambda b, i: (b, 0, 0)) if per_batch else (lambda b, i: (0, 0, 0))
    tok = lambda d: pl.BlockSpec((1, TM, d), lambda b, i: (b, i, 0))
    slot = lambda: pl.BlockSpec((TOP_K, TM), lambda b, i: (0, b * nt + i))
    return pl.pallas_call(
        functools.partial(_post_kernel, gt // TM),
        grid=(B, nt),
        in_specs=[tok(D), tok(D), tok(D), tok(D), tok(D), pl.BlockSpec((1, N_MOD, D), mod_idx),
                  _const_spec((1, D)), _const_spec((D, D)), _const_spec((D, D)), _const_spec((D, D)),
                  _const_spec((D, LANE)), _const_spec((N_EXPERTS, 1)),
                  _const_spec((D, D_EXPERT)), _const_spec((D, D_EXPERT)), _const_spec((D_EXPERT, D))],
        out_specs=[tok(D),
                   pl.BlockSpec((TM * HP_ROWS, LANE), lambda b, i: (b * nt + i, 0)),
                   slot(), slot(), slot()],
        out_shape=[jax.ShapeDtypeStruct((B, S, D), F32),
                   jax.ShapeDtypeStruct((B * S * HP_ROWS, LANE), F32),
                   jax.ShapeDtypeStruct((TOP_K, B * S), I32),
                   jax.ShapeDtypeStruct((TOP_K, B * S), F32),
                   jax.ShapeDtypeStruct((TOP_K, B * S), I32)],
        scratch_shapes=[pltpu.VMEM((N_EXPERTS, 1), F32)],
        compiler_params=pltpu.CompilerParams(dimension_semantics=("arbitrary", "arbitrary"),
                                             vmem_limit_bytes=VMEM_LIMIT),
        name="post",
    )(oa, ob, ga, gb, x, modsel, w["norm2_g"], w["w_oa"], w["w_ob"], w["w_out"], w["w_router"],
      w["router_bias"], w["w_s1"], w["w_s3"], w["w_s2"])


def _dispatch_rows(idx, wk, rank, n_groups):
    T = idx.shape[1]
    gt = T // n_groups
    experts = jnp.arange(N_EXPERTS, dtype=I32)
    regroup = lambda a: a.reshape(TOP_K, n_groups, gt).transpose(1, 0, 2)
    eid = regroup(idx)
    onehot = (eid[:, None] == experts[None, :, None, None]).astype(I32)
    counts = jnp.sum(onehot, axis=(2, 3))
    padded = (counts + (MB - 1)) // MB * MB
    bounds = jnp.concatenate([jnp.zeros((n_groups, 1), I32), jnp.cumsum(padded, axis=1)], axis=1)
    row = jnp.sum(onehot * bounds[:, :N_EXPERTS, None, None], axis=1) + regroup(rank)
    wtab = jnp.pad(regroup(wk), ((0, 0), (0, 0), (0, gt)))
    return bounds, counts, row.reshape(n_groups, TOP_K * gt), wtab.reshape(n_groups, TOP_K * 2 * gt)


def _moe_kernel(gt, seg_ref, cnt_ref, row_ref, wtab_ref, hp_ref, w1_ref, w3_ref, w2_ref, o_ref,
                tin0_ref, tin1_ref, tout0_ref, tout1_ref, w1b_ref, w3b_ref, w2b_ref, code_ref):
    tin = (tin0_ref, tin1_ref)
    tout = (tout0_ref, tout1_ref)

    def build_row_list():
        def spare_tail(ex, carry):
            def one(p, c):
                code_ref[p] = gt
                return c
            return lax.fori_loop(seg_ref[ex] + cnt_ref[ex], seg_ref[ex + 1], one, carry)

        lax.fori_loop(0, N_EXPERTS, spare_tail, 0)

        def chunk(i, carry):
            t0 = i * LIST_CHUNK
            for u in range(LIST_CHUNK):
                for k in range(TOP_K):
                    code_ref[row_ref[k * gt + t0 + u]] = k * 2 * gt + t0 + u
            return carry

        lax.fori_loop(0, gt // LIST_CHUNK, chunk, 0)

    e = pl.program_id(0)
    n_total = lax.shift_right_logical(seg_ref[N_EXPERTS], MB_LOG2)

    def gather(b, slot):
        r0 = b * MB
        for m in range(MB):
            t = code_ref[r0 + m] & (gt - 1)
            slab = hp_ref[pl.ds(pl.multiple_of(t * HP_ROWS, HP_ROWS), HP_ROWS), :]
            tin[slot][pl.ds(m, HP_ROWS, stride=S_IN), :] = slab

    def scatter(b, slot, live):
        r0 = b * MB
        for b0 in range(0, MB, RMW_BATCH):
            pending = []
            for m in range(b0, b0 + RMW_BATCH):
                code = code_ref[r0 + m]
                off = pl.multiple_of((code & (2 * gt - 1)) * OUT_ROWS, OUT_ROWS)
                upd = tout[slot][pl.ds(m, OUT_ROWS, stride=S_OUT), :] * wtab_ref[code]
                if live is not None:
                    upd = jnp.where(live, upd, 0.0)
                pending.append((off, o_ref[pl.ds(off, OUT_ROWS), :] + upd))
            for off, v in pending:
                o_ref[pl.ds(off, OUT_ROWS), :] = v

    @pl.when(e == 0)
    def _():
        build_row_list()
        o_ref[...] = jnp.zeros_like(o_ref)
        tout1_ref[...] = jnp.zeros_like(tout1_ref)
        gather(0, 0)

    w1b_ref[...] = w1_ref[0].astype(BF16)
    w3b_ref[...] = w3_ref[0].astype(BF16)
    w2b_ref[...] = w2_ref[0].astype(BF16)
    first = lax.shift_right_logical(seg_ref[e], MB_LOG2)
    n_blocks = lax.shift_right_logical(seg_ref[e + 1], MB_LOG2) - first

    def stages(b, slot):
        gather(jnp.minimum(b + 1, n_total - 1), 1 - slot)
        x = jnp.concatenate([tin[slot][pl.ds(j * S_IN, MB), :] for j in range(HP_ROWS)], axis=-1).astype(BF16)
        act = _silu(_dot(x, w1b_ref[...])) * _dot(x, w3b_ref[...])
        y = _dot(act.astype(BF16), w2b_ref[...])
        for j in range(OUT_ROWS):
            tout[slot][pl.ds(j * S_OUT, MB), :] = y[:, j * LANE:(j + 1) * LANE]
        scatter(jnp.maximum(b - 1, 0), 1 - slot, b > 0)

    def block(i, carry):
        b = first + i
        for slot in range(2):
            @pl.when(b & 1 == slot)
            def _(slot=slot):
                stages(b, slot)
        return carry

    lax.fori_loop(0, n_blocks, block, 0)

    @pl.when(e == N_EXPERTS - 1)
    def _():
        for slot in range(2):
            @pl.when((n_total - 1) & 1 == slot)
            def _(slot=slot):
                scatter(n_total - 1, slot, None)


def _moe(hp, group, gt, seg, counts, rows, wtab, w):
    n_rows = TOP_K * gt + N_EXPERTS * MB
    grid_spec = pltpu.PrefetchScalarGridSpec(
        num_scalar_prefetch=4,
        grid=(N_EXPERTS,),
        in_specs=[pl.BlockSpec((gt * HP_ROWS, LANE), lambda e, *_: (group, 0), pipeline_mode=pl.Buffered(1)),
                  pl.BlockSpec((1, D, D_EXPERT), lambda e, *_: (e, 0, 0)),
                  pl.BlockSpec((1, D, D_EXPERT), lambda e, *_: (e, 0, 0)),
                  pl.BlockSpec((1, D_EXPERT, D), lambda e, *_: (e, 0, 0))],
        out_specs=pl.BlockSpec(((gt + 1) * OUT_ROWS, LANE), lambda e, *_: (0, 0), pipeline_mode=pl.Buffered(1)),
        scratch_shapes=[pltpu.VMEM((HP_ROWS * S_IN, LANE), F32), pltpu.VMEM((HP_ROWS * S_IN, LANE), F32),
                        pltpu.VMEM((OUT_ROWS * S_OUT, LANE), F32), pltpu.VMEM((OUT_ROWS * S_OUT, LANE), F32),
                        pltpu.VMEM((D, D_EXPERT), BF16), pltpu.VMEM((D, D_EXPERT), BF16),
                        pltpu.VMEM((D_EXPERT, D), BF16),
                        pltpu.SMEM((n_rows,), I32)])
    return pl.pallas_call(
        functools.partial(_moe_kernel, gt),
        grid_spec=grid_spec,
        out_shape=jax.ShapeDtypeStruct(((gt + 1) * OUT_ROWS, LANE), F32),
        compiler_params=pltpu.CompilerParams(dimension_semantics=("arbitrary",), vmem_limit_bytes=VMEM_LIMIT),
        name="moe_experts",
    )(seg, counts, rows, wtab, hp, w["w_e1"], w["w_e3"], w["w_e2"])


def _combine_kernel(n_groups, tiles_per_group, *refs):
    r_refs = refs[:n_groups]
    base_ref, mod_ref, o_ref = refs[n_groups:]
    tile = pl.program_id(0) * pl.num_programs(1) + pl.program_id(1)
    g2 = mod_ref[0][5:6]
    n_tok = base_ref.shape[1]
    for g, r_ref in enumerate(r_refs):
        @pl.when(tile // tiles_per_group == g)
        def _(r_ref=r_ref):
            for s in range(OUT_ROWS):
                cols = slice(s * LANE, (s + 1) * LANE)
                o_ref[0, :, cols] = base_ref[0, :, cols] + g2[:, cols] * r_ref[pl.ds(s, n_tok, stride=OUT_ROWS), :]


def _combine(routed, base, modsel, per_batch, gt):
    B, S, _ = base.shape
    nt = S // TM
    tiles_per_group = gt // TM
    mod_idx = (lambda b, i: (b, 0, 0)) if per_batch else (lambda b, i: (0, 0, 0))

    def r_spec(g):
        def idx(b, i):
            tile = b * nt + i
            return (jnp.where(tile // tiles_per_group == g, tile % tiles_per_group, 0), 0)
        return pl.BlockSpec((TM * OUT_ROWS, LANE), idx)

    return pl.pallas_call(
        functools.partial(_combine_kernel, len(routed), tiles_per_group),
        grid=(B, nt),
        in_specs=[r_spec(g) for g in range(len(routed))] + [
            pl.BlockSpec((1, TM, D), lambda b, i: (b, i, 0)),
            pl.BlockSpec((1, N_MOD, D), mod_idx)],
        out_specs=pl.BlockSpec((1, TM, D), lambda b, i: (b, i, 0)),
        out_shape=jax.ShapeDtypeStruct((B, S, D), F32),
        compiler_params=pltpu.CompilerParams(dimension_semantics=("arbitrary", "arbitrary"),
                                             vmem_limit_bytes=VMEM_LIMIT),
        name="moe_combine",
    )(*routed, base, modsel)


def _trunk_pass(x, modsel, per_batch, w, rope_tabs, cache):
    B, S, _ = x.shape
    qa, ka, va, qb, ckv, krp, ga, gb = _project(x, modsel, per_batch, w, rope_tabs)
    if cache is None:
        ka_all, va_all, ckv_all, krp_all = ka, va, ckv, krp
    else:
        ck, cv, cckv, ckr = cache
        ka_all = jnp.concatenate([ck, ka], axis=2)
        va_all = jnp.concatenate([cv, va], axis=2)
        ckv_all = jnp.concatenate([cckv, ckv], axis=1)
        krp_all = jnp.concatenate([_pad64(ckr), krp], axis=1)
    kb, vb = _kv_up(ckv_all, krp_all, w)
    va_ones = jnp.concatenate([va_all.astype(BF16), jnp.ones(va_all.shape, BF16)], axis=-1)
    oa = _attention(qa, ka_all.astype(BF16), va_ones, "attn_gqa")
    ob = _attention(qb, kb, vb, "attn_mla")
    T = B * S
    gt = min(GROUP_TOKENS, T)
    assert T % gt == 0 and gt % TM == 0 and gt & (gt - 1) == 0 and gt >= MB and gt % LIST_CHUNK == 0
    base, hp, idx, wk, rank = _post(oa, ob, ga, gb, x, modsel, per_batch, w, gt)
    seg, counts, rows, wtab = _dispatch_rows(idx, wk, rank, T // gt)
    routed = [_moe(hp, g, gt, seg[g], counts[g], rows[g], wtab[g], w) for g in range(T // gt)]
    y = _combine(routed, base, modsel, per_batch, gt)
    return y, (ka, va, ckv, _unpad64(krp))


def _rope_tables(n_tokens):
    t = jnp.arange(n_tokens)
    row = (t // GRID_W).astype(F32)[:, None]
    col = (t % GRID_W).astype(F32)[:, None]

    def tabs(rot_dim):
        axis_dim = rot_dim // 2
        inv = ROPE_THETA ** (-jnp.arange(0, axis_dim, 2, dtype=F32) / axis_dim)
        ang = jnp.concatenate([row * inv, col * inv], axis=-1)
        return jnp.cos(ang), jnp.sin(ang)

    cos_a, sin_a = tabs(HD_A)
    cos_b, sin_b = tabs(ROPE_B)
    return (jnp.concatenate([cos_a, cos_a], -1), jnp.concatenate([-sin_a, sin_a], -1),
            _pad64(jnp.concatenate([cos_b, cos_b], -1)), _pad64(jnp.concatenate([-sin_b, sin_b], -1)))


def _layer_weights(l, w_in, w_qb, w_kvb, w_router, router_bias, named):
    w = {k: v[l] for k, v in named.items()}
    wi = w_in[l]
    w["w_in"] = jnp.concatenate([wi[:, :C_KR], _pad64(wi[:, C_KR:C_KR + ROPE_B]), wi[:, C_KR + ROPE_B:]],
                                axis=1).astype(BF16)
    wq = w_qb[l].reshape(Q_LORA, H_B, NOPE_B + ROPE_B)
    w["w_qn"] = wq[:, :, :NOPE_B].reshape(Q_LORA, H_B * LANE).astype(BF16)
    w["w_qr"] = _pad64(wq[:, :, NOPE_B:]).reshape(Q_LORA, H_B * LANE).astype(BF16)
    wkv = w_kvb[l].reshape(KV_LORA, H_B, NOPE_B + VD_B)
    w["w_kvk"] = wkv[:, :, :NOPE_B].reshape(KV_LORA, H_B * LANE).astype(BF16)
    w["w_kvv"] = wkv[:, :, NOPE_B:].reshape(KV_LORA, H_B * LANE).astype(BF16)
    w["w_router"] = jnp.pad(w_router[l], ((0, 0), (0, LANE - N_EXPERTS)))
    w["router_bias"] = router_bias[l].reshape(N_EXPERTS, 1)
    for k in ("norm1_g", "norm2_g", "gqa_qn", "gqa_kn", "mla_qa_g", "mla_kva_g", "mla_qn_nope", "mla_kn_nope"):
        w[k] = w[k].reshape(1, -1)
    for k in ("mla_qn_rope", "mla_kn_rope"):
        w[k] = _pad64(w[k]).reshape(1, LANE)
    for k in ("w_oa", "w_ob", "w_out", "w_s1", "w_s3", "w_s2"):
        w[k] = w[k].astype(BF16)
    return w


def kernel(x_prompt, x_sample, c, cache_gqa_k, cache_gqa_v, cache_mla_ckv, cache_mla_krope, c_ctx, w_mod, b_mod, norm1_g, norm2_g, w_in, gqa_qn, gqa_kn, mla_qa_g, mla_kva_g, w_qb, w_kvb, mla_qn_nope, mla_qn_rope, mla_kn_nope, mla_kn_rope, w_oa, w_ob, w_out, w_router, router_bias, w_e1, w_e3, w_e2, w_s1, w_s3, w_s2):
    depth = w_mod.shape[0]
    n_dec = x_sample.shape[0]
    rope_tabs = _rope_tables(x_sample.shape[1])
    cond8 = jnp.concatenate([c_ctx[None, :], c, jnp.zeros((8 - 1 - n_dec, D), F32)], axis=0)
    named = dict(norm1_g=norm1_g, norm2_g=norm2_g, gqa_qn=gqa_qn, gqa_kn=gqa_kn, mla_qa_g=mla_qa_g,
                 mla_kva_g=mla_kva_g, mla_qn_nope=mla_qn_nope, mla_qn_rope=mla_qn_rope, mla_kn_nope=mla_kn_nope,
                 mla_kn_rope=mla_kn_rope, w_oa=w_oa, w_ob=w_ob, w_out=w_out, w_e1=w_e1, w_e3=w_e3, w_e2=w_e2,
                 w_s1=w_s1, w_s3=w_s3, w_s2=w_s2)
    xp, xs = x_prompt, x_sample
    ks, vs, ckvs, krs = [], [], [], []
    for l in range(depth):
        w = _layer_weights(l, w_in, w_qb, w_kvb, w_router, router_bias, named)
        mod = _modulation(cond8, w_mod[l], b_mod[l].reshape(1, -1))
        mod_ctx = mod[0:1].reshape(1, N_MOD, D)
        mod_dec = mod[1:1 + n_dec].reshape(n_dec, N_MOD, D)
        xp, ctx = _trunk_pass(xp, mod_ctx, False, w, None, None)
        ks.append(ctx[0])
        vs.append(ctx[1])
        ckvs.append(ctx[2])
        krs.append(ctx[3])
        cache = (cache_gqa_k[:, l], cache_gqa_v[:, l], cache_mla_ckv[:, l], cache_mla_krope[:, l])
        xs, _ = _trunk_pass(xs, mod_dec, True, w, rope_tabs, cache)
    return (xp, xs, jnp.stack(ks, axis=1), jnp.stack(vs, axis=1), jnp.stack(ckvs, axis=1), jnp.stack(krs, axis=1))
```

```python
import functools
import math

import jax
import jax.numpy as jnp
from jax import lax
from jax.experimental import pallas as pl
from jax.experimental.pallas import tpu as pltpu

F32 = jnp.float32
BF16 = jnp.bfloat16
I32 = jnp.int32
HIGHEST = lax.Precision.HIGHEST

D = 1024
GRID_W = 64
RMS_EPS = 1e-6
ROPE_THETA = 10000.0
N_MOD = 6
H_A, KV_A, HD_A = 8, 2, 128
H_B, Q_LORA, KV_LORA, NOPE_B, ROPE_B, VD_B = 8, 256, 128, 128, 64, 128
N_EXPERTS, TOP_K, N_GROUPS, TOPK_GROUPS = 64, 8, 8, 4
GROUP_SIZE = N_EXPERTS // N_GROUPS
D_EXPERT = 256
ROUTED_SCALE = 2.5
QSCALE_A = HD_A ** -0.5 * math.log2(math.e)
QSCALE_B = (NOPE_B + ROPE_B) ** -0.5 * math.log2(math.e)

LANE = 128
C_QA, C_KA, C_VA, C_CQ, C_CKV, C_KR, C_GA, C_GB, C_END = 0, 1024, 1280, 1536, 1792, 1920, 2048, 3072, 4096
VMEM_LIMIT = 52 * 1024 * 1024

TM = 256
TQ = 256
TL = 256
GROUP_TOKENS = 4096
MB, MB_LOG2 = 128, 7
HP_ROWS = D // LANE
OUT_ROWS = D // LANE
S_IN = MB + 1
S_OUT = MB + 1
RMW_BATCH = 8
LIST_CHUNK = 16


def _dot(a, b):
    return jnp.dot(a, b, preferred_element_type=F32)


def _rms(t, n_valid):
    ms = jnp.sum(t * t, axis=-1, keepdims=True) * (1.0 / n_valid)
    return t * lax.rsqrt(ms + RMS_EPS)


def _silu(t):
    return t * jax.nn.sigmoid(t)


def _pad64(a):
    z = jnp.zeros(a.shape[:-1] + (32,), a.dtype)
    return jnp.concatenate([a[..., :32], z, a[..., 32:], z], axis=-1)


def _unpad64(a):
    return jnp.concatenate([a[..., :32], a[..., 64:96]], axis=-1)


def _const_spec(shape):
    nd = len(shape)
    return pl.BlockSpec(shape, lambda *_: (0,) * nd)


def _mod_kernel(c_ref, w_ref, b_ref, o_ref):
    s = _silu(c_ref[...])
    o_ref[...] = jnp.dot(s, w_ref[...], preferred_element_type=F32, precision=HIGHEST) + b_ref[...]


def _modulation(cond8, w_mod, b_mod):
    tn = 1024
    n = w_mod.shape[1]
    return pl.pallas_call(
        _mod_kernel,
        grid=(n // tn,),
        in_specs=[pl.BlockSpec((8, D), lambda j: (0, 0)),
                  pl.BlockSpec((D, tn), lambda j: (0, j)),
                  pl.BlockSpec((1, tn), lambda j: (0, j))],
        out_specs=pl.BlockSpec((8, tn), lambda j: (0, j)),
        out_shape=jax.ShapeDtypeStruct((8, n), F32),
        compiler_params=pltpu.CompilerParams(dimension_semantics=("arbitrary",), vmem_limit_bytes=VMEM_LIMIT),
        name="adaln_mod",
    )(cond8, w_mod, b_mod)


def _proj_kernel(use_rope, *refs):
    (x_ref, mod_ref, g1_ref, win_ref, wgate_ref, wqn_ref, wqr_ref,
     gqn_ref, gkn_ref, gcq_ref, gckv_ref, gnope_ref, gqr_ref, gkr_ref) = refs[:14]
    n_in = 18 if use_rope else 14
    if use_rope:
        ca_ref, sa_ref, cb_ref, sb_ref = refs[14:18]
    qa_ref, ka_ref, va_ref, qb_ref, ckv_ref, kr_ref, ga_ref, gb_ref = refs[n_in:n_in + 8]
    zbuf = refs[n_in + 8:n_in + 10]
    qbuf = refs[n_in + 10:n_in + 12]
    step = pl.program_id(0)

    def rope(t, c_ref, s_ref):
        return t * c_ref[...] + pltpu.roll(t, LANE // 2, 1) * s_ref[...]

    rope_a = (lambda t: rope(t, ca_ref, sa_ref)) if use_rope else (lambda t: t)
    rope_b = (lambda t: rope(t, cb_ref, sb_ref)) if use_rope else (lambda t: t)

    @pl.when(step == 0)
    def _():
        zbuf[1][...] = jnp.zeros_like(zbuf[1])
        qbuf[1][...] = jnp.zeros_like(qbuf[1])

    def matmuls(z_ref, q_ref):
        x = x_ref[0]
        m = mod_ref[0]
        sh1, sc1 = m[0:1], m[1:2]
        hb = ((_rms(x, D) * g1_ref[...]) * (1.0 + sc1) + sh1).astype(BF16)
        for lo, hi in ((C_QA, C_KA), (C_KA, C_CQ), (C_CKV, C_GA)):
            z_ref[:, lo:hi] = _dot(hb, win_ref[:, lo:hi])
        for lo, hi in ((C_GA, C_GB), (C_GB, C_END)):
            z_ref[:, lo:hi] = _dot(hb, wgate_ref[:, lo - C_GA:hi - C_GA])
        cq = (_rms(_dot(hb, win_ref[:, C_CQ:C_CKV]), Q_LORA) * gcq_ref[...]).astype(BF16)
        q_ref[:, 0:H_B * LANE] = _dot(cq, wqn_ref[...])
        q_ref[:, H_B * LANE:2 * H_B * LANE] = _dot(cq, wqr_ref[...])

    def epilogue(z_ref, q_ref):
        for hh in range(H_A):
            t = _rms(z_ref[:, C_QA + hh * HD_A:C_QA + (hh + 1) * HD_A], HD_A) * gqn_ref[...]
            qa_ref[0, hh] = (rope_a(t) * QSCALE_A).astype(BF16)
        for j in range(KV_A):
            t = _rms(z_ref[:, C_KA + j * HD_A:C_KA + (j + 1) * HD_A], HD_A) * gkn_ref[...]
            ka_ref[0, j] = rope_a(t)
            va_ref[0, j] = z_ref[:, C_VA + j * HD_A:C_VA + (j + 1) * HD_A]
        for hh in range(H_B):
            t = _rms(q_ref[:, hh * LANE:(hh + 1) * LANE], NOPE_B) * gnope_ref[...]
            qb_ref[0, hh, :, 0:LANE] = (t * QSCALE_B).astype(BF16)
            t = _rms(q_ref[:, (H_B + hh) * LANE:(H_B + hh + 1) * LANE], ROPE_B) * gqr_ref[...]
            qb_ref[0, hh, :, LANE:2 * LANE] = (rope_b(t) * QSCALE_B).astype(BF16)
        ckv_ref[0] = _rms(z_ref[:, C_CKV:C_KR], KV_LORA) * gckv_ref[...]
        t = _rms(z_ref[:, C_KR:C_GA], ROPE_B) * gkr_ref[...]
        lane = lax.broadcasted_iota(I32, t.shape, 1)
        second = jnp.where((lane >= LANE // 2) & (lane < 3 * LANE // 4), pltpu.roll(t, LANE // 4, 1), 0.0)
        kr_ref[0] = rope_b(jnp.where(lane < LANE // 4, t, second))
        ga_ref[0] = jax.nn.sigmoid(z_ref[:, C_GA:C_GB])
        gb_ref[0] = jax.nn.sigmoid(z_ref[:, C_GB:C_END])

    for parity in range(2):
        @pl.when(step % 2 == parity)
        def _(parity=parity):
            matmuls(zbuf[parity], qbuf[parity])
            epilogue(zbuf[1 - parity], qbuf[1 - parity])


def _project(x, modsel, per_batch, w, rope_tabs):
    B, S, _ = x.shape
    use_rope = rope_tabs is not None
    nt = S // TM
    n_tiles = B * nt
    cur = lambda s: jnp.minimum(s, n_tiles - 1)
    prev = lambda s: jnp.maximum(s - 1, 0)
    mod_idx = (lambda s: (cur(s) // nt, 0, 0)) if per_batch else (lambda s: (0, 0, 0))
    in_specs = [pl.BlockSpec((1, TM, D), lambda s: (cur(s) // nt, cur(s) % nt, 0)),
                pl.BlockSpec((1, N_MOD, D), mod_idx),
                _const_spec((1, D)), _const_spec((D, C_GA)), _const_spec((D, C_END - C_GA)),
                _const_spec((Q_LORA, H_B * LANE)), _const_spec((Q_LORA, H_B * LANE)),
                _const_spec((1, LANE)), _const_spec((1, LANE)), _const_spec((1, Q_LORA)), _const_spec((1, LANE)),
                _const_spec((1, LANE)), _const_spec((1, LANE)), _const_spec((1, LANE))]
    args = [x, modsel, w["norm1_g"], w["w_in"], w["w_gate"], w["w_qn"], w["w_qr"], w["gqa_qn"], w["gqa_kn"], w["mla_qa_g"],
            w["mla_kva_g"], w["mla_qn_nope"], w["mla_qn_rope"], w["mla_kn_rope"]]
    if use_rope:
        in_specs += [pl.BlockSpec((TM, LANE), lambda s: (prev(s) % nt, 0))] * 4
        args += list(rope_tabs)
    tok4 = lambda h, d: pl.BlockSpec((1, h, TM, d), lambda s: (prev(s) // nt, 0, prev(s) % nt, 0))
    tok3 = lambda d: pl.BlockSpec((1, TM, d), lambda s: (prev(s) // nt, prev(s) % nt, 0))
    out_specs = [tok4(H_A, HD_A), tok4(KV_A, HD_A), tok4(KV_A, HD_A), tok4(H_B, 2 * LANE),
                 tok3(KV_LORA), tok3(LANE), tok3(D), tok3(D)]
    out_shape = [jax.ShapeDtypeStruct((B, H_A, S, HD_A), BF16),
                 jax.ShapeDtypeStruct((B, KV_A, S, HD_A), F32),
                 jax.ShapeDtypeStruct((B, KV_A, S, HD_A), F32),
                 jax.ShapeDtypeStruct((B, H_B, S, 2 * LANE), BF16),
                 jax.ShapeDtypeStruct((B, S, KV_LORA), F32),
                 jax.ShapeDtypeStruct((B, S, LANE), F32),
                 jax.ShapeDtypeStruct((B, S, D), F32),
                 jax.ShapeDtypeStruct((B, S, D), F32)]
    return pl.pallas_call(
        functools.partial(_proj_kernel, use_rope),
        grid=(n_tiles + 1,),
        in_specs=in_specs, out_specs=out_specs, out_shape=out_shape,
        scratch_shapes=[pltpu.VMEM((TM, C_END), F32), pltpu.VMEM((TM, C_END), F32),
                        pltpu.VMEM((TM, 2 * H_B * LANE), F32), pltpu.VMEM((TM, 2 * H_B * LANE), F32)],
        compiler_params=pltpu.CompilerParams(dimension_semantics=("arbitrary",), vmem_limit_bytes=VMEM_LIMIT),
        name="proj_rope" if use_rope else "proj",
    )(*args)


def _attend(n_heads, q_ref, key, value, o_ref):
    for hh in range(n_heads):
        v = value(hh)
        dv = v.shape[-1] // 2
        s = lax.dot_general(q_ref[0, hh], key(hh), (((1,), (1,)), ((), ())), preferred_element_type=F32)
        p = jnp.exp2(s - jnp.max(s, axis=-1, keepdims=True))
        o = _dot(p.astype(BF16), v)
        o_ref[0, :, hh * dv:(hh + 1) * dv] = (o[:, :dv] / o[:, dv:]).astype(BF16)


def _attn_kernel(n_heads, group, q_ref, k_ref, v_ref, o_ref):
    _attend(n_heads, q_ref, lambda hh: k_ref[0, hh // group], lambda hh: v_ref[0, hh // group], o_ref)


def _attn_mla_kernel(q_ref, ckv_ref, kr_ref, wk_ref, wv_ref, gk_ref, o_ref, kb_ref, vb_ref):
    @pl.when(pl.program_id(1) == 0)
    def _():
        def chunk(c, carry):
            rows = pl.ds(pl.multiple_of(c * TL, TL), TL)
            cb = ckv_ref[0, rows, :].astype(BF16)
            k = _dot(cb, wk_ref[...])
            v = _dot(cb, wv_ref[...])
            krb = kr_ref[0, rows, :].astype(BF16)
            for hh in range(H_B):
                t = _rms(k[:, hh * LANE:(hh + 1) * LANE], NOPE_B) * gk_ref[...]
                kb_ref[hh, rows, 0:LANE] = t.astype(BF16)
                kb_ref[hh, rows, LANE:2 * LANE] = krb
                vb_ref[hh, rows, 0:VD_B] = v[:, hh * LANE:(hh + 1) * LANE].astype(BF16)
                vb_ref[hh, rows, VD_B:2 * VD_B] = jnp.ones((TL, VD_B), BF16)
            return carry

        lax.fori_loop(0, ckv_ref.shape[1] // TL, chunk, 0)

    _attend(H_B, q_ref, lambda hh: kb_ref[hh], lambda hh: vb_ref[hh], o_ref)


def _attention_mla(q, ckv_all, krp_all, w):
    B, H, S, dk = q.shape
    L = ckv_all.shape[1]
    return pl.pallas_call(
        _attn_mla_kernel,
        grid=(B, S // TQ),
        in_specs=[pl.BlockSpec((1, H, TQ, dk), lambda b, i: (b, 0, i, 0)),
                  pl.BlockSpec((1, L, KV_LORA), lambda b, i: (b, 0, 0)),
                  pl.BlockSpec((1, L, LANE), lambda b, i: (b, 0, 0)),
                  _const_spec((KV_LORA, H_B * LANE)), _const_spec((KV_LORA, H_B * LANE)), _const_spec((1, LANE))],
        out_specs=pl.BlockSpec((1, TQ, H * VD_B), lambda b, i: (b, i, 0)),
        out_shape=jax.ShapeDtypeStruct((B, S, H * VD_B), BF16),
        scratch_shapes=[pltpu.VMEM((H_B, L, 2 * LANE), BF16), pltpu.VMEM((H_B, L, 2 * VD_B), BF16)],
        compiler_params=pltpu.CompilerParams(dimension_semantics=("arbitrary", "arbitrary"),
                                             vmem_limit_bytes=VMEM_LIMIT),
        name="attn_mla",
    )(q, ckv_all, krp_all, w["w_kvk"], w["w_kvv"], w["mla_kn_nope"])


def _attention(q, k, v, name):
    B, H, S, dk = q.shape
    _, Hk, L, dv2 = v.shape
    dv = dv2 // 2
    return pl.pallas_call(
        functools.partial(_attn_kernel, H, H // Hk),
        grid=(B, S // TQ),
        in_specs=[pl.BlockSpec((1, H, TQ, dk), lambda b, i: (b, 0, i, 0)),
                  pl.BlockSpec((1, Hk, L, dk), lambda b, i: (b, 0, 0, 0)),
                  pl.BlockSpec((1, Hk, L, dv2), lambda b, i: (b, 0, 0, 0))],
        out_specs=pl.BlockSpec((1, TQ, H * dv), lambda b, i: (b, i, 0)),
        out_shape=jax.ShapeDtypeStruct((B, S, H * dv), BF16),
        compiler_params=pltpu.CompilerParams(dimension_semantics=("arbitrary", "arbitrary"),
                                             vmem_limit_bytes=VMEM_LIMIT),
        name=name,
    )(q, k, v)


def _route(bi, sc, carry):
    n_tok = bi.shape[1]
    iota8 = lax.broadcasted_iota(I32, (GROUP_SIZE, n_tok), 0)
    neg = -jnp.inf
    grp = [bi[j * GROUP_SIZE:(j + 1) * GROUP_SIZE] for j in range(N_GROUPS)]
    gscore = []
    for g in grp:
        m1 = jnp.max(g, axis=0, keepdims=True)
        first = jnp.min(jnp.where(g == m1, iota8, GROUP_SIZE), axis=0, keepdims=True)
        m2 = jnp.max(jnp.where(iota8 == first, neg, g), axis=0, keepdims=True)
        gscore.append(m1 + m2)
    masked = []
    for j in range(N_GROUPS):
        cnt = jnp.zeros((1, n_tok), I32)
        for j2 in range(N_GROUPS):
            if j2 == j:
                continue
            beats = (gscore[j2] > gscore[j]) if j2 > j else (gscore[j2] >= gscore[j])
            cnt = cnt + beats.astype(I32)
        keep = jnp.broadcast_to(cnt, grp[j].shape) < TOPK_GROUPS
        masked.append(jnp.where(keep, grp[j], neg))
    eid = [(iota8 + j * GROUP_SIZE).astype(F32) for j in range(N_GROUPS)]
    scg = [sc[j * GROUP_SIZE:(j + 1) * GROUP_SIZE] for j in range(N_GROUPS)]
    work = list(masked)
    kept = [jnp.zeros((GROUP_SIZE, n_tok), F32) for _ in range(N_GROUPS)]
    idx_rows = jnp.zeros((TOP_K, n_tok), F32)
    score_rows = jnp.zeros((TOP_K, n_tok), F32)
    for k in range(TOP_K):
        top = work[0]
        for j in range(1, N_GROUPS):
            top = jnp.maximum(top, work[j])
        top = jnp.max(top, axis=0, keepdims=True)
        cand = jnp.where(work[0] == top, eid[0], float(N_EXPERTS))
        for j in range(1, N_GROUPS):
            cand = jnp.minimum(cand, jnp.where(work[j] == top, eid[j], float(N_EXPERTS)))
        chosen = jnp.min(cand, axis=0, keepdims=True)
        score = jnp.zeros((GROUP_SIZE, n_tok), F32)
        for j in range(N_GROUPS):
            hit = eid[j] == chosen
            score = score + jnp.where(hit, scg[j], 0.0)
            kept[j] = kept[j] + jnp.where(hit, 1.0, 0.0)
            work[j] = jnp.where(hit, neg, work[j])
        idx_rows = jnp.where(iota8 == k, chosen, idx_rows)
        score_rows = jnp.where(iota8 == k, jnp.sum(score, axis=0, keepdims=True), score_rows)
    w_rows = score_rows / jnp.sum(score_rows, axis=0, keepdims=True) * ROUTED_SCALE
    kept = jnp.concatenate(kept, axis=0)
    earlier = (lax.broadcasted_iota(I32, (n_tok, n_tok), 0) < lax.broadcasted_iota(I32, (n_tok, n_tok), 1))
    rank = _dot(kept.astype(BF16), earlier.astype(F32).astype(BF16)) + carry
    new_carry = carry + jnp.sum(kept, axis=1, keepdims=True)
    rank_rows = jnp.zeros((TOP_K, n_tok), F32)
    for k in range(TOP_K):
        ra = jnp.zeros((GROUP_SIZE, n_tok), F32)
        for j in range(N_GROUPS):
            ra = ra + jnp.where(eid[j] == idx_rows[k:k + 1], rank[j * GROUP_SIZE:(j + 1) * GROUP_SIZE], 0.0)
        rank_rows = jnp.where(iota8 == k, jnp.sum(ra, axis=0, keepdims=True), rank_rows)
    return idx_rows, w_rows, rank_rows, new_carry


def _post_kernel(tiles_per_group, oa_ref, ob_ref, ga_ref, gb_ref, x_ref, mod_ref, g2n_ref, woa_ref, wob_ref,
                 wout_ref, wr_ref, rb_ref, ws1_ref, ws3_ref, ws2_ref, base_ref, hp_ref, idx_ref, wk_ref, rank_ref,
                 carry_ref):
    tile = pl.program_id(0) * pl.num_programs(1) + pl.program_id(1)

    @pl.when(tile % tiles_per_group == 0)
    def _():
        carry_ref[...] = jnp.zeros_like(carry_ref)

    m = mod_ref[0]
    g1, sh2, sc2, g2 = m[2:3], m[3:4], m[4:5], m[5:6]
    merged = ga_ref[0] * _dot(oa_ref[0], woa_ref[...]) + gb_ref[0] * _dot(ob_ref[0], wob_ref[...])
    x1 = x_ref[0] + g1 * _dot(merged.astype(BF16), wout_ref[...])
    h2 = (_rms(x1, D) * g2n_ref[...]) * (1.0 + sc2) + sh2
    h2b = h2.astype(BF16)
    act = _silu(_dot(h2b, ws1_ref[...])) * _dot(h2b, ws3_ref[...])
    base_ref[0] = x1 + g2 * _dot(act.astype(BF16), ws2_ref[...])
    n_tok = h2.shape[0]
    for j in range(HP_ROWS):
        hp_ref[pl.ds(j, n_tok, stride=HP_ROWS), :] = h2[:, j * LANE:(j + 1) * LANE]
    logits = jnp.dot(h2, wr_ref[...], preferred_element_type=F32, precision=HIGHEST)
    sc = jax.nn.sigmoid(logits.T[0:N_EXPERTS])
    idx_rows, w_rows, rank_rows, carry = _route(sc + rb_ref[...], sc, carry_ref[...])
    carry_ref[...] = carry
    idx_ref[...] = idx_rows.astype(I32)
    wk_ref[...] = w_rows
    rank_ref[...] = rank_rows.astype(I32)


def _post(oa, ob, ga, gb, x, modsel, per_batch, w, gt):
    B, S, _ = x.shape
    nt = S // TM
    mod_idx = (lambda b, i: (b, 0, 0)) if per_batch else (lambda b, i: (0, 0, 0))
    tok = lambda d: pl.BlockSpec((1, TM, d), lambda b, i: (b, i, 0))
    slot = lambda: pl.BlockSpec((TOP_K, TM), lambda b, i: (0, b * nt + i))
    return pl.pallas_call(
        functools.partial(_post_kernel, gt // TM),
        grid=(B, nt),
        in_specs=[tok(D), tok(D), tok(D), tok(D), tok(D), pl.BlockSpec((1, N_MOD, D), mod_idx),
                  _const_spec((1, D)), _const_spec((D, D)), _const_spec((D, D)), _const_spec((D, D)),
                  _const_spec((D, LANE)), _const_spec((N_EXPERTS, 1)),
                  _const_spec((D, D_EXPERT)), _const_spec((D, D_EXPERT)), _const_spec((D_EXPERT, D))],
        out_specs=[tok(D),
                   pl.BlockSpec((TM * HP_ROWS, LANE), lambda b, i: (b * nt + i, 0)),
                   slot(), slot(), slot()],
        out_shape=[jax.ShapeDtypeStruct((B, S, D), F32),
                   jax.ShapeDtypeStruct((B * S * HP_ROWS, LANE), F32),
                   jax.ShapeDtypeStruct((TOP_K, B * S), I32),
                   jax.ShapeDtypeStruct((TOP_K, B * S), F32),
                   jax.ShapeDtypeStruct((TOP_K, B * S), I32)],
        scratch_shapes=[pltpu.VMEM((N_EXPERTS, 1), F32)],
        compiler_params=pltpu.CompilerParams(dimension_semantics=("arbitrary", "arbitrary"),
                                             vmem_limit_bytes=VMEM_LIMIT),
        name="post",
    )(oa, ob, ga, gb, x, modsel, w["norm2_g"], w["w_oa"], w["w_ob"], w["w_out"], w["w_router"],
      w["router_bias"], w["w_s1"], w["w_s3"], w["w_s2"])


def _dispatch_rows(idx, wk, rank, n_groups):
    T = idx.shape[1]
    gt = T // n_groups
    experts = jnp.arange(N_EXPERTS, dtype=I32)
    regroup = lambda a: a.reshape(TOP_K, n_groups, gt).transpose(1, 0, 2)
    eid = regroup(idx)
    onehot = (eid[:, None] == experts[None, :, None, None]).astype(I32)
    counts = jnp.sum(onehot, axis=(2, 3))
    padded = (counts + (MB - 1)) // MB * MB
    bounds = jnp.concatenate([jnp.zeros((n_groups, 1), I32), jnp.cumsum(padded, axis=1)], axis=1)
    row = jnp.sum(onehot * bounds[:, :N_EXPERTS, None, None], axis=1) + regroup(rank)
    wtab = jnp.pad(regroup(wk), ((0, 0), (0, 0), (0, gt)))
    return bounds, counts, row.reshape(n_groups, TOP_K * gt), wtab.reshape(n_groups, TOP_K * 2 * gt)


def _moe_kernel(gt, seg_ref, cnt_ref, row_ref, wtab_ref, hp_ref, w1_ref, w3_ref, w2_ref, o_ref,
                tin0_ref, tin1_ref, tout0_ref, tout1_ref, w1b_ref, w3b_ref, w2b_ref, code_ref):
    tin = (tin0_ref, tin1_ref)
    tout = (tout0_ref, tout1_ref)

    def build_row_list():
        def spare_tail(ex, carry):
            def one(p, c):
                code_ref[p] = gt
                return c
            return lax.fori_loop(seg_ref[ex] + cnt_ref[ex], seg_ref[ex + 1], one, carry)

        lax.fori_loop(0, N_EXPERTS, spare_tail, 0)

        def chunk(i, carry):
            t0 = i * LIST_CHUNK
            for u in range(LIST_CHUNK):
                for k in range(TOP_K):
                    code_ref[row_ref[k * gt + t0 + u]] = k * 2 * gt + t0 + u
            return carry

        lax.fori_loop(0, gt // LIST_CHUNK, chunk, 0)

    e = pl.program_id(0)
    n_total = lax.shift_right_logical(seg_ref[N_EXPERTS], MB_LOG2)

    def gather(b, slot):
        r0 = b * MB
        for m in range(MB):
            t = code_ref[r0 + m] & (gt - 1)
            slab = hp_ref[pl.ds(pl.multiple_of(t * HP_ROWS, HP_ROWS), HP_ROWS), :]
            tin[slot][pl.ds(m, HP_ROWS, stride=S_IN), :] = slab

    def scatter(b, slot, live):
        r0 = b * MB
        for b0 in range(0, MB, RMW_BATCH):
            pending = []
            for m in range(b0, b0 + RMW_BATCH):
                code = code_ref[r0 + m]
                off = pl.multiple_of((code & (2 * gt - 1)) * OUT_ROWS, OUT_ROWS)
                upd = tout[slot][pl.ds(m, OUT_ROWS, stride=S_OUT), :] * wtab_ref[code]
                if live is not None:
                    upd = jnp.where(live, upd, 0.0)
                pending.append((off, o_ref[pl.ds(off, OUT_ROWS), :] + upd))
            for off, v in pending:
                o_ref[pl.ds(off, OUT_ROWS), :] = v

    @pl.when(e == 0)
    def _():
        build_row_list()
        o_ref[...] = jnp.zeros_like(o_ref)
        tout1_ref[...] = jnp.zeros_like(tout1_ref)
        gather(0, 0)

    w1b_ref[...] = w1_ref[0].astype(BF16)
    w3b_ref[...] = w3_ref[0].astype(BF16)
    w2b_ref[...] = w2_ref[0].astype(BF16)
    first = lax.shift_right_logical(seg_ref[e], MB_LOG2)
    n_blocks = lax.shift_right_logical(seg_ref[e + 1], MB_LOG2) - first

    def stages(b, slot):
        gather(jnp.minimum(b + 1, n_total - 1), 1 - slot)
        x = jnp.concatenate([tin[slot][pl.ds(j * S_IN, MB), :] for j in range(HP_ROWS)], axis=-1).astype(BF16)
        act = _silu(_dot(x, w1b_ref[...])) * _dot(x, w3b_ref[...])
        y = _dot(act.astype(BF16), w2b_ref[...])
        for j in range(OUT_ROWS):
            tout[slot][pl.ds(j * S_OUT, MB), :] = y[:, j * LANE:(j + 1) * LANE]
        scatter(jnp.maximum(b - 1, 0), 1 - slot, b > 0)

    def block(i, carry):
        b = first + i
        for slot in range(2):
            @pl.when(b & 1 == slot)
            def _(slot=slot):
                stages(b, slot)
        return carry

    lax.fori_loop(0, n_blocks, block, 0)

    @pl.when(e == N_EXPERTS - 1)
    def _():
        for slot in range(2):
            @pl.when((n_total - 1) & 1 == slot)
            def _(slot=slot):
                scatter(n_total - 1, slot, None)


def _moe(hp, group, gt, seg, counts, rows, wtab, w):
    n_rows = TOP_K * gt + N_EXPERTS * MB
    grid_spec = pltpu.PrefetchScalarGridSpec(
        num_scalar_prefetch=4,
        grid=(N_EXPERTS,),
        in_specs=[pl.BlockSpec((gt * HP_ROWS, LANE), lambda e, *_: (group, 0), pipeline_mode=pl.Buffered(1)),
                  pl.BlockSpec((1, D, D_EXPERT), lambda e, *_: (e, 0, 0)),
                  pl.BlockSpec((1, D, D_EXPERT), lambda e, *_: (e, 0, 0)),
                  pl.BlockSpec((1, D_EXPERT, D), lambda e, *_: (e, 0, 0))],
        out_specs=pl.BlockSpec(((gt + 1) * OUT_ROWS, LANE), lambda e, *_: (0, 0), pipeline_mode=pl.Buffered(1)),
        scratch_shapes=[pltpu.VMEM((HP_ROWS * S_IN, LANE), F32), pltpu.VMEM((HP_ROWS * S_IN, LANE), F32),
                        pltpu.VMEM((OUT_ROWS * S_OUT, LANE), F32), pltpu.VMEM((OUT_ROWS * S_OUT, LANE), F32),
                        pltpu.VMEM((D, D_EXPERT), BF16), pltpu.VMEM((D, D_EXPERT), BF16),
                        pltpu.VMEM((D_EXPERT, D), BF16),
                        pltpu.SMEM((n_rows,), I32)])
    return pl.pallas_call(
        functools.partial(_moe_kernel, gt),
        grid_spec=grid_spec,
        out_shape=jax.ShapeDtypeStruct(((gt + 1) * OUT_ROWS, LANE), F32),
        compiler_params=pltpu.CompilerParams(dimension_semantics=("arbitrary",), vmem_limit_bytes=VMEM_LIMIT),
        name="moe_experts",
    )(seg, counts, rows, wtab, hp, w["w_e1"], w["w_e3"], w["w_e2"])


def _combine_kernel(n_groups, tiles_per_group, *refs):
    r_refs = refs[:n_groups]
    base_ref, mod_ref, o_ref = refs[n_groups:]
    tile = pl.program_id(0) * pl.num_programs(1) + pl.program_id(1)
    g2 = mod_ref[0][5:6]
    n_tok = base_ref.shape[1]
    for g, r_ref in enumerate(r_refs):
        @pl.when(tile // tiles_per_group == g)
        def _(r_ref=r_ref):
            for s in range(OUT_ROWS):
                cols = slice(s * LANE, (s + 1) * LANE)
                o_ref[0, :, cols] = base_ref[0, :, cols] + g2[:, cols] * r_ref[pl.ds(s, n_tok, stride=OUT_ROWS), :]


def _combine(routed, base, modsel, per_batch, gt):
    B, S, _ = base.shape
    nt = S // TM
    tiles_per_group = gt // TM
    mod_idx = (lambda b, i: (b, 0, 0)) if per_batch else (lambda b, i: (0, 0, 0))

    def r_spec(g):
        def idx(b, i):
            tile = b * nt + i
            return (jnp.where(tile // tiles_per_group == g, tile % tiles_per_group, 0), 0)
        return pl.BlockSpec((TM * OUT_ROWS, LANE), idx)

    return pl.pallas_call(
        functools.partial(_combine_kernel, len(routed), tiles_per_group),
        grid=(B, nt),
        in_specs=[r_spec(g) for g in range(len(routed))] + [
            pl.BlockSpec((1, TM, D), lambda b, i: (b, i, 0)),
            pl.BlockSpec((1, N_MOD, D), mod_idx)],
        out_specs=pl.BlockSpec((1, TM, D), lambda b, i: (b, i, 0)),
        out_shape=jax.ShapeDtypeStruct((B, S, D), F32),
        compiler_params=pltpu.CompilerParams(dimension_semantics=("arbitrary", "arbitrary"),
                                             vmem_limit_bytes=VMEM_LIMIT),
        name="moe_combine",
    )(*routed, base, modsel)


def _trunk_pass(x, modsel, per_batch, w, rope_tabs, cache):
    B, S, _ = x.shape
    qa, ka, va, qb, ckv, krp, ga, gb = _project(x, modsel, per_batch, w, rope_tabs)
    if cache is None:
        ka_all, va_all, ckv_all, krp_all = ka, va, ckv, krp
    else:
        ck, cv, cckv, ckr = cache
        ka_all = jnp.concatenate([ck, ka], axis=2)
        va_all = jnp.concatenate([cv, va], axis=2)
        ckv_all = jnp.concatenate([cckv, ckv], axis=1)
        krp_all = jnp.concatenate([_pad64(ckr), krp], axis=1)
    va_ones = jnp.concatenate([va_all.astype(BF16), jnp.ones(va_all.shape, BF16)], axis=-1)
    oa = _attention(qa, ka_all.astype(BF16), va_ones, "attn_gqa")
    ob = _attention_mla(qb, ckv_all, krp_all, w)
    T = B * S
    gt = min(GROUP_TOKENS, T)
    assert T % gt == 0 and gt % TM == 0 and gt & (gt - 1) == 0 and gt >= MB and gt % LIST_CHUNK == 0
    base, hp, idx, wk, rank = _post(oa, ob, ga, gb, x, modsel, per_batch, w, gt)
    seg, counts, rows, wtab = _dispatch_rows(idx, wk, rank, T // gt)
    routed = [_moe(hp, g, gt, seg[g], counts[g], rows[g], wtab[g], w) for g in range(T // gt)]
    y = _combine(routed, base, modsel, per_batch, gt)
    return y, (ka, va, ckv, _unpad64(krp))


def _rope_tables(n_tokens):
    t = jnp.arange(n_tokens)
    row = (t // GRID_W).astype(F32)[:, None]
    col = (t % GRID_W).astype(F32)[:, None]

    def tabs(rot_dim):
        axis_dim = rot_dim // 2
        inv = ROPE_THETA ** (-jnp.arange(0, axis_dim, 2, dtype=F32) / axis_dim)
        ang = jnp.concatenate([row * inv, col * inv], axis=-1)
        return jnp.cos(ang), jnp.sin(ang)

    cos_a, sin_a = tabs(HD_A)
    cos_b, sin_b = tabs(ROPE_B)
    return (jnp.concatenate([cos_a, cos_a], -1), jnp.concatenate([-sin_a, sin_a], -1),
            _pad64(jnp.concatenate([cos_b, cos_b], -1)), _pad64(jnp.concatenate([-sin_b, sin_b], -1)))


def _layer_weights(l, w_in, w_qb, w_kvb, w_router, router_bias, named):
    w = {k: v[l] for k, v in named.items()}
    wi = w_in[l]
    n_front = C_KR + ROPE_B
    w["w_in"] = jnp.pad(wi[:, :n_front].astype(BF16), ((0, 0), (0, C_GA - n_front)))
    w["w_gate"] = wi[:, n_front:].astype(BF16)
    wq = w_qb[l].reshape(Q_LORA, H_B, NOPE_B + ROPE_B)
    w["w_qn"] = wq[:, :, :NOPE_B].reshape(Q_LORA, H_B * LANE).astype(BF16)
    w["w_qr"] = _pad64(wq[:, :, NOPE_B:]).reshape(Q_LORA, H_B * LANE).astype(BF16)
    wkv = w_kvb[l].reshape(KV_LORA, H_B, NOPE_B + VD_B)
    w["w_kvk"] = wkv[:, :, :NOPE_B].reshape(KV_LORA, H_B * LANE).astype(BF16)
    w["w_kvv"] = wkv[:, :, NOPE_B:].reshape(KV_LORA, H_B * LANE).astype(BF16)
    w["w_router"] = jnp.pad(w_router[l], ((0, 0), (0, LANE - N_EXPERTS)))
    w["router_bias"] = router_bias[l].reshape(N_EXPERTS, 1)
    for k in ("norm1_g", "norm2_g", "gqa_qn", "gqa_kn", "mla_qa_g", "mla_kva_g", "mla_qn_nope", "mla_kn_nope"):
        w[k] = w[k].reshape(1, -1)
    w["mla_qn_rope"] = _pad64(w["mla_qn_rope"]).reshape(1, LANE)
    w["mla_kn_rope"] = jnp.pad(w["mla_kn_rope"], (0, LANE - ROPE_B)).reshape(1, LANE)
    for k in ("w_oa", "w_ob", "w_out", "w_s1", "w_s3", "w_s2"):
        w[k] = w[k].astype(BF16)
    return w


def kernel(x_prompt, x_sample, c, cache_gqa_k, cache_gqa_v, cache_mla_ckv, cache_mla_krope, c_ctx, w_mod, b_mod, norm1_g, norm2_g, w_in, gqa_qn, gqa_kn, mla_qa_g, mla_kva_g, w_qb, w_kvb, mla_qn_nope, mla_qn_rope, mla_kn_nope, mla_kn_rope, w_oa, w_ob, w_out, w_router, router_bias, w_e1, w_e3, w_e2, w_s1, w_s3, w_s2):
    depth = w_mod.shape[0]
    n_dec = x_sample.shape[0]
    rope_tabs = _rope_tables(x_sample.shape[1])
    cond8 = jnp.concatenate([c_ctx[None, :], c, jnp.zeros((8 - 1 - n_dec, D), F32)], axis=0)
    named = dict(norm1_g=norm1_g, norm2_g=norm2_g, gqa_qn=gqa_qn, gqa_kn=gqa_kn, mla_qa_g=mla_qa_g,
                 mla_kva_g=mla_kva_g, mla_qn_nope=mla_qn_nope, mla_qn_rope=mla_qn_rope, mla_kn_nope=mla_kn_nope,
                 mla_kn_rope=mla_kn_rope, w_oa=w_oa, w_ob=w_ob, w_out=w_out, w_e1=w_e1, w_e3=w_e3, w_e2=w_e2,
                 w_s1=w_s1, w_s3=w_s3, w_s2=w_s2)
    xp, xs = x_prompt, x_sample
    ks, vs, ckvs, krs = [], [], [], []
    for l in range(depth):
        w = _layer_weights(l, w_in, w_qb, w_kvb, w_router, router_bias, named)
        mod = _modulation(cond8, w_mod[l], b_mod[l].reshape(1, -1))
        mod_ctx = mod[0:1].reshape(1, N_MOD, D)
        mod_dec = mod[1:1 + n_dec].reshape(n_dec, N_MOD, D)
        xp, ctx = _trunk_pass(xp, mod_ctx, False, w, None, None)
        ks.append(ctx[0])
        vs.append(ctx[1])
        ckvs.append(ctx[2])
        krs.append(ctx[3])
        cache = (cache_gqa_k[:, l], cache_gqa_v[:, l], cache_mla_ckv[:, l], cache_mla_krope[:, l])
        xs, _ = _trunk_pass(xs, mod_dec, True, w, rope_tabs, cache)
    return (xp, xs, jnp.stack(ks, axis=1), jnp.stack(vs, axis=1), jnp.stack(ckvs, axis=1), jnp.stack(krs, axis=1))
```

```python
import functools
import math

import jax
import jax.numpy as jnp
from jax import lax
from jax.experimental import pallas as pl
from jax.experimental.pallas import tpu as pltpu

F32 = jnp.float32
BF16 = jnp.bfloat16
I32 = jnp.int32
HIGHEST = lax.Precision.HIGHEST

D = 1024
GRID_W = 64
RMS_EPS = 1e-6
ROPE_THETA = 10000.0
N_MOD = 6
H_A, KV_A, HD_A = 8, 2, 128
H_B, Q_LORA, KV_LORA, NOPE_B, ROPE_B, VD_B = 8, 256, 128, 128, 64, 128
N_EXPERTS, TOP_K, N_GROUPS, TOPK_GROUPS = 64, 8, 8, 4
GROUP_SIZE = N_EXPERTS // N_GROUPS
D_EXPERT = 256
ROUTED_SCALE = 2.5
QSCALE_A = HD_A ** -0.5 * math.log2(math.e)
QSCALE_B = (NOPE_B + ROPE_B) ** -0.5 * math.log2(math.e)

LANE = 128
C_QA, C_KA, C_VA, C_CQ, C_CKV, C_KR, C_GA, C_GB, C_END = 0, 1024, 1280, 1536, 1792, 1920, 2048, 3072, 4096
VMEM_LIMIT = 52 * 1024 * 1024

TM = 256
TQ = 256
TL = 256
GROUP_TOKENS = 4096
MB, MB_LOG2 = 128, 7
HP_ROWS = D // LANE
OUT_ROWS = D // LANE
S_IN = MB + 1
S_OUT = MB + 1
RMW_BATCH = 8
LIST_CHUNK = 16
EXPERTS_PER_STEP = 2


def _dot(a, b):
    return jnp.dot(a, b, preferred_element_type=F32)


def _rms(t, n_valid):
    ms = jnp.sum(t * t, axis=-1, keepdims=True) * (1.0 / n_valid)
    return t * lax.rsqrt(ms + RMS_EPS)


def _silu(t):
    return t * jax.nn.sigmoid(t)


def _pad64(a):
    z = jnp.zeros(a.shape[:-1] + (32,), a.dtype)
    return jnp.concatenate([a[..., :32], z, a[..., 32:], z], axis=-1)


def _unpad64(a):
    return jnp.concatenate([a[..., :32], a[..., 64:96]], axis=-1)


def _const_spec(shape):
    nd = len(shape)
    return pl.BlockSpec(shape, lambda *_: (0,) * nd)


def _mod_kernel(c_ref, w_ref, b_ref, o_ref):
    s = _silu(c_ref[...])
    o_ref[...] = jnp.dot(s, w_ref[...], preferred_element_type=F32, precision=HIGHEST) + b_ref[...]


def _modulation(cond8, w_mod, b_mod):
    tn = 1024
    n = w_mod.shape[1]
    return pl.pallas_call(
        _mod_kernel,
        grid=(n // tn,),
        in_specs=[pl.BlockSpec((8, D), lambda j: (0, 0)),
                  pl.BlockSpec((D, tn), lambda j: (0, j)),
                  pl.BlockSpec((1, tn), lambda j: (0, j))],
        out_specs=pl.BlockSpec((8, tn), lambda j: (0, j)),
        out_shape=jax.ShapeDtypeStruct((8, n), F32),
        compiler_params=pltpu.CompilerParams(dimension_semantics=("arbitrary",), vmem_limit_bytes=VMEM_LIMIT),
        name="adaln_mod",
    )(cond8, w_mod, b_mod)


def _proj_kernel(use_rope, *refs):
    (x_ref, mod_ref, g1_ref, win_ref, wgate_ref, wqn_ref, wqr_ref,
     gqn_ref, gkn_ref, gcq_ref, gckv_ref, gnope_ref, gqr_ref, gkr_ref) = refs[:14]
    n_in = 18 if use_rope else 14
    if use_rope:
        ca_ref, sa_ref, cb_ref, sb_ref = refs[14:18]
    qa_ref, ka_ref, va_ref, qb_ref, ckv_ref, kr_ref, ga_ref, gb_ref = refs[n_in:n_in + 8]
    zbuf = refs[n_in + 8:n_in + 10]
    qbuf = refs[n_in + 10:n_in + 12]
    step = pl.program_id(0)

    def rope(t, c_ref, s_ref):
        return t * c_ref[...] + pltpu.roll(t, LANE // 2, 1) * s_ref[...]

    rope_a = (lambda t: rope(t, ca_ref, sa_ref)) if use_rope else (lambda t: t)
    rope_b = (lambda t: rope(t, cb_ref, sb_ref)) if use_rope else (lambda t: t)

    @pl.when(step == 0)
    def _():
        zbuf[1][...] = jnp.zeros_like(zbuf[1])
        qbuf[1][...] = jnp.zeros_like(qbuf[1])

    def matmuls(z_ref, q_ref):
        x = x_ref[0]
        m = mod_ref[0]
        sh1, sc1 = m[0:1], m[1:2]
        hb = ((_rms(x, D) * g1_ref[...]) * (1.0 + sc1) + sh1).astype(BF16)
        for lo, hi in ((C_QA, C_KA), (C_KA, C_CQ), (C_CKV, C_GA)):
            z_ref[:, lo:hi] = _dot(hb, win_ref[:, lo:hi])
        for lo, hi in ((C_GA, C_GB), (C_GB, C_END)):
            z_ref[:, lo:hi] = _dot(hb, wgate_ref[:, lo - C_GA:hi - C_GA])
        cq = (_rms(_dot(hb, win_ref[:, C_CQ:C_CKV]), Q_LORA) * gcq_ref[...]).astype(BF16)
        q_ref[:, 0:H_B * LANE] = _dot(cq, wqn_ref[...])
        q_ref[:, H_B * LANE:2 * H_B * LANE] = _dot(cq, wqr_ref[...])

    def epilogue(z_ref, q_ref):
        for hh in range(H_A):
            t = _rms(z_ref[:, C_QA + hh * HD_A:C_QA + (hh + 1) * HD_A], HD_A) * gqn_ref[...]
            qa_ref[0, hh] = (rope_a(t) * QSCALE_A).astype(BF16)
        for j in range(KV_A):
            t = _rms(z_ref[:, C_KA + j * HD_A:C_KA + (j + 1) * HD_A], HD_A) * gkn_ref[...]
            ka_ref[0, j] = rope_a(t)
            va_ref[0, j] = z_ref[:, C_VA + j * HD_A:C_VA + (j + 1) * HD_A]
        for hh in range(H_B):
            t = _rms(q_ref[:, hh * LANE:(hh + 1) * LANE], NOPE_B) * gnope_ref[...]
            qb_ref[0, hh, :, 0:LANE] = (t * QSCALE_B).astype(BF16)
            t = _rms(q_ref[:, (H_B + hh) * LANE:(H_B + hh + 1) * LANE], ROPE_B) * gqr_ref[...]
            qb_ref[0, hh, :, LANE:2 * LANE] = (rope_b(t) * QSCALE_B).astype(BF16)
        ckv_ref[0] = _rms(z_ref[:, C_CKV:C_KR], KV_LORA) * gckv_ref[...]
        t = _rms(z_ref[:, C_KR:C_GA], ROPE_B) * gkr_ref[...]
        lane = lax.broadcasted_iota(I32, t.shape, 1)
        second = jnp.where((lane >= LANE // 2) & (lane < 3 * LANE // 4), pltpu.roll(t, LANE // 4, 1), 0.0)
        kr_ref[0] = rope_b(jnp.where(lane < LANE // 4, t, second))
        ga_ref[0] = jax.nn.sigmoid(z_ref[:, C_GA:C_GB])
        gb_ref[0] = jax.nn.sigmoid(z_ref[:, C_GB:C_END])

    for parity in range(2):
        @pl.when(step % 2 == parity)
        def _(parity=parity):
            matmuls(zbuf[parity], qbuf[parity])
            epilogue(zbuf[1 - parity], qbuf[1 - parity])


def _project(x, modsel, per_batch, w, rope_tabs):
    B, S, _ = x.shape
    use_rope = rope_tabs is not None
    nt = S // TM
    n_tiles = B * nt
    cur = lambda s: jnp.minimum(s, n_tiles - 1)
    prev = lambda s: jnp.maximum(s - 1, 0)
    mod_idx = (lambda s: (cur(s) // nt, 0, 0)) if per_batch else (lambda s: (0, 0, 0))
    in_specs = [pl.BlockSpec((1, TM, D), lambda s: (cur(s) // nt, cur(s) % nt, 0)),
                pl.BlockSpec((1, N_MOD, D), mod_idx),
                _const_spec((1, D)), _const_spec((D, C_GA)), _const_spec((D, C_END - C_GA)),
                _const_spec((Q_LORA, H_B * LANE)), _const_spec((Q_LORA, H_B * LANE)),
                _const_spec((1, LANE)), _const_spec((1, LANE)), _const_spec((1, Q_LORA)), _const_spec((1, LANE)),
                _const_spec((1, LANE)), _const_spec((1, LANE)), _const_spec((1, LANE))]
    args = [x, modsel, w["norm1_g"], w["w_in"], w["w_gate"], w["w_qn"], w["w_qr"], w["gqa_qn"], w["gqa_kn"], w["mla_qa_g"],
            w["mla_kva_g"], w["mla_qn_nope"], w["mla_qn_rope"], w["mla_kn_rope"]]
    if use_rope:
        in_specs += [pl.BlockSpec((TM, LANE), lambda s: (prev(s) % nt, 0))] * 4
        args += list(rope_tabs)
    tok4 = lambda h, d: pl.BlockSpec((1, h, TM, d), lambda s: (prev(s) // nt, 0, prev(s) % nt, 0))
    tok3 = lambda d: pl.BlockSpec((1, TM, d), lambda s: (prev(s) // nt, prev(s) % nt, 0))
    out_specs = [tok4(H_A, HD_A), tok4(KV_A, HD_A), tok4(KV_A, HD_A), tok4(H_B, 2 * LANE),
                 tok3(KV_LORA), tok3(LANE), tok3(D), tok3(D)]
    out_shape = [jax.ShapeDtypeStruct((B, H_A, S, HD_A), BF16),
                 jax.ShapeDtypeStruct((B, KV_A, S, HD_A), F32),
                 jax.ShapeDtypeStruct((B, KV_A, S, HD_A), F32),
                 jax.ShapeDtypeStruct((B, H_B, S, 2 * LANE), BF16),
                 jax.ShapeDtypeStruct((B, S, KV_LORA), F32),
                 jax.ShapeDtypeStruct((B, S, LANE), F32),
                 jax.ShapeDtypeStruct((B, S, D), F32),
                 jax.ShapeDtypeStruct((B, S, D), F32)]
    return pl.pallas_call(
        functools.partial(_proj_kernel, use_rope),
        grid=(n_tiles + 1,),
        in_specs=in_specs, out_specs=out_specs, out_shape=out_shape,
        scratch_shapes=[pltpu.VMEM((TM, C_END), F32), pltpu.VMEM((TM, C_END), F32),
                        pltpu.VMEM((TM, 2 * H_B * LANE), F32), pltpu.VMEM((TM, 2 * H_B * LANE), F32)],
        compiler_params=pltpu.CompilerParams(dimension_semantics=("arbitrary",), vmem_limit_bytes=VMEM_LIMIT),
        name="proj_rope" if use_rope else "proj",
    )(*args)


def _attend(n_heads, q_ref, key, value, o_ref):
    for hh in range(n_heads):
        v = value(hh)
        dv = v.shape[-1] // 2
        s = lax.dot_general(q_ref[0, hh], key(hh), (((1,), (1,)), ((), ())), preferred_element_type=F32)
        p = jnp.exp2(s - jnp.max(s, axis=-1, keepdims=True))
        o = _dot(p.astype(BF16), v)
        o_ref[0, :, hh * dv:(hh + 1) * dv] = (o[:, :dv] / o[:, dv:]).astype(BF16)


def _attn_kernel(n_heads, group, q_ref, k_ref, v_ref, o_ref):
    _attend(n_heads, q_ref, lambda hh: k_ref[0, hh // group], lambda hh: v_ref[0, hh // group], o_ref)


def _attn_mla_kernel(q_ref, ckv_ref, kr_ref, wk_ref, wv_ref, gk_ref, o_ref, kb_ref, vb_ref):
    @pl.when(pl.program_id(1) == 0)
    def _():
        def chunk(c, carry):
            rows = pl.ds(pl.multiple_of(c * TL, TL), TL)
            cb = ckv_ref[0, rows, :].astype(BF16)
            k = _dot(cb, wk_ref[...])
            v = _dot(cb, wv_ref[...])
            krb = kr_ref[0, rows, :].astype(BF16)
            for hh in range(H_B):
                t = _rms(k[:, hh * LANE:(hh + 1) * LANE], NOPE_B) * gk_ref[...]
                kb_ref[hh, rows, 0:LANE] = t.astype(BF16)
                kb_ref[hh, rows, LANE:2 * LANE] = krb
                vb_ref[hh, rows, 0:VD_B] = v[:, hh * LANE:(hh + 1) * LANE].astype(BF16)
                vb_ref[hh, rows, VD_B:2 * VD_B] = jnp.ones((TL, VD_B), BF16)
            return carry

        lax.fori_loop(0, ckv_ref.shape[1] // TL, chunk, 0)

    _attend(H_B, q_ref, lambda hh: kb_ref[hh], lambda hh: vb_ref[hh], o_ref)


def _attention_mla(q, ckv_all, krp_all, w):
    B, H, S, dk = q.shape
    L = ckv_all.shape[1]
    return pl.pallas_call(
        _attn_mla_kernel,
        grid=(B, S // TQ),
        in_specs=[pl.BlockSpec((1, H, TQ, dk), lambda b, i: (b, 0, i, 0)),
                  pl.BlockSpec((1, L, KV_LORA), lambda b, i: (b, 0, 0)),
                  pl.BlockSpec((1, L, LANE), lambda b, i: (b, 0, 0)),
                  _const_spec((KV_LORA, H_B * LANE)), _const_spec((KV_LORA, H_B * LANE)), _const_spec((1, LANE))],
        out_specs=pl.BlockSpec((1, TQ, H * VD_B), lambda b, i: (b, i, 0)),
        out_shape=jax.ShapeDtypeStruct((B, S, H * VD_B), BF16),
        scratch_shapes=[pltpu.VMEM((H_B, L, 2 * LANE), BF16), pltpu.VMEM((H_B, L, 2 * VD_B), BF16)],
        compiler_params=pltpu.CompilerParams(dimension_semantics=("arbitrary", "arbitrary"),
                                             vmem_limit_bytes=VMEM_LIMIT),
        name="attn_mla",
    )(q, ckv_all, krp_all, w["w_kvk"], w["w_kvv"], w["mla_kn_nope"])


def _attention(q, k, v, name):
    B, H, S, dk = q.shape
    _, Hk, L, dv2 = v.shape
    dv = dv2 // 2
    return pl.pallas_call(
        functools.partial(_attn_kernel, H, H // Hk),
        grid=(B, S // TQ),
        in_specs=[pl.BlockSpec((1, H, TQ, dk), lambda b, i: (b, 0, i, 0)),
                  pl.BlockSpec((1, Hk, L, dk), lambda b, i: (b, 0, 0, 0)),
                  pl.BlockSpec((1, Hk, L, dv2), lambda b, i: (b, 0, 0, 0))],
        out_specs=pl.BlockSpec((1, TQ, H * dv), lambda b, i: (b, i, 0)),
        out_shape=jax.ShapeDtypeStruct((B, S, H * dv), BF16),
        compiler_params=pltpu.CompilerParams(dimension_semantics=("arbitrary", "arbitrary"),
                                             vmem_limit_bytes=VMEM_LIMIT),
        name=name,
    )(q, k, v)


def _route(bi, sc, carry):
    n_tok = bi.shape[1]
    iota8 = lax.broadcasted_iota(I32, (GROUP_SIZE, n_tok), 0)
    neg = -jnp.inf
    grp = [bi[j * GROUP_SIZE:(j + 1) * GROUP_SIZE] for j in range(N_GROUPS)]
    gscore = []
    for g in grp:
        m1 = jnp.max(g, axis=0, keepdims=True)
        first = jnp.min(jnp.where(g == m1, iota8, GROUP_SIZE), axis=0, keepdims=True)
        m2 = jnp.max(jnp.where(iota8 == first, neg, g), axis=0, keepdims=True)
        gscore.append(m1 + m2)
    masked = []
    for j in range(N_GROUPS):
        cnt = jnp.zeros((1, n_tok), I32)
        for j2 in range(N_GROUPS):
            if j2 == j:
                continue
            beats = (gscore[j2] > gscore[j]) if j2 > j else (gscore[j2] >= gscore[j])
            cnt = cnt + beats.astype(I32)
        keep = jnp.broadcast_to(cnt, grp[j].shape) < TOPK_GROUPS
        masked.append(jnp.where(keep, grp[j], neg))
    eid = [(iota8 + j * GROUP_SIZE).astype(F32) for j in range(N_GROUPS)]
    scg = [sc[j * GROUP_SIZE:(j + 1) * GROUP_SIZE] for j in range(N_GROUPS)]
    work = list(masked)
    kept = [jnp.zeros((GROUP_SIZE, n_tok), F32) for _ in range(N_GROUPS)]
    idx_rows = jnp.zeros((TOP_K, n_tok), F32)
    score_rows = jnp.zeros((TOP_K, n_tok), F32)
    for k in range(TOP_K):
        top = work[0]
        for j in range(1, N_GROUPS):
            top = jnp.maximum(top, work[j])
        top = jnp.max(top, axis=0, keepdims=True)
        cand = jnp.where(work[0] == top, eid[0], float(N_EXPERTS))
        for j in range(1, N_GROUPS):
            cand = jnp.minimum(cand, jnp.where(work[j] == top, eid[j], float(N_EXPERTS)))
        chosen = jnp.min(cand, axis=0, keepdims=True)
        score = jnp.zeros((GROUP_SIZE, n_tok), F32)
        for j in range(N_GROUPS):
            hit = eid[j] == chosen
            score = score + jnp.where(hit, scg[j], 0.0)
            kept[j] = kept[j] + jnp.where(hit, 1.0, 0.0)
            work[j] = jnp.where(hit, neg, work[j])
        idx_rows = jnp.where(iota8 == k, chosen, idx_rows)
        score_rows = jnp.where(iota8 == k, jnp.sum(score, axis=0, keepdims=True), score_rows)
    w_rows = score_rows / jnp.sum(score_rows, axis=0, keepdims=True) * ROUTED_SCALE
    kept = jnp.concatenate(kept, axis=0)
    earlier = (lax.broadcasted_iota(I32, (n_tok, n_tok), 0) < lax.broadcasted_iota(I32, (n_tok, n_tok), 1))
    rank = _dot(kept.astype(BF16), earlier.astype(F32).astype(BF16)) + carry
    new_carry = carry + jnp.sum(kept, axis=1, keepdims=True)
    rank_rows = jnp.zeros((TOP_K, n_tok), F32)
    for k in range(TOP_K):
        ra = jnp.zeros((GROUP_SIZE, n_tok), F32)
        for j in range(N_GROUPS):
            ra = ra + jnp.where(eid[j] == idx_rows[k:k + 1], rank[j * GROUP_SIZE:(j + 1) * GROUP_SIZE], 0.0)
        rank_rows = jnp.where(iota8 == k, jnp.sum(ra, axis=0, keepdims=True), rank_rows)
    return idx_rows, w_rows, rank_rows, new_carry


def _post_kernel(tiles_per_group, oa_ref, ob_ref, ga_ref, gb_ref, x_ref, mod_ref, g2n_ref, woa_ref, wob_ref,
                 wout_ref, wr_ref, rb_ref, ws1_ref, ws3_ref, ws2_ref, base_ref, hp_ref, idx_ref, wk_ref, rank_ref,
                 carry_ref):
    tile = pl.program_id(0) * pl.num_programs(1) + pl.program_id(1)

    @pl.when(tile % tiles_per_group == 0)
    def _():
        carry_ref[...] = jnp.zeros_like(carry_ref)

    m = mod_ref[0]
    g1, sh2, sc2, g2 = m[2:3], m[3:4], m[4:5], m[5:6]
    merged = ga_ref[0] * _dot(oa_ref[0], woa_ref[...]) + gb_ref[0] * _dot(ob_ref[0], wob_ref[...])
    x1 = x_ref[0] + g1 * _dot(merged.astype(BF16), wout_ref[...])
    h2 = (_rms(x1, D) * g2n_ref[...]) * (1.0 + sc2) + sh2
    h2b = h2.astype(BF16)
    act = _silu(_dot(h2b, ws1_ref[...])) * _dot(h2b, ws3_ref[...])
    base_ref[0] = x1 + g2 * _dot(act.astype(BF16), ws2_ref[...])
    n_tok = h2.shape[0]
    for j in range(HP_ROWS):
        hp_ref[pl.ds(j, n_tok, stride=HP_ROWS), :] = h2[:, j * LANE:(j + 1) * LANE]
    logits = jnp.dot(h2, wr_ref[...], preferred_element_type=F32, precision=HIGHEST)
    sc = jax.nn.sigmoid(logits.T[0:N_EXPERTS])
    idx_rows, w_rows, rank_rows, carry = _route(sc + rb_ref[...], sc, carry_ref[...])
    carry_ref[...] = carry
    idx_ref[...] = idx_rows.astype(I32)
    wk_ref[...] = w_rows
    rank_ref[...] = rank_rows.astype(I32)


def _post(oa, ob, ga, gb, x, modsel, per_batch, w, gt):
    B, S, _ = x.shape
    nt = S // TM
    mod_idx = (lambda b, i: (b, 0, 0)) if per_batch else (lambda b, i: (0, 0, 0))
    tok = lambda d: pl.BlockSpec((1, TM, d), lambda b, i: (b, i, 0))
    slot = lambda: pl.BlockSpec((TOP_K, TM), lambda b, i: (0, b * nt + i))
    return pl.pallas_call(
        functools.partial(_post_kernel, gt // TM),
        grid=(B, nt),
        in_specs=[tok(D), tok(D), tok(D), tok(D), tok(D), pl.BlockSpec((1, N_MOD, D), mod_idx),
                  _const_spec((1, D)), _const_spec((D, D)), _const_spec((D, D)), _const_spec((D, D)),
                  _const_spec((D, LANE)), _const_spec((N_EXPERTS, 1)),
                  _const_spec((D, D_EXPERT)), _const_spec((D, D_EXPERT)), _const_spec((D_EXPERT, D))],
        out_specs=[tok(D),
                   pl.BlockSpec((TM * HP_ROWS, LANE), lambda b, i: (b * nt + i, 0)),
                   slot(), slot(), slot()],
        out_shape=[jax.ShapeDtypeStruct((B, S, D), F32),
                   jax.ShapeDtypeStruct((B * S * HP_ROWS, LANE), F32),
                   jax.ShapeDtypeStruct((TOP_K, B * S), I32),
                   jax.ShapeDtypeStruct((TOP_K, B * S), F32),
                   jax.ShapeDtypeStruct((TOP_K, B * S), I32)],
        scratch_shapes=[pltpu.VMEM((N_EXPERTS, 1), F32)],
        compiler_params=pltpu.CompilerParams(dimension_semantics=("arbitrary", "arbitrary"),
                                             vmem_limit_bytes=VMEM_LIMIT),
        name="post",
    )(oa, ob, ga, gb, x, modsel, w["norm2_g"], w["w_oa"], w["w_ob"], w["w_out"], w["w_router"],
      w["router_bias"], w["w_s1"], w["w_s3"], w["w_s2"])


def _dispatch_rows(idx, wk, rank, n_groups):
    T = idx.shape[1]
    gt = T // n_groups
    experts = jnp.arange(N_EXPERTS, dtype=I32)
    regroup = lambda a: a.reshape(TOP_K, n_groups, gt).transpose(1, 0, 2)
    eid = regroup(idx)
    onehot = (eid[:, None] == experts[None, :, None, None]).astype(I32)
    counts = jnp.sum(onehot, axis=(2, 3))
    padded = (counts + (MB - 1)) // MB * MB
    bounds = jnp.concatenate([jnp.zeros((n_groups, 1), I32), jnp.cumsum(padded, axis=1)], axis=1)
    row = jnp.sum(onehot * bounds[:, :N_EXPERTS, None, None], axis=1) + regroup(rank)
    wtab = jnp.pad(regroup(wk), ((0, 0), (0, 0), (0, gt)))
    return bounds, counts, row.reshape(n_groups, TOP_K * gt), wtab.reshape(n_groups, TOP_K * 2 * gt)


def _moe_kernel(gt, seg_ref, cnt_ref, row_ref, wtab_ref, hp_ref, w1_ref, w3_ref, w2_ref, o_ref,
                tin0_ref, tin1_ref, tout0_ref, tout1_ref, w1b_ref, w3b_ref, w2b_ref, code_ref):
    tin = (tin0_ref, tin1_ref)
    tout = (tout0_ref, tout1_ref)

    def build_row_list():
        def spare_tail(ex, carry):
            def one(p, c):
                code_ref[p] = gt
                return c
            return lax.fori_loop(seg_ref[ex] + cnt_ref[ex], seg_ref[ex + 1], one, carry)

        lax.fori_loop(0, N_EXPERTS, spare_tail, 0)

        def chunk(i, carry):
            t0 = i * LIST_CHUNK
            for u in range(LIST_CHUNK):
                for k in range(TOP_K):
                    code_ref[row_ref[k * gt + t0 + u]] = k * 2 * gt + t0 + u
            return carry

        lax.fori_loop(0, gt // LIST_CHUNK, chunk, 0)

    e = pl.program_id(0)
    n_total = lax.shift_right_logical(seg_ref[N_EXPERTS], MB_LOG2)

    def gather(b, slot):
        r0 = b * MB
        for m in range(MB):
            t = code_ref[r0 + m] & (gt - 1)
            slab = hp_ref[pl.ds(pl.multiple_of(t * HP_ROWS, HP_ROWS), HP_ROWS), :]
            tin[slot][pl.ds(m, HP_ROWS, stride=S_IN), :] = slab

    def scatter(b, slot, live):
        r0 = b * MB
        for b0 in range(0, MB, RMW_BATCH):
            pending = []
            for m in range(b0, b0 + RMW_BATCH):
                code = code_ref[r0 + m]
                off = pl.multiple_of((code & (2 * gt - 1)) * OUT_ROWS, OUT_ROWS)
                upd = tout[slot][pl.ds(m, OUT_ROWS, stride=S_OUT), :] * wtab_ref[code]
                if live is not None:
                    upd = jnp.where(live, upd, 0.0)
                pending.append((off, o_ref[pl.ds(off, OUT_ROWS), :] + upd))
            for off, v in pending:
                o_ref[pl.ds(off, OUT_ROWS), :] = v

    @pl.when(e == 0)
    def _():
        build_row_list()
        o_ref[...] = jnp.zeros_like(o_ref)
        tout1_ref[...] = jnp.zeros_like(tout1_ref)
        gather(0, 0)

    def stages(b, slot):
        gather(jnp.minimum(b + 1, n_total - 1), 1 - slot)
        x = jnp.concatenate([tin[slot][pl.ds(j * S_IN, MB), :] for j in range(HP_ROWS)], axis=-1).astype(BF16)
        act = _silu(_dot(x, w1b_ref[...])) * _dot(x, w3b_ref[...])
        y = _dot(act.astype(BF16), w2b_ref[...])
        for j in range(OUT_ROWS):
            tout[slot][pl.ds(j * S_OUT, MB), :] = y[:, j * LANE:(j + 1) * LANE]
        scatter(jnp.maximum(b - 1, 0), 1 - slot, b > 0)

    def expert(sub, carry):
        ex = e * EXPERTS_PER_STEP + sub
        w1b_ref[...] = w1_ref[sub].astype(BF16)
        w3b_ref[...] = w3_ref[sub].astype(BF16)
        w2b_ref[...] = w2_ref[sub].astype(BF16)
        first = lax.shift_right_logical(seg_ref[ex], MB_LOG2)

        def block(i, c):
            b = first + i
            for slot in range(2):
                @pl.when(b & 1 == slot)
                def _(slot=slot):
                    stages(b, slot)
            return c

        return lax.fori_loop(0, lax.shift_right_logical(seg_ref[ex + 1], MB_LOG2) - first, block, carry)

    lax.fori_loop(0, EXPERTS_PER_STEP, expert, 0)

    @pl.when(e == N_EXPERTS // EXPERTS_PER_STEP - 1)
    def _():
        for slot in range(2):
            @pl.when((n_total - 1) & 1 == slot)
            def _(slot=slot):
                scatter(n_total - 1, slot, None)


def _moe(hp, group, gt, seg, counts, rows, wtab, w):
    n_rows = TOP_K * gt + N_EXPERTS * MB
    grid_spec = pltpu.PrefetchScalarGridSpec(
        num_scalar_prefetch=4,
        grid=(N_EXPERTS // EXPERTS_PER_STEP,),
        in_specs=[pl.BlockSpec((gt * HP_ROWS, LANE), lambda e, *_: (group, 0), pipeline_mode=pl.Buffered(1)),
                  pl.BlockSpec((EXPERTS_PER_STEP, D, D_EXPERT), lambda e, *_: (e, 0, 0)),
                  pl.BlockSpec((EXPERTS_PER_STEP, D, D_EXPERT), lambda e, *_: (e, 0, 0)),
                  pl.BlockSpec((EXPERTS_PER_STEP, D_EXPERT, D), lambda e, *_: (e, 0, 0))],
        out_specs=pl.BlockSpec(((gt + 1) * OUT_ROWS, LANE), lambda e, *_: (0, 0), pipeline_mode=pl.Buffered(1)),
        scratch_shapes=[pltpu.VMEM((HP_ROWS * S_IN, LANE), F32), pltpu.VMEM((HP_ROWS * S_IN, LANE), F32),
                        pltpu.VMEM((OUT_ROWS * S_OUT, LANE), F32), pltpu.VMEM((OUT_ROWS * S_OUT, LANE), F32),
                        pltpu.VMEM((D, D_EXPERT), BF16), pltpu.VMEM((D, D_EXPERT), BF16),
                        pltpu.VMEM((D_EXPERT, D), BF16),
                        pltpu.SMEM((n_rows,), I32)])
    return pl.pallas_call(
        functools.partial(_moe_kernel, gt),
        grid_spec=grid_spec,
        out_shape=jax.ShapeDtypeStruct(((gt + 1) * OUT_ROWS, LANE), F32),
        compiler_params=pltpu.CompilerParams(dimension_semantics=("arbitrary",), vmem_limit_bytes=VMEM_LIMIT),
        name="moe_experts",
    )(seg, counts, rows, wtab, hp, w["w_e1"], w["w_e3"], w["w_e2"])


def _combine_kernel(n_groups, tiles_per_group, *refs):
    r_refs = refs[:n_groups]
    base_ref, mod_ref, o_ref = refs[n_groups:]
    tile = pl.program_id(0) * pl.num_programs(1) + pl.program_id(1)
    g2 = mod_ref[0][5:6]
    n_tok = base_ref.shape[1]
    for g, r_ref in enumerate(r_refs):
        @pl.when(tile // tiles_per_group == g)
        def _(r_ref=r_ref):
            for s in range(OUT_ROWS):
                cols = slice(s * LANE, (s + 1) * LANE)
                o_ref[0, :, cols] = base_ref[0, :, cols] + g2[:, cols] * r_ref[pl.ds(s, n_tok, stride=OUT_ROWS), :]


def _combine(routed, base, modsel, per_batch, gt):
    B, S, _ = base.shape
    nt = S // TM
    tiles_per_group = gt // TM
    mod_idx = (lambda b, i: (b, 0, 0)) if per_batch else (lambda b, i: (0, 0, 0))

    def r_spec(g):
        def idx(b, i):
            tile = b * nt + i
            return (jnp.where(tile // tiles_per_group == g, tile % tiles_per_group, 0), 0)
        return pl.BlockSpec((TM * OUT_ROWS, LANE), idx)

    return pl.pallas_call(
        functools.partial(_combine_kernel, len(routed), tiles_per_group),
        grid=(B, nt),
        in_specs=[r_spec(g) for g in range(len(routed))] + [
            pl.BlockSpec((1, TM, D), lambda b, i: (b, i, 0)),
            pl.BlockSpec((1, N_MOD, D), mod_idx)],
        out_specs=pl.BlockSpec((1, TM, D), lambda b, i: (b, i, 0)),
        out_shape=jax.ShapeDtypeStruct((B, S, D), F32),
        compiler_params=pltpu.CompilerParams(dimension_semantics=("arbitrary", "arbitrary"),
                                             vmem_limit_bytes=VMEM_LIMIT),
        name="moe_combine",
    )(*routed, base, modsel)


def _trunk_pass(x, modsel, per_batch, w, rope_tabs, cache):
    B, S, _ = x.shape
    qa, ka, va, qb, ckv, krp, ga, gb = _project(x, modsel, per_batch, w, rope_tabs)
    if cache is None:
        ka_all, va_all, ckv_all, krp_all = ka, va, ckv, krp
    else:
        ck, cv, cckv, ckr = cache
        ka_all = jnp.concatenate([ck, ka], axis=2)
        va_all = jnp.concatenate([cv, va], axis=2)
        ckv_all = jnp.concatenate([cckv, ckv], axis=1)
        krp_all = jnp.concatenate([_pad64(ckr), krp], axis=1)
    va_ones = jnp.concatenate([va_all.astype(BF16), jnp.ones(va_all.shape, BF16)], axis=-1)
    oa = _attention(qa, ka_all.astype(BF16), va_ones, "attn_gqa")
    ob = _attention_mla(qb, ckv_all, krp_all, w)
    T = B * S
    gt = min(GROUP_TOKENS, T)
    assert T % gt == 0 and gt % TM == 0 and gt & (gt - 1) == 0 and gt >= MB and gt % LIST_CHUNK == 0
    base, hp, idx, wk, rank = _post(oa, ob, ga, gb, x, modsel, per_batch, w, gt)
    seg, counts, rows, wtab = _dispatch_rows(idx, wk, rank, T // gt)
    routed = [_moe(hp, g, gt, seg[g], counts[g], rows[g], wtab[g], w) for g in range(T // gt)]
    y = _combine(routed, base, modsel, per_batch, gt)
    return y, (ka, va, ckv, _unpad64(krp))


def _rope_tables(n_tokens):
    t = jnp.arange(n_tokens)
    row = (t // GRID_W).astype(F32)[:, None]
    col = (t % GRID_W).astype(F32)[:, None]

    def tabs(rot_dim):
        axis_dim = rot_dim // 2
        inv = ROPE_THETA ** (-jnp.arange(0, axis_dim, 2, dtype=F32) / axis_dim)
        ang = jnp.concatenate([row * inv, col * inv], axis=-1)
        return jnp.cos(ang), jnp.sin(ang)

    cos_a, sin_a = tabs(HD_A)
    cos_b, sin_b = tabs(ROPE_B)
    return (jnp.concatenate([cos_a, cos_a], -1), jnp.concatenate([-sin_a, sin_a], -1),
            _pad64(jnp.concatenate([cos_b, cos_b], -1)), _pad64(jnp.concatenate([-sin_b, sin_b], -1)))


def _layer_weights(l, w_in, w_qb, w_kvb, w_router, router_bias, named):
    w = {k: v[l] for k, v in named.items()}
    wi = w_in[l]
    n_front = C_KR + ROPE_B
    w["w_in"] = jnp.pad(wi[:, :n_front].astype(BF16), ((0, 0), (0, C_GA - n_front)))
    w["w_gate"] = wi[:, n_front:].astype(BF16)
    wq = w_qb[l].reshape(Q_LORA, H_B, NOPE_B + ROPE_B)
    w["w_qn"] = wq[:, :, :NOPE_B].reshape(Q_LORA, H_B * LANE).astype(BF16)
    w["w_qr"] = _pad64(wq[:, :, NOPE_B:]).reshape(Q_LORA, H_B * LANE).astype(BF16)
    wkv = w_kvb[l].reshape(KV_LORA, H_B, NOPE_B + VD_B)
    w["w_kvk"] = wkv[:, :, :NOPE_B].reshape(KV_LORA, H_B * LANE).astype(BF16)
    w["w_kvv"] = wkv[:, :, NOPE_B:].reshape(KV_LORA, H_B * LANE).astype(BF16)
    w["w_router"] = jnp.pad(w_router[l], ((0, 0), (0, LANE - N_EXPERTS)))
    w["router_bias"] = router_bias[l].reshape(N_EXPERTS, 1)
    for k in ("norm1_g", "norm2_g", "gqa_qn", "gqa_kn", "mla_qa_g", "mla_kva_g", "mla_qn_nope", "mla_kn_nope"):
        w[k] = w[k].reshape(1, -1)
    w["mla_qn_rope"] = _pad64(w["mla_qn_rope"]).reshape(1, LANE)
    w["mla_kn_rope"] = jnp.pad(w["mla_kn_rope"], (0, LANE - ROPE_B)).reshape(1, LANE)
    for k in ("w_oa", "w_ob", "w_out", "w_s1", "w_s3", "w_s2"):
        w[k] = w[k].astype(BF16)
    return w


def kernel(x_prompt, x_sample, c, cache_gqa_k, cache_gqa_v, cache_mla_ckv, cache_mla_krope, c_ctx, w_mod, b_mod, norm1_g, norm2_g, w_in, gqa_qn, gqa_kn, mla_qa_g, mla_kva_g, w_qb, w_kvb, mla_qn_nope, mla_qn_rope, mla_kn_nope, mla_kn_rope, w_oa, w_ob, w_out, w_router, router_bias, w_e1, w_e3, w_e2, w_s1, w_s3, w_s2):
    depth = w_mod.shape[0]
    n_dec = x_sample.shape[0]
    rope_tabs = _rope_tables(x_sample.shape[1])
    cond8 = jnp.concatenate([c_ctx[None, :], c, jnp.zeros((8 - 1 - n_dec, D), F32)], axis=0)
    named = dict(norm1_g=norm1_g, norm2_g=norm2_g, gqa_qn=gqa_qn, gqa_kn=gqa_kn, mla_qa_g=mla_qa_g,
                 mla_kva_g=mla_kva_g, mla_qn_nope=mla_qn_nope, mla_qn_rope=mla_qn_rope, mla_kn_nope=mla_kn_nope,
                 mla_kn_rope=mla_kn_rope, w_oa=w_oa, w_ob=w_ob, w_out=w_out, w_e1=w_e1, w_e3=w_e3, w_e2=w_e2,
                 w_s1=w_s1, w_s3=w_s3, w_s2=w_s2)
    xp, xs = x_prompt, x_sample
    ks, vs, ckvs, krs = [], [], [], []
    for l in range(depth):
        w = _layer_weights(l, w_in, w_qb, w_kvb, w_router, router_bias, named)
        mod = _modulation(cond8, w_mod[l], b_mod[l].reshape(1, -1))
        mod_ctx = mod[0:1].reshape(1, N_MOD, D)
        mod_dec = mod[1:1 + n_dec].reshape(n_dec, N_MOD, D)
        xp, ctx = _trunk_pass(xp, mod_ctx, False, w, None, None)
        ks.append(ctx[0])
        vs.append(ctx[1])
        ckvs.append(ctx[2])
        krs.append(ctx[3])
        cache = (cache_gqa_k[:, l], cache_gqa_v[:, l], cache_mla_ckv[:, l], cache_mla_krope[:, l])
        xs, _ = _trunk_pass(xs, mod_dec, True, w, rope_tabs, cache)
    return (xp, xs, jnp.stack(ks, axis=1), jnp.stack(vs, axis=1), jnp.stack(ckvs, axis=1), jnp.stack(krs, axis=1))
```

```python
import functools
import math

import jax
import jax.numpy as jnp
from jax import lax
from jax.experimental import pallas as pl
from jax.experimental.pallas import tpu as pltpu

F32 = jnp.float32
BF16 = jnp.bfloat16
I32 = jnp.int32
HIGHEST = lax.Precision.HIGHEST

D = 1024
GRID_W = 64
RMS_EPS = 1e-6
ROPE_THETA = 10000.0
N_MOD = 6
H_A, KV_A, HD_A = 8, 2, 128
H_B, Q_LORA, KV_LORA, NOPE_B, ROPE_B, VD_B = 8, 256, 128, 128, 64, 128
N_EXPERTS, TOP_K, N_GROUPS, TOPK_GROUPS = 64, 8, 8, 4
GROUP_SIZE = N_EXPERTS // N_GROUPS
D_EXPERT = 256
ROUTED_SCALE = 2.5
QSCALE_A = HD_A ** -0.5 * math.log2(math.e)
QSCALE_B = (NOPE_B + ROPE_B) ** -0.5 * math.log2(math.e)

LANE = 128
C_QA, C_KA, C_VA, C_CQ, C_CKV, C_KR, C_GA, C_GB, C_END = 0, 1024, 1280, 1536, 1792, 1920, 2048, 3072, 4096
VMEM_LIMIT = 52 * 1024 * 1024

TM = 256
TQ = 256
TL = 256
GROUP_TOKENS = 4096
MB, MB_LOG2 = 128, 7
HP_ROWS = D // LANE
OUT_ROWS = D // LANE
S_IN = MB + 4
S_OUT = MB + 4
RMW_BATCH = 8
LIST_CHUNK = 16
EXPERTS_PER_STEP = 2


def _dot(a, b):
    return jnp.dot(a, b, preferred_element_type=F32)


def _rms(t, n_valid):
    ms = jnp.sum(t * t, axis=-1, keepdims=True) * (1.0 / n_valid)
    return t * lax.rsqrt(ms + RMS_EPS)


def _silu(t):
    return t * jax.nn.sigmoid(t)


def _pad64(a):
    z = jnp.zeros(a.shape[:-1] + (32,), a.dtype)
    return jnp.concatenate([a[..., :32], z, a[..., 32:], z], axis=-1)


def _unpad64(a):
    return jnp.concatenate([a[..., :32], a[..., 64:96]], axis=-1)


def _const_spec(shape):
    nd = len(shape)
    return pl.BlockSpec(shape, lambda *_: (0,) * nd)


def _mod_kernel(c_ref, w_ref, b_ref, o_ref):
    s = _silu(c_ref[...])
    o_ref[...] = jnp.dot(s, w_ref[...], preferred_element_type=F32, precision=HIGHEST) + b_ref[...]


def _modulation(cond8, w_mod, b_mod):
    tn = 1024
    n = w_mod.shape[1]
    return pl.pallas_call(
        _mod_kernel,
        grid=(n // tn,),
        in_specs=[pl.BlockSpec((8, D), lambda j: (0, 0)),
                  pl.BlockSpec((D, tn), lambda j: (0, j)),
                  pl.BlockSpec((1, tn), lambda j: (0, j))],
        out_specs=pl.BlockSpec((8, tn), lambda j: (0, j)),
        out_shape=jax.ShapeDtypeStruct((8, n), F32),
        compiler_params=pltpu.CompilerParams(dimension_semantics=("arbitrary",), vmem_limit_bytes=VMEM_LIMIT),
        name="adaln_mod",
    )(cond8, w_mod, b_mod)


def _proj_kernel(use_rope, *refs):
    (x_ref, mod_ref, g1_ref, win_ref, wgate_ref, wqn_ref, wqr_ref,
     gqn_ref, gkn_ref, gcq_ref, gckv_ref, gnope_ref, gqr_ref, gkr_ref) = refs[:14]
    n_in = 18 if use_rope else 14
    if use_rope:
        ca_ref, sa_ref, cb_ref, sb_ref = refs[14:18]
    qa_ref, ka_ref, va_ref, qb_ref, ckv_ref, kr_ref, ga_ref, gb_ref = refs[n_in:n_in + 8]
    zbuf = refs[n_in + 8:n_in + 10]
    qbuf = refs[n_in + 10:n_in + 12]
    step = pl.program_id(0)

    def rope(t, c_ref, s_ref):
        return t * c_ref[...] + pltpu.roll(t, LANE // 2, 1) * s_ref[...]

    rope_a = (lambda t: rope(t, ca_ref, sa_ref)) if use_rope else (lambda t: t)
    rope_b = (lambda t: rope(t, cb_ref, sb_ref)) if use_rope else (lambda t: t)

    @pl.when(step == 0)
    def _():
        zbuf[1][...] = jnp.zeros_like(zbuf[1])
        qbuf[1][...] = jnp.zeros_like(qbuf[1])

    def matmuls(z_ref, q_ref):
        x = x_ref[0]
        m = mod_ref[0]
        sh1, sc1 = m[0:1], m[1:2]
        hb = ((_rms(x, D) * g1_ref[...]) * (1.0 + sc1) + sh1).astype(BF16)
        for lo, hi in ((C_QA, C_KA), (C_KA, C_CQ), (C_CKV, C_GA)):
            z_ref[:, lo:hi] = _dot(hb, win_ref[:, lo:hi])
        for lo, hi in ((C_GA, C_GB), (C_GB, C_END)):
            z_ref[:, lo:hi] = _dot(hb, wgate_ref[:, lo - C_GA:hi - C_GA])
        cq = (_rms(_dot(hb, win_ref[:, C_CQ:C_CKV]), Q_LORA) * gcq_ref[...]).astype(BF16)
        q_ref[:, 0:H_B * LANE] = _dot(cq, wqn_ref[...])
        q_ref[:, H_B * LANE:2 * H_B * LANE] = _dot(cq, wqr_ref[...])

    def epilogue(z_ref, q_ref):
        for hh in range(H_A):
            t = _rms(z_ref[:, C_QA + hh * HD_A:C_QA + (hh + 1) * HD_A], HD_A) * gqn_ref[...]
            qa_ref[0, hh] = (rope_a(t) * QSCALE_A).astype(BF16)
        for j in range(KV_A):
            t = _rms(z_ref[:, C_KA + j * HD_A:C_KA + (j + 1) * HD_A], HD_A) * gkn_ref[...]
            ka_ref[0, j] = rope_a(t)
            va_ref[0, j] = z_ref[:, C_VA + j * HD_A:C_VA + (j + 1) * HD_A]
        for hh in range(H_B):
            t = _rms(q_ref[:, hh * LANE:(hh + 1) * LANE], NOPE_B) * gnope_ref[...]
            qb_ref[0, hh, :, 0:LANE] = (t * QSCALE_B).astype(BF16)
            t = _rms(q_ref[:, (H_B + hh) * LANE:(H_B + hh + 1) * LANE], ROPE_B) * gqr_ref[...]
            qb_ref[0, hh, :, LANE:2 * LANE] = (rope_b(t) * QSCALE_B).astype(BF16)
        ckv_ref[0] = _rms(z_ref[:, C_CKV:C_KR], KV_LORA) * gckv_ref[...]
        t = _rms(z_ref[:, C_KR:C_GA], ROPE_B) * gkr_ref[...]
        lane = lax.broadcasted_iota(I32, t.shape, 1)
        second = jnp.where((lane >= LANE // 2) & (lane < 3 * LANE // 4), pltpu.roll(t, LANE // 4, 1), 0.0)
        kr_ref[0] = rope_b(jnp.where(lane < LANE // 4, t, second))
        ga_ref[0] = jax.nn.sigmoid(z_ref[:, C_GA:C_GB])
        gb_ref[0] = jax.nn.sigmoid(z_ref[:, C_GB:C_END])

    for parity in range(2):
        @pl.when(step % 2 == parity)
        def _(parity=parity):
            matmuls(zbuf[parity], qbuf[parity])
            epilogue(zbuf[1 - parity], qbuf[1 - parity])


def _project(x, modsel, per_batch, w, rope_tabs):
    B, S, _ = x.shape
    use_rope = rope_tabs is not None
    nt = S // TM
    n_tiles = B * nt
    cur = lambda s: jnp.minimum(s, n_tiles - 1)
    prev = lambda s: jnp.maximum(s - 1, 0)
    mod_idx = (lambda s: (cur(s) // nt, 0, 0)) if per_batch else (lambda s: (0, 0, 0))
    in_specs = [pl.BlockSpec((1, TM, D), lambda s: (cur(s) // nt, cur(s) % nt, 0)),
                pl.BlockSpec((1, N_MOD, D), mod_idx),
                _const_spec((1, D)), _const_spec((D, C_GA)), _const_spec((D, C_END - C_GA)),
                _const_spec((Q_LORA, H_B * LANE)), _const_spec((Q_LORA, H_B * LANE)),
                _const_spec((1, LANE)), _const_spec((1, LANE)), _const_spec((1, Q_LORA)), _const_spec((1, LANE)),
                _const_spec((1, LANE)), _const_spec((1, LANE)), _const_spec((1, LANE))]
    args = [x, modsel, w["norm1_g"], w["w_in"], w["w_gate"], w["w_qn"], w["w_qr"], w["gqa_qn"], w["gqa_kn"], w["mla_qa_g"],
            w["mla_kva_g"], w["mla_qn_nope"], w["mla_qn_rope"], w["mla_kn_rope"]]
    if use_rope:
        in_specs += [pl.BlockSpec((TM, LANE), lambda s: (prev(s) % nt, 0))] * 4
        args += list(rope_tabs)
    tok4 = lambda h, d: pl.BlockSpec((1, h, TM, d), lambda s: (prev(s) // nt, 0, prev(s) % nt, 0))
    tok3 = lambda d: pl.BlockSpec((1, TM, d), lambda s: (prev(s) // nt, prev(s) % nt, 0))
    out_specs = [tok4(H_A, HD_A), tok4(KV_A, HD_A), tok4(KV_A, HD_A), tok4(H_B, 2 * LANE),
                 tok3(KV_LORA), tok3(LANE), tok3(D), tok3(D)]
    out_shape = [jax.ShapeDtypeStruct((B, H_A, S, HD_A), BF16),
                 jax.ShapeDtypeStruct((B, KV_A, S, HD_A), F32),
                 jax.ShapeDtypeStruct((B, KV_A, S, HD_A), F32),
                 jax.ShapeDtypeStruct((B, H_B, S, 2 * LANE), BF16),
                 jax.ShapeDtypeStruct((B, S, KV_LORA), F32),
                 jax.ShapeDtypeStruct((B, S, LANE), F32),
                 jax.ShapeDtypeStruct((B, S, D), F32),
                 jax.ShapeDtypeStruct((B, S, D), F32)]
    return pl.pallas_call(
        functools.partial(_proj_kernel, use_rope),
        grid=(n_tiles + 1,),
        in_specs=in_specs, out_specs=out_specs, out_shape=out_shape,
        scratch_shapes=[pltpu.VMEM((TM, C_END), F32), pltpu.VMEM((TM, C_END), F32),
                        pltpu.VMEM((TM, 2 * H_B * LANE), F32), pltpu.VMEM((TM, 2 * H_B * LANE), F32)],
        compiler_params=pltpu.CompilerParams(dimension_semantics=("arbitrary",), vmem_limit_bytes=VMEM_LIMIT),
        name="proj_rope" if use_rope else "proj",
    )(*args)


def _attend(n_heads, q_ref, key, value, o_ref):
    for hh in range(n_heads):
        v = value(hh)
        dv = v.shape[-1] // 2
        s = lax.dot_general(q_ref[0, hh], key(hh), (((1,), (1,)), ((), ())), preferred_element_type=F32)
        p = jnp.exp2(s - jnp.max(s, axis=-1, keepdims=True))
        o = _dot(p.astype(BF16), v)
        o_ref[0, :, hh * dv:(hh + 1) * dv] = (o[:, :dv] / o[:, dv:]).astype(BF16)


def _attn_kernel(n_heads, group, q_ref, k_ref, v_ref, o_ref):
    _attend(n_heads, q_ref, lambda hh: k_ref[0, hh // group], lambda hh: v_ref[0, hh // group], o_ref)


def _attn_mla_kernel(q_ref, ckv_ref, kr_ref, wk_ref, wv_ref, gk_ref, o_ref, kb_ref, vb_ref):
    @pl.when(pl.program_id(1) == 0)
    def _():
        def chunk(c, carry):
            rows = pl.ds(pl.multiple_of(c * TL, TL), TL)
            cb = ckv_ref[0, rows, :].astype(BF16)
            k = _dot(cb, wk_ref[...])
            v = _dot(cb, wv_ref[...])
            krb = kr_ref[0, rows, :].astype(BF16)
            for hh in range(H_B):
                t = _rms(k[:, hh * LANE:(hh + 1) * LANE], NOPE_B) * gk_ref[...]
                kb_ref[hh, rows, 0:LANE] = t.astype(BF16)
                kb_ref[hh, rows, LANE:2 * LANE] = krb
                vb_ref[hh, rows, 0:VD_B] = v[:, hh * LANE:(hh + 1) * LANE].astype(BF16)
                vb_ref[hh, rows, VD_B:2 * VD_B] = jnp.ones((TL, VD_B), BF16)
            return carry

        lax.fori_loop(0, ckv_ref.shape[1] // TL, chunk, 0)

    _attend(H_B, q_ref, lambda hh: kb_ref[hh], lambda hh: vb_ref[hh], o_ref)


def _attention_mla(q, ckv_all, krp_all, w):
    B, H, S, dk = q.shape
    L = ckv_all.shape[1]
    return pl.pallas_call(
        _attn_mla_kernel,
        grid=(B, S // TQ),
        in_specs=[pl.BlockSpec((1, H, TQ, dk), lambda b, i: (b, 0, i, 0)),
                  pl.BlockSpec((1, L, KV_LORA), lambda b, i: (b, 0, 0)),
                  pl.BlockSpec((1, L, LANE), lambda b, i: (b, 0, 0)),
                  _const_spec((KV_LORA, H_B * LANE)), _const_spec((KV_LORA, H_B * LANE)), _const_spec((1, LANE))],
        out_specs=pl.BlockSpec((1, TQ, H * VD_B), lambda b, i: (b, i, 0)),
        out_shape=jax.ShapeDtypeStruct((B, S, H * VD_B), BF16),
        scratch_shapes=[pltpu.VMEM((H_B, L, 2 * LANE), BF16), pltpu.VMEM((H_B, L, 2 * VD_B), BF16)],
        compiler_params=pltpu.CompilerParams(dimension_semantics=("arbitrary", "arbitrary"),
                                             vmem_limit_bytes=VMEM_LIMIT),
        name="attn_mla",
    )(q, ckv_all, krp_all, w["w_kvk"], w["w_kvv"], w["mla_kn_nope"])


def _attention(q, k, v, name):
    B, H, S, dk = q.shape
    _, Hk, L, dv2 = v.shape
    dv = dv2 // 2
    return pl.pallas_call(
        functools.partial(_attn_kernel, H, H // Hk),
        grid=(B, S // TQ),
        in_specs=[pl.BlockSpec((1, H, TQ, dk), lambda b, i: (b, 0, i, 0)),
                  pl.BlockSpec((1, Hk, L, dk), lambda b, i: (b, 0, 0, 0)),
                  pl.BlockSpec((1, Hk, L, dv2), lambda b, i: (b, 0, 0, 0))],
        out_specs=pl.BlockSpec((1, TQ, H * dv), lambda b, i: (b, i, 0)),
        out_shape=jax.ShapeDtypeStruct((B, S, H * dv), BF16),
        compiler_params=pltpu.CompilerParams(dimension_semantics=("arbitrary", "arbitrary"),
                                             vmem_limit_bytes=VMEM_LIMIT),
        name=name,
    )(q, k, v)


def _route(bi, sc, carry):
    n_tok = bi.shape[1]
    iota8 = lax.broadcasted_iota(I32, (GROUP_SIZE, n_tok), 0)
    neg = -jnp.inf
    grp = [bi[j * GROUP_SIZE:(j + 1) * GROUP_SIZE] for j in range(N_GROUPS)]
    gscore = []
    for g in grp:
        m1 = jnp.max(g, axis=0, keepdims=True)
        first = jnp.min(jnp.where(g == m1, iota8, GROUP_SIZE), axis=0, keepdims=True)
        m2 = jnp.max(jnp.where(iota8 == first, neg, g), axis=0, keepdims=True)
        gscore.append(m1 + m2)
    masked = []
    for j in range(N_GROUPS):
        cnt = jnp.zeros((1, n_tok), I32)
        for j2 in range(N_GROUPS):
            if j2 == j:
                continue
            beats = (gscore[j2] > gscore[j]) if j2 > j else (gscore[j2] >= gscore[j])
            cnt = cnt + beats.astype(I32)
        keep = jnp.broadcast_to(cnt, grp[j].shape) < TOPK_GROUPS
        masked.append(jnp.where(keep, grp[j], neg))
    eid = [(iota8 + j * GROUP_SIZE).astype(F32) for j in range(N_GROUPS)]
    scg = [sc[j * GROUP_SIZE:(j + 1) * GROUP_SIZE] for j in range(N_GROUPS)]
    work = list(masked)
    kept = [jnp.zeros((GROUP_SIZE, n_tok), F32) for _ in range(N_GROUPS)]
    idx_rows = jnp.zeros((TOP_K, n_tok), F32)
    score_rows = jnp.zeros((TOP_K, n_tok), F32)
    for k in range(TOP_K):
        top = work[0]
        for j in range(1, N_GROUPS):
            top = jnp.maximum(top, work[j])
        top = jnp.max(top, axis=0, keepdims=True)
        cand = jnp.where(work[0] == top, eid[0], float(N_EXPERTS))
        for j in range(1, N_GROUPS):
            cand = jnp.minimum(cand, jnp.where(work[j] == top, eid[j], float(N_EXPERTS)))
        chosen = jnp.min(cand, axis=0, keepdims=True)
        score = jnp.zeros((GROUP_SIZE, n_tok), F32)
        for j in range(N_GROUPS):
            hit = eid[j] == chosen
            score = score + jnp.where(hit, scg[j], 0.0)
            kept[j] = kept[j] + jnp.where(hit, 1.0, 0.0)
            work[j] = jnp.where(hit, neg, work[j])
        idx_rows = jnp.where(iota8 == k, chosen, idx_rows)
        score_rows = jnp.where(iota8 == k, jnp.sum(score, axis=0, keepdims=True), score_rows)
    w_rows = score_rows / jnp.sum(score_rows, axis=0, keepdims=True) * ROUTED_SCALE
    kept = jnp.concatenate(kept, axis=0)
    earlier = (lax.broadcasted_iota(I32, (n_tok, n_tok), 0) < lax.broadcasted_iota(I32, (n_tok, n_tok), 1))
    rank = _dot(kept.astype(BF16), earlier.astype(F32).astype(BF16)) + carry
    new_carry = carry + jnp.sum(kept, axis=1, keepdims=True)
    rank_rows = jnp.zeros((TOP_K, n_tok), F32)
    for k in range(TOP_K):
        ra = jnp.zeros((GROUP_SIZE, n_tok), F32)
        for j in range(N_GROUPS):
            ra = ra + jnp.where(eid[j] == idx_rows[k:k + 1], rank[j * GROUP_SIZE:(j + 1) * GROUP_SIZE], 0.0)
        rank_rows = jnp.where(iota8 == k, jnp.sum(ra, axis=0, keepdims=True), rank_rows)
    return idx_rows, w_rows, rank_rows, new_carry


def _post_kernel(tiles_per_group, oa_ref, ob_ref, ga_ref, gb_ref, x_ref, mod_ref, g2n_ref, woa_ref, wob_ref,
                 wout_ref, wr_ref, rb_ref, ws1_ref, ws3_ref, ws2_ref, base_ref, hp_ref, idx_ref, wk_ref, rank_ref,
                 carry_ref):
    tile = pl.program_id(0) * pl.num_programs(1) + pl.program_id(1)

    @pl.when(tile % tiles_per_group == 0)
    def _():
        carry_ref[...] = jnp.zeros_like(carry_ref)

    m = mod_ref[0]
    g1, sh2, sc2, g2 = m[2:3], m[3:4], m[4:5], m[5:6]
    merged = ga_ref[0] * _dot(oa_ref[0], woa_ref[...]) + gb_ref[0] * _dot(ob_ref[0], wob_ref[...])
    x1 = x_ref[0] + g1 * _dot(merged.astype(BF16), wout_ref[...])
    h2 = (_rms(x1, D) * g2n_ref[...]) * (1.0 + sc2) + sh2
    h2b = h2.astype(BF16)
    act = _silu(_dot(h2b, ws1_ref[...])) * _dot(h2b, ws3_ref[...])
    base_ref[0] = x1 + g2 * _dot(act.astype(BF16), ws2_ref[...])
    n_tok = h2.shape[0]
    for j in range(HP_ROWS):
        hp_ref[pl.ds(j, n_tok, stride=HP_ROWS), :] = h2[:, j * LANE:(j + 1) * LANE]
    h2lo = (h2 - h2b.astype(F32)).astype(BF16)
    hw = _dot(h2b, wr_ref[...])
    logits = hw[:, 0:LANE] + (hw[:, LANE:2 * LANE] + _dot(h2lo, wr_ref[:, 0:LANE]))
    sc = jax.nn.sigmoid(logits.T[0:N_EXPERTS])
    idx_rows, w_rows, rank_rows, carry = _route(sc + rb_ref[...], sc, carry_ref[...])
    carry_ref[...] = carry
    idx_ref[...] = idx_rows.astype(I32)
    wk_ref[...] = w_rows
    rank_ref[...] = rank_rows.astype(I32)


def _post(oa, ob, ga, gb, x, modsel, per_batch, w, gt):
    B, S, _ = x.shape
    nt = S // TM
    mod_idx = (lambda b, i: (b, 0, 0)) if per_batch else (lambda b, i: (0, 0, 0))
    tok = lambda d: pl.BlockSpec((1, TM, d), lambda b, i: (b, i, 0))
    slot = lambda: pl.BlockSpec((TOP_K, TM), lambda b, i: (0, b * nt + i))
    return pl.pallas_call(
        functools.partial(_post_kernel, gt // TM),
        grid=(B, nt),
        in_specs=[tok(D), tok(D), tok(D), tok(D), tok(D), pl.BlockSpec((1, N_MOD, D), mod_idx),
                  _const_spec((1, D)), _const_spec((D, D)), _const_spec((D, D)), _const_spec((D, D)),
                  _const_spec((D, 2 * LANE)), _const_spec((N_EXPERTS, 1)),
                  _const_spec((D, D_EXPERT)), _const_spec((D, D_EXPERT)), _const_spec((D_EXPERT, D))],
        out_specs=[tok(D),
                   pl.BlockSpec((TM * HP_ROWS, LANE), lambda b, i: (b * nt + i, 0)),
                   slot(), slot(), slot()],
        out_shape=[jax.ShapeDtypeStruct((B, S, D), F32),
                   jax.ShapeDtypeStruct((B * S * HP_ROWS, LANE), F32),
                   jax.ShapeDtypeStruct((TOP_K, B * S), I32),
                   jax.ShapeDtypeStruct((TOP_K, B * S), F32),
                   jax.ShapeDtypeStruct((TOP_K, B * S), I32)],
        scratch_shapes=[pltpu.VMEM((N_EXPERTS, 1), F32)],
        compiler_params=pltpu.CompilerParams(dimension_semantics=("arbitrary", "arbitrary"),
                                             vmem_limit_bytes=VMEM_LIMIT),
        name="post",
    )(oa, ob, ga, gb, x, modsel, w["norm2_g"], w["w_oa"], w["w_ob"], w["w_out"], w["w_router"],
      w["router_bias"], w["w_s1"], w["w_s3"], w["w_s2"])


def _dispatch_rows(idx, wk, rank, n_groups):
    T = idx.shape[1]
    gt = T // n_groups
    experts = jnp.arange(N_EXPERTS, dtype=I32)
    regroup = lambda a: a.reshape(TOP_K, n_groups, gt).transpose(1, 0, 2)
    eid = regroup(idx)
    onehot = (eid[:, None] == experts[None, :, None, None]).astype(I32)
    counts = jnp.sum(onehot, axis=(2, 3))
    padded = (counts + (MB - 1)) // MB * MB
    bounds = jnp.concatenate([jnp.zeros((n_groups, 1), I32), jnp.cumsum(padded, axis=1)], axis=1)
    row = jnp.sum(onehot * bounds[:, :N_EXPERTS, None, None], axis=1) + regroup(rank)
    wtab = jnp.pad(regroup(wk), ((0, 0), (0, 0), (0, gt)))
    return bounds, counts, row.reshape(n_groups, TOP_K * gt), wtab.reshape(n_groups, TOP_K * 2 * gt)


def _moe_kernel(gt, seg_ref, cnt_ref, row_ref, wtab_ref, hp_ref, w1_ref, w3_ref, w2_ref, o_ref,
                tin0_ref, tin1_ref, tout0_ref, tout1_ref, w1b_ref, w3b_ref, w2b_ref, code_ref):
    tin = (tin0_ref, tin1_ref)
    tout = (tout0_ref, tout1_ref)

    def build_row_list():
        def spare_tail(ex, carry):
            def one(p, c):
                code_ref[p] = gt
                return c
            return lax.fori_loop(seg_ref[ex] + cnt_ref[ex], seg_ref[ex + 1], one, carry)

        lax.fori_loop(0, N_EXPERTS, spare_tail, 0)

        def chunk(i, carry):
            t0 = i * LIST_CHUNK
            for u in range(LIST_CHUNK):
                for k in range(TOP_K):
                    code_ref[row_ref[k * gt + t0 + u]] = k * 2 * gt + t0 + u
            return carry

        lax.fori_loop(0, gt // LIST_CHUNK, chunk, 0)

    e = pl.program_id(0)
    n_total = lax.shift_right_logical(seg_ref[N_EXPERTS], MB_LOG2)

    def gather(b, slot):
        r0 = b * MB
        for m in range(MB):
            t = code_ref[r0 + m] & (gt - 1)
            slab = hp_ref[pl.ds(pl.multiple_of(t * HP_ROWS, HP_ROWS), HP_ROWS), :]
            tin[slot][pl.ds(m, HP_ROWS, stride=S_IN), :] = slab

    def scatter(b, slot, live):
        r0 = b * MB
        for b0 in range(0, MB, RMW_BATCH):
            pending = []
            for m in range(b0, b0 + RMW_BATCH):
                code = code_ref[r0 + m]
                off = pl.multiple_of((code & (2 * gt - 1)) * OUT_ROWS, OUT_ROWS)
                upd = tout[slot][pl.ds(m, OUT_ROWS, stride=S_OUT), :] * wtab_ref[code]
                if live is not None:
                    upd = jnp.where(live, upd, 0.0)
                pending.append((off, o_ref[pl.ds(off, OUT_ROWS), :] + upd))
            for off, v in pending:
                o_ref[pl.ds(off, OUT_ROWS), :] = v

    @pl.when(e == 0)
    def _():
        build_row_list()
        o_ref[...] = jnp.zeros_like(o_ref)
        tout1_ref[...] = jnp.zeros_like(tout1_ref)
        gather(0, 0)

    def stages(b, slot):
        gather(jnp.minimum(b + 1, n_total - 1), 1 - slot)
        x = jnp.concatenate([tin[slot][pl.ds(j * S_IN, MB), :] for j in range(HP_ROWS)], axis=-1).astype(BF16)
        act = _silu(_dot(x, w1b_ref[...])) * _dot(x, w3b_ref[...])
        y = _dot(act.astype(BF16), w2b_ref[...])
        for j in range(OUT_ROWS):
            tout[slot][pl.ds(j * S_OUT, MB), :] = y[:, j * LANE:(j + 1) * LANE]
        scatter(jnp.maximum(b - 1, 0), 1 - slot, b > 0)

    def expert(sub, carry):
        ex = e * EXPERTS_PER_STEP + sub
        w1b_ref[...] = w1_ref[sub].astype(BF16)
        w3b_ref[...] = w3_ref[sub].astype(BF16)
        w2b_ref[...] = w2_ref[sub].astype(BF16)
        first = lax.shift_right_logical(seg_ref[ex], MB_LOG2)

        def block(i, c):
            b = first + i
            for slot in range(2):
                @pl.when(b & 1 == slot)
                def _(slot=slot):
                    stages(b, slot)
            return c

        return lax.fori_loop(0, lax.shift_right_logical(seg_ref[ex + 1], MB_LOG2) - first, block, carry)

    lax.fori_loop(0, EXPERTS_PER_STEP, expert, 0)

    @pl.when(e == N_EXPERTS // EXPERTS_PER_STEP - 1)
    def _():
        for slot in range(2):
            @pl.when((n_total - 1) & 1 == slot)
            def _(slot=slot):
                scatter(n_total - 1, slot, None)


def _moe(hp, group, gt, seg, counts, rows, wtab, w):
    n_rows = TOP_K * gt + N_EXPERTS * MB
    grid_spec = pltpu.PrefetchScalarGridSpec(
        num_scalar_prefetch=4,
        grid=(N_EXPERTS // EXPERTS_PER_STEP,),
        in_specs=[pl.BlockSpec((gt * HP_ROWS, LANE), lambda e, *_: (group, 0), pipeline_mode=pl.Buffered(1)),
                  pl.BlockSpec((EXPERTS_PER_STEP, D, D_EXPERT), lambda e, *_: (e, 0, 0)),
                  pl.BlockSpec((EXPERTS_PER_STEP, D, D_EXPERT), lambda e, *_: (e, 0, 0)),
                  pl.BlockSpec((EXPERTS_PER_STEP, D_EXPERT, D), lambda e, *_: (e, 0, 0))],
        out_specs=pl.BlockSpec(((gt + 1) * OUT_ROWS, LANE), lambda e, *_: (0, 0), pipeline_mode=pl.Buffered(1)),
        scratch_shapes=[pltpu.VMEM((HP_ROWS * S_IN, LANE), F32), pltpu.VMEM((HP_ROWS * S_IN, LANE), F32),
                        pltpu.VMEM((OUT_ROWS * S_OUT, LANE), F32), pltpu.VMEM((OUT_ROWS * S_OUT, LANE), F32),
                        pltpu.VMEM((D, D_EXPERT), BF16), pltpu.VMEM((D, D_EXPERT), BF16),
                        pltpu.VMEM((D_EXPERT, D), BF16),
                        pltpu.SMEM((n_rows,), I32)])
    return pl.pallas_call(
        functools.partial(_moe_kernel, gt),
        grid_spec=grid_spec,
        out_shape=jax.ShapeDtypeStruct(((gt + 1) * OUT_ROWS, LANE), F32),
        compiler_params=pltpu.CompilerParams(dimension_semantics=("arbitrary",), vmem_limit_bytes=VMEM_LIMIT),
        name="moe_experts",
    )(seg, counts, rows, wtab, hp, w["w_e1"], w["w_e3"], w["w_e2"])


def _combine_kernel(n_groups, tiles_per_group, *refs):
    r_refs = refs[:n_groups]
    base_ref, mod_ref, o_ref = refs[n_groups:]
    tile = pl.program_id(0) * pl.num_programs(1) + pl.program_id(1)
    g2 = mod_ref[0][5:6]
    n_tok = base_ref.shape[1]
    for g, r_ref in enumerate(r_refs):
        @pl.when(tile // tiles_per_group == g)
        def _(r_ref=r_ref):
            for s in range(OUT_ROWS):
                cols = slice(s * LANE, (s + 1) * LANE)
                o_ref[0, :, cols] = base_ref[0, :, cols] + g2[:, cols] * r_ref[pl.ds(s, n_tok, stride=OUT_ROWS), :]


def _combine(routed, base, modsel, per_batch, gt):
    B, S, _ = base.shape
    nt = S // TM
    tiles_per_group = gt // TM
    mod_idx = (lambda b, i: (b, 0, 0)) if per_batch else (lambda b, i: (0, 0, 0))

    def r_spec(g):
        def idx(b, i):
            tile = b * nt + i
            return (jnp.where(tile // tiles_per_group == g, tile % tiles_per_group, 0), 0)
        return pl.BlockSpec((TM * OUT_ROWS, LANE), idx)

    return pl.pallas_call(
        functools.partial(_combine_kernel, len(routed), tiles_per_group),
        grid=(B, nt),
        in_specs=[r_spec(g) for g in range(len(routed))] + [
            pl.BlockSpec((1, TM, D), lambda b, i: (b, i, 0)),
            pl.BlockSpec((1, N_MOD, D), mod_idx)],
        out_specs=pl.BlockSpec((1, TM, D), lambda b, i: (b, i, 0)),
        out_shape=jax.ShapeDtypeStruct((B, S, D), F32),
        compiler_params=pltpu.CompilerParams(dimension_semantics=("arbitrary", "arbitrary"),
                                             vmem_limit_bytes=VMEM_LIMIT),
        name="moe_combine",
    )(*routed, base, modsel)


def _trunk_pass(x, modsel, per_batch, w, rope_tabs, cache):
    B, S, _ = x.shape
    qa, ka, va, qb, ckv, krp, ga, gb = _project(x, modsel, per_batch, w, rope_tabs)
    if cache is None:
        ka_all, va_all, ckv_all, krp_all = ka, va, ckv, krp
    else:
        ck, cv, cckv, ckr = cache
        ka_all = jnp.concatenate([ck, ka], axis=2)
        va_all = jnp.concatenate([cv, va], axis=2)
        ckv_all = jnp.concatenate([cckv, ckv], axis=1)
        krp_all = jnp.concatenate([_pad64(ckr), krp], axis=1)
    va_ones = jnp.concatenate([va_all.astype(BF16), jnp.ones(va_all.shape, BF16)], axis=-1)
    oa = _attention(qa, ka_all.astype(BF16), va_ones, "attn_gqa")
    ob = _attention_mla(qb, ckv_all, krp_all, w)
    T = B * S
    gt = min(GROUP_TOKENS, T)
    assert T % gt == 0 and gt % TM == 0 and gt & (gt - 1) == 0 and gt >= MB and gt % LIST_CHUNK == 0
    base, hp, idx, wk, rank = _post(oa, ob, ga, gb, x, modsel, per_batch, w, gt)
    seg, counts, rows, wtab = _dispatch_rows(idx, wk, rank, T // gt)
    routed = [_moe(hp, g, gt, seg[g], counts[g], rows[g], wtab[g], w) for g in range(T // gt)]
    y = _combine(routed, base, modsel, per_batch, gt)
    return y, (ka, va, ckv, _unpad64(krp))


def _rope_tables(n_tokens):
    t = jnp.arange(n_tokens)
    row = (t // GRID_W).astype(F32)[:, None]
    col = (t % GRID_W).astype(F32)[:, None]

    def tabs(rot_dim):
        axis_dim = rot_dim // 2
        inv = ROPE_THETA ** (-jnp.arange(0, axis_dim, 2, dtype=F32) / axis_dim)
        ang = jnp.concatenate([row * inv, col * inv], axis=-1)
        return jnp.cos(ang), jnp.sin(ang)

    cos_a, sin_a = tabs(HD_A)
    cos_b, sin_b = tabs(ROPE_B)
    return (jnp.concatenate([cos_a, cos_a], -1), jnp.concatenate([-sin_a, sin_a], -1),
            _pad64(jnp.concatenate([cos_b, cos_b], -1)), _pad64(jnp.concatenate([-sin_b, sin_b], -1)))


def _layer_weights(l, w_in, w_qb, w_kvb, w_router, router_bias, named):
    w = {k: v[l] for k, v in named.items()}
    wi = w_in[l]
    n_front = C_KR + ROPE_B
    w["w_in"] = jnp.pad(wi[:, :n_front].astype(BF16), ((0, 0), (0, C_GA - n_front)))
    w["w_gate"] = wi[:, n_front:].astype(BF16)
    wq = w_qb[l].reshape(Q_LORA, H_B, NOPE_B + ROPE_B)
    w["w_qn"] = wq[:, :, :NOPE_B].reshape(Q_LORA, H_B * LANE).astype(BF16)
    w["w_qr"] = _pad64(wq[:, :, NOPE_B:]).reshape(Q_LORA, H_B * LANE).astype(BF16)
    wkv = w_kvb[l].reshape(KV_LORA, H_B, NOPE_B + VD_B)
    w["w_kvk"] = wkv[:, :, :NOPE_B].reshape(KV_LORA, H_B * LANE).astype(BF16)
    w["w_kvv"] = wkv[:, :, NOPE_B:].reshape(KV_LORA, H_B * LANE).astype(BF16)
    wr = jnp.pad(w_router[l], ((0, 0), (0, LANE - N_EXPERTS)))
    wr_hi = wr.astype(BF16)
    w["w_router"] = jnp.concatenate([wr_hi, (wr - wr_hi.astype(F32)).astype(BF16)], axis=1)
    w["router_bias"] = router_bias[l].reshape(N_EXPERTS, 1)
    for k in ("norm1_g", "norm2_g", "gqa_qn", "gqa_kn", "mla_qa_g", "mla_kva_g", "mla_qn_nope", "mla_kn_nope"):
        w[k] = w[k].reshape(1, -1)
    w["mla_qn_rope"] = _pad64(w["mla_qn_rope"]).reshape(1, LANE)
    w["mla_kn_rope"] = jnp.pad(w["mla_kn_rope"], (0, LANE - ROPE_B)).reshape(1, LANE)
    for k in ("w_oa", "w_ob", "w_out", "w_s1", "w_s3", "w_s2"):
        w[k] = w[k].astype(BF16)
    return w


def kernel(x_prompt, x_sample, c, cache_gqa_k, cache_gqa_v, cache_mla_ckv, cache_mla_krope, c_ctx, w_mod, b_mod, norm1_g, norm2_g, w_in, gqa_qn, gqa_kn, mla_qa_g, mla_kva_g, w_qb, w_kvb, mla_qn_nope, mla_qn_rope, mla_kn_nope, mla_kn_rope, w_oa, w_ob, w_out, w_router, router_bias, w_e1, w_e3, w_e2, w_s1, w_s3, w_s2):
    depth = w_mod.shape[0]
    n_dec = x_sample.shape[0]
    rope_tabs = _rope_tables(x_sample.shape[1])
    cond8 = jnp.concatenate([c_ctx[None, :], c, jnp.zeros((8 - 1 - n_dec, D), F32)], axis=0)
    named = dict(norm1_g=norm1_g, norm2_g=norm2_g, gqa_qn=gqa_qn, gqa_kn=gqa_kn, mla_qa_g=mla_qa_g,
                 mla_kva_g=mla_kva_g, mla_qn_nope=mla_qn_nope, mla_qn_rope=mla_qn_rope, mla_kn_nope=mla_kn_nope,
                 mla_kn_rope=mla_kn_rope, w_oa=w_oa, w_ob=w_ob, w_out=w_out, w_e1=w_e1, w_e3=w_e3, w_e2=w_e2,
                 w_s1=w_s1, w_s3=w_s3, w_s2=w_s2)
    xp, xs = x_prompt, x_sample
    ks, vs, ckvs, krs = [], [], [], []
    for l in range(depth):
        w = _layer_weights(l, w_in, w_qb, w_kvb, w_router, router_bias, named)
        mod = _modulation(cond8, w_mod[l], b_mod[l].reshape(1, -1))
        mod_ctx = mod[0:1].reshape(1, N_MOD, D)
        mod_dec = mod[1:1 + n_dec].reshape(n_dec, N_MOD, D)
        xp, ctx = _trunk_pass(xp, mod_ctx, False, w, None, None)
        ks.append(ctx[0])
        vs.append(ctx[1])
        ckvs.append(ctx[2])
        krs.append(ctx[3])
        cache = (cache_gqa_k[:, l], cache_gqa_v[:, l], cache_mla_ckv[:, l], cache_mla_krope[:, l])
        xs, _ = _trunk_pass(xs, mod_dec, True, w, rope_tabs, cache)
    return (xp, xs, jnp.stack(ks, axis=1), jnp.stack(vs, axis=1), jnp.stack(ckvs, axis=1), jnp.stack(krs, axis=1))
```

```python
import functools
import math

import jax
import jax.numpy as jnp
from jax import lax
from jax.experimental import pallas as pl
from jax.experimental.pallas import tpu as pltpu

F32 = jnp.float32
BF16 = jnp.bfloat16
I32 = jnp.int32
HIGHEST = lax.Precision.HIGHEST

D = 1024
GRID_W = 64
RMS_EPS = 1e-6
ROPE_THETA = 10000.0
N_MOD = 6
H_A, KV_A, HD_A = 8, 2, 128
H_B, Q_LORA, KV_LORA, NOPE_B, ROPE_B, VD_B = 8, 256, 128, 128, 64, 128
N_EXPERTS, TOP_K, N_GROUPS, TOPK_GROUPS = 64, 8, 8, 4
GROUP_SIZE = N_EXPERTS // N_GROUPS
D_EXPERT = 256
ROUTED_SCALE = 2.5
QSCALE_A = HD_A ** -0.5 * math.log2(math.e)
QSCALE_B = (NOPE_B + ROPE_B) ** -0.5 * math.log2(math.e)

LANE = 128
C_QA, C_KA, C_VA, C_CQ, C_CKV, C_KR, C_GA, C_GB, C_END = 0, 1024, 1280, 1536, 1792, 1920, 2048, 3072, 4096
VMEM_LIMIT = 52 * 1024 * 1024

TM = 256
TQ = 256
TL = 256
GROUP_TOKENS = 4096
MB, MB_LOG2 = 256, 8
HP_ROWS = D // LANE
OUT_ROWS = D // LANE
S_IN = MB + 4
S_OUT = MB + 4
RMW_BATCH = 8
LIST_CHUNK = 16
EXPERTS_PER_STEP = 2


def _dot(a, b):
    return jnp.dot(a, b, preferred_element_type=F32)


def _rms(t, n_valid):
    ms = jnp.sum(t * t, axis=-1, keepdims=True) * (1.0 / n_valid)
    return t * lax.rsqrt(ms + RMS_EPS)


def _silu(t):
    return t * jax.nn.sigmoid(t)


def _pad64(a):
    z = jnp.zeros(a.shape[:-1] + (32,), a.dtype)
    return jnp.concatenate([a[..., :32], z, a[..., 32:], z], axis=-1)


def _unpad64(a):
    return jnp.concatenate([a[..., :32], a[..., 64:96]], axis=-1)


def _const_spec(shape):
    nd = len(shape)
    return pl.BlockSpec(shape, lambda *_: (0,) * nd)


def _mod_kernel(c_ref, w_ref, b_ref, o_ref):
    s = _silu(c_ref[...])
    o_ref[...] = jnp.dot(s, w_ref[...], preferred_element_type=F32, precision=HIGHEST) + b_ref[...]


def _modulation(cond8, w_mod, b_mod):
    tn = 1024
    n = w_mod.shape[1]
    return pl.pallas_call(
        _mod_kernel,
        grid=(n // tn,),
        in_specs=[pl.BlockSpec((8, D), lambda j: (0, 0)),
                  pl.BlockSpec((D, tn), lambda j: (0, j)),
                  pl.BlockSpec((1, tn), lambda j: (0, j))],
        out_specs=pl.BlockSpec((8, tn), lambda j: (0, j)),
        out_shape=jax.ShapeDtypeStruct((8, n), F32),
        compiler_params=pltpu.CompilerParams(dimension_semantics=("arbitrary",), vmem_limit_bytes=VMEM_LIMIT),
        name="adaln_mod",
    )(cond8, w_mod, b_mod)


def _proj_kernel(use_rope, *refs):
    (x_ref, mod_ref, g1_ref, win_ref, wgate_ref, wqn_ref, wqr_ref,
     gqn_ref, gkn_ref, gcq_ref, gckv_ref, gnope_ref, gqr_ref, gkr_ref) = refs[:14]
    n_in = 18 if use_rope else 14
    if use_rope:
        ca_ref, sa_ref, cb_ref, sb_ref = refs[14:18]
    qa_ref, ka_ref, va_ref, qb_ref, ckv_ref, kr_ref, ga_ref, gb_ref = refs[n_in:n_in + 8]
    zbuf = refs[n_in + 8:n_in + 10]
    qbuf = refs[n_in + 10:n_in + 12]
    step = pl.program_id(0)

    def rope(t, c_ref, s_ref):
        return t * c_ref[...] + pltpu.roll(t, LANE // 2, 1) * s_ref[...]

    rope_a = (lambda t: rope(t, ca_ref, sa_ref)) if use_rope else (lambda t: t)
    rope_b = (lambda t: rope(t, cb_ref, sb_ref)) if use_rope else (lambda t: t)

    @pl.when(step == 0)
    def _():
        zbuf[1][...] = jnp.zeros_like(zbuf[1])
        qbuf[1][...] = jnp.zeros_like(qbuf[1])

    def matmuls(z_ref, q_ref):
        x = x_ref[0]
        m = mod_ref[0]
        sh1, sc1 = m[0:1], m[1:2]
        hb = ((_rms(x, D) * g1_ref[...]) * (1.0 + sc1) + sh1).astype(BF16)
        for lo, hi in ((C_QA, C_KA), (C_KA, C_CQ), (C_CKV, C_GA)):
            z_ref[:, lo:hi] = _dot(hb, win_ref[:, lo:hi])
        for lo, hi in ((C_GA, C_GB), (C_GB, C_END)):
            z_ref[:, lo:hi] = _dot(hb, wgate_ref[:, lo - C_GA:hi - C_GA])
        cq = (_rms(_dot(hb, win_ref[:, C_CQ:C_CKV]), Q_LORA) * gcq_ref[...]).astype(BF16)
        q_ref[:, 0:H_B * LANE] = _dot(cq, wqn_ref[...])
        q_ref[:, H_B * LANE:2 * H_B * LANE] = _dot(cq, wqr_ref[...])

    def epilogue(z_ref, q_ref):
        for hh in range(H_A):
            t = _rms(z_ref[:, C_QA + hh * HD_A:C_QA + (hh + 1) * HD_A], HD_A) * gqn_ref[...]
            qa_ref[0, hh] = (rope_a(t) * QSCALE_A).astype(BF16)
        for j in range(KV_A):
            t = _rms(z_ref[:, C_KA + j * HD_A:C_KA + (j + 1) * HD_A], HD_A) * gkn_ref[...]
            ka_ref[0, j] = rope_a(t)
            va_ref[0, j] = z_ref[:, C_VA + j * HD_A:C_VA + (j + 1) * HD_A]
        for hh in range(H_B):
            t = _rms(q_ref[:, hh * LANE:(hh + 1) * LANE], NOPE_B) * gnope_ref[...]
            qb_ref[0, hh, :, 0:LANE] = (t * QSCALE_B).astype(BF16)
            t = _rms(q_ref[:, (H_B + hh) * LANE:(H_B + hh + 1) * LANE], ROPE_B) * gqr_ref[...]
            qb_ref[0, hh, :, LANE:2 * LANE] = (rope_b(t) * QSCALE_B).astype(BF16)
        ckv_ref[0] = _rms(z_ref[:, C_CKV:C_KR], KV_LORA) * gckv_ref[...]
        t = _rms(z_ref[:, C_KR:C_GA], ROPE_B) * gkr_ref[...]
        lane = lax.broadcasted_iota(I32, t.shape, 1)
        second = jnp.where((lane >= LANE // 2) & (lane < 3 * LANE // 4), pltpu.roll(t, LANE // 4, 1), 0.0)
        kr_ref[0] = rope_b(jnp.where(lane < LANE // 4, t, second))
        ga_ref[0] = jax.nn.sigmoid(z_ref[:, C_GA:C_GB])
        gb_ref[0] = jax.nn.sigmoid(z_ref[:, C_GB:C_END])

    for parity in range(2):
        @pl.when(step % 2 == parity)
        def _(parity=parity):
            matmuls(zbuf[parity], qbuf[parity])
            epilogue(zbuf[1 - parity], qbuf[1 - parity])


def _project(x, modsel, per_batch, w, rope_tabs):
    B, S, _ = x.shape
    use_rope = rope_tabs is not None
    nt = S // TM
    n_tiles = B * nt
    cur = lambda s: jnp.minimum(s, n_tiles - 1)
    prev = lambda s: jnp.maximum(s - 1, 0)
    mod_idx = (lambda s: (cur(s) // nt, 0, 0)) if per_batch else (lambda s: (0, 0, 0))
    in_specs = [pl.BlockSpec((1, TM, D), lambda s: (cur(s) // nt, cur(s) % nt, 0)),
                pl.BlockSpec((1, N_MOD, D), mod_idx),
                _const_spec((1, D)), _const_spec((D, C_GA)), _const_spec((D, C_END - C_GA)),
                _const_spec((Q_LORA, H_B * LANE)), _const_spec((Q_LORA, H_B * LANE)),
                _const_spec((1, LANE)), _const_spec((1, LANE)), _const_spec((1, Q_LORA)), _const_spec((1, LANE)),
                _const_spec((1, LANE)), _const_spec((1, LANE)), _const_spec((1, LANE))]
    args = [x, modsel, w["norm1_g"], w["w_in"], w["w_gate"], w["w_qn"], w["w_qr"], w["gqa_qn"], w["gqa_kn"], w["mla_qa_g"],
            w["mla_kva_g"], w["mla_qn_nope"], w["mla_qn_rope"], w["mla_kn_rope"]]
    if use_rope:
        in_specs += [pl.BlockSpec((TM, LANE), lambda s: (prev(s) % nt, 0))] * 4
        args += list(rope_tabs)
    tok4 = lambda h, d: pl.BlockSpec((1, h, TM, d), lambda s: (prev(s) // nt, 0, prev(s) % nt, 0))
    tok3 = lambda d: pl.BlockSpec((1, TM, d), lambda s: (prev(s) // nt, prev(s) % nt, 0))
    out_specs = [tok4(H_A, HD_A), tok4(KV_A, HD_A), tok4(KV_A, HD_A), tok4(H_B, 2 * LANE),
                 tok3(KV_LORA), tok3(LANE), tok3(D), tok3(D)]
    out_shape = [jax.ShapeDtypeStruct((B, H_A, S, HD_A), BF16),
                 jax.ShapeDtypeStruct((B, KV_A, S, HD_A), F32),
                 jax.ShapeDtypeStruct((B, KV_A, S, HD_A), F32),
                 jax.ShapeDtypeStruct((B, H_B, S, 2 * LANE), BF16),
                 jax.ShapeDtypeStruct((B, S, KV_LORA), F32),
                 jax.ShapeDtypeStruct((B, S, LANE), F32),
                 jax.ShapeDtypeStruct((B, S, D), F32),
                 jax.ShapeDtypeStruct((B, S, D), F32)]
    return pl.pallas_call(
        functools.partial(_proj_kernel, use_rope),
        grid=(n_tiles + 1,),
        in_specs=in_specs, out_specs=out_specs, out_shape=out_shape,
        scratch_shapes=[pltpu.VMEM((TM, C_END), F32), pltpu.VMEM((TM, C_END), F32),
                        pltpu.VMEM((TM, 2 * H_B * LANE), F32), pltpu.VMEM((TM, 2 * H_B * LANE), F32)],
        compiler_params=pltpu.CompilerParams(dimension_semantics=("arbitrary",), vmem_limit_bytes=VMEM_LIMIT),
        name="proj_rope" if use_rope else "proj",
    )(*args)


def _attend(n_heads, q_ref, key, value, o_ref):
    for hh in range(n_heads):
        v = value(hh)
        dv = v.shape[-1] // 2
        s = lax.dot_general(q_ref[0, hh], key(hh), (((1,), (1,)), ((), ())), preferred_element_type=F32)
        p = jnp.exp2(s - jnp.max(s, axis=-1, keepdims=True))
        o = _dot(p.astype(BF16), v)
        o_ref[0, :, hh * dv:(hh + 1) * dv] = (o[:, :dv] / o[:, dv:]).astype(BF16)


def _attn_kernel(n_heads, group, q_ref, k_ref, v_ref, o_ref):
    _attend(n_heads, q_ref, lambda hh: k_ref[0, hh // group], lambda hh: v_ref[0, hh // group], o_ref)


def _attn_mla_kernel(q_ref, ckv_ref, kr_ref, wk_ref, wv_ref, gk_ref, o_ref, kb_ref, vb_ref):
    @pl.when(pl.program_id(1) == 0)
    def _():
        def chunk(c, carry):
            rows = pl.ds(pl.multiple_of(c * TL, TL), TL)
            cb = ckv_ref[0, rows, :].astype(BF16)
            k = _dot(cb, wk_ref[...])
            v = _dot(cb, wv_ref[...])
            krb = kr_ref[0, rows, :].astype(BF16)
            for hh in range(H_B):
                t = _rms(k[:, hh * LANE:(hh + 1) * LANE], NOPE_B) * gk_ref[...]
                kb_ref[hh, rows, 0:LANE] = t.astype(BF16)
                kb_ref[hh, rows, LANE:2 * LANE] = krb
                vb_ref[hh, rows, 0:VD_B] = v[:, hh * LANE:(hh + 1) * LANE].astype(BF16)
                vb_ref[hh, rows, VD_B:2 * VD_B] = jnp.ones((TL, VD_B), BF16)
            return carry

        lax.fori_loop(0, ckv_ref.shape[1] // TL, chunk, 0)

    _attend(H_B, q_ref, lambda hh: kb_ref[hh], lambda hh: vb_ref[hh], o_ref)


def _attention_mla(q, ckv_all, krp_all, w):
    B, H, S, dk = q.shape
    L = ckv_all.shape[1]
    return pl.pallas_call(
        _attn_mla_kernel,
        grid=(B, S // TQ),
        in_specs=[pl.BlockSpec((1, H, TQ, dk), lambda b, i: (b, 0, i, 0)),
                  pl.BlockSpec((1, L, KV_LORA), lambda b, i: (b, 0, 0)),
                  pl.BlockSpec((1, L, LANE), lambda b, i: (b, 0, 0)),
                  _const_spec((KV_LORA, H_B * LANE)), _const_spec((KV_LORA, H_B * LANE)), _const_spec((1, LANE))],
        out_specs=pl.BlockSpec((1, TQ, H * VD_B), lambda b, i: (b, i, 0)),
        out_shape=jax.ShapeDtypeStruct((B, S, H * VD_B), BF16),
        scratch_shapes=[pltpu.VMEM((H_B, L, 2 * LANE), BF16), pltpu.VMEM((H_B, L, 2 * VD_B), BF16)],
        compiler_params=pltpu.CompilerParams(dimension_semantics=("arbitrary", "arbitrary"),
                                             vmem_limit_bytes=VMEM_LIMIT),
        name="attn_mla",
    )(q, ckv_all, krp_all, w["w_kvk"], w["w_kvv"], w["mla_kn_nope"])


def _attention(q, k, v, name):
    B, H, S, dk = q.shape
    _, Hk, L, dv2 = v.shape
    dv = dv2 // 2
    return pl.pallas_call(
        functools.partial(_attn_kernel, H, H // Hk),
        grid=(B, S // TQ),
        in_specs=[pl.BlockSpec((1, H, TQ, dk), lambda b, i: (b, 0, i, 0)),
                  pl.BlockSpec((1, Hk, L, dk), lambda b, i: (b, 0, 0, 0)),
                  pl.BlockSpec((1, Hk, L, dv2), lambda b, i: (b, 0, 0, 0))],
        out_specs=pl.BlockSpec((1, TQ, H * dv), lambda b, i: (b, i, 0)),
        out_shape=jax.ShapeDtypeStruct((B, S, H * dv), BF16),
        compiler_params=pltpu.CompilerParams(dimension_semantics=("arbitrary", "arbitrary"),
                                             vmem_limit_bytes=VMEM_LIMIT),
        name=name,
    )(q, k, v)


def _route(bi, sc, carry):
    n_tok = bi.shape[1]
    iota8 = lax.broadcasted_iota(I32, (GROUP_SIZE, n_tok), 0)
    neg = -jnp.inf
    grp = [bi[j * GROUP_SIZE:(j + 1) * GROUP_SIZE] for j in range(N_GROUPS)]
    gscore = []
    for g in grp:
        m1 = jnp.max(g, axis=0, keepdims=True)
        first = jnp.min(jnp.where(g == m1, iota8, GROUP_SIZE), axis=0, keepdims=True)
        m2 = jnp.max(jnp.where(iota8 == first, neg, g), axis=0, keepdims=True)
        gscore.append(m1 + m2)
    masked = []
    for j in range(N_GROUPS):
        cnt = jnp.zeros((1, n_tok), I32)
        for j2 in range(N_GROUPS):
            if j2 == j:
                continue
            beats = (gscore[j2] > gscore[j]) if j2 > j else (gscore[j2] >= gscore[j])
            cnt = cnt + beats.astype(I32)
        keep = jnp.broadcast_to(cnt, grp[j].shape) < TOPK_GROUPS
        masked.append(jnp.where(keep, grp[j], neg))
    eid = [(iota8 + j * GROUP_SIZE).astype(F32) for j in range(N_GROUPS)]
    scg = [sc[j * GROUP_SIZE:(j + 1) * GROUP_SIZE] for j in range(N_GROUPS)]
    work = list(masked)
    kept = [jnp.zeros((GROUP_SIZE, n_tok), F32) for _ in range(N_GROUPS)]
    idx_rows = jnp.zeros((TOP_K, n_tok), F32)
    score_rows = jnp.zeros((TOP_K, n_tok), F32)
    for k in range(TOP_K):
        top = work[0]
        for j in range(1, N_GROUPS):
            top = jnp.maximum(top, work[j])
        top = jnp.max(top, axis=0, keepdims=True)
        cand = jnp.where(work[0] == top, eid[0], float(N_EXPERTS))
        for j in range(1, N_GROUPS):
            cand = jnp.minimum(cand, jnp.where(work[j] == top, eid[j], float(N_EXPERTS)))
        chosen = jnp.min(cand, axis=0, keepdims=True)
        score = jnp.zeros((GROUP_SIZE, n_tok), F32)
        for j in range(N_GROUPS):
            hit = eid[j] == chosen
            score = score + jnp.where(hit, scg[j], 0.0)
            kept[j] = kept[j] + jnp.where(hit, 1.0, 0.0)
            work[j] = jnp.where(hit, neg, work[j])
        idx_rows = jnp.where(iota8 == k, chosen, idx_rows)
        score_rows = jnp.where(iota8 == k, jnp.sum(score, axis=0, keepdims=True), score_rows)
    w_rows = score_rows / jnp.sum(score_rows, axis=0, keepdims=True) * ROUTED_SCALE
    kept = jnp.concatenate(kept, axis=0)
    earlier = (lax.broadcasted_iota(I32, (n_tok, n_tok), 0) < lax.broadcasted_iota(I32, (n_tok, n_tok), 1))
    rank = _dot(kept.astype(BF16), earlier.astype(F32).astype(BF16)) + carry
    new_carry = carry + jnp.sum(kept, axis=1, keepdims=True)
    rank_rows = jnp.zeros((TOP_K, n_tok), F32)
    for k in range(TOP_K):
        ra = jnp.zeros((GROUP_SIZE, n_tok), F32)
        for j in range(N_GROUPS):
            ra = ra + jnp.where(eid[j] == idx_rows[k:k + 1], rank[j * GROUP_SIZE:(j + 1) * GROUP_SIZE], 0.0)
        rank_rows = jnp.where(iota8 == k, jnp.sum(ra, axis=0, keepdims=True), rank_rows)
    return idx_rows, w_rows, rank_rows, new_carry


def _post_kernel(tiles_per_group, oa_ref, ob_ref, ga_ref, gb_ref, x_ref, mod_ref, g2n_ref, woa_ref, wob_ref,
                 wout_ref, wr_ref, rb_ref, ws1_ref, ws3_ref, ws2_ref, base_ref, hp_ref, idx_ref, wk_ref, rank_ref,
                 carry_ref):
    tile = pl.program_id(0) * pl.num_programs(1) + pl.program_id(1)

    @pl.when(tile % tiles_per_group == 0)
    def _():
        carry_ref[...] = jnp.zeros_like(carry_ref)

    m = mod_ref[0]
    g1, sh2, sc2, g2 = m[2:3], m[3:4], m[4:5], m[5:6]
    merged = ga_ref[0] * _dot(oa_ref[0], woa_ref[...]) + gb_ref[0] * _dot(ob_ref[0], wob_ref[...])
    x1 = x_ref[0] + g1 * _dot(merged.astype(BF16), wout_ref[...])
    h2 = (_rms(x1, D) * g2n_ref[...]) * (1.0 + sc2) + sh2
    h2b = h2.astype(BF16)
    act = _silu(_dot(h2b, ws1_ref[...])) * _dot(h2b, ws3_ref[...])
    base_ref[0] = x1 + g2 * _dot(act.astype(BF16), ws2_ref[...])
    n_tok = h2.shape[0]
    for j in range(HP_ROWS):
        hp_ref[pl.ds(j, n_tok, stride=HP_ROWS), :] = h2[:, j * LANE:(j + 1) * LANE]
    h2lo = (h2 - h2b.astype(F32)).astype(BF16)
    hw = _dot(h2b, wr_ref[...])
    logits = hw[:, 0:LANE] + (hw[:, LANE:2 * LANE] + _dot(h2lo, wr_ref[:, 0:LANE]))
    sc = jax.nn.sigmoid(logits.T[0:N_EXPERTS])
    idx_rows, w_rows, rank_rows, carry = _route(sc + rb_ref[...], sc, carry_ref[...])
    carry_ref[...] = carry
    idx_ref[...] = idx_rows.astype(I32)
    wk_ref[...] = w_rows
    rank_ref[...] = rank_rows.astype(I32)


def _post(oa, ob, ga, gb, x, modsel, per_batch, w, gt):
    B, S, _ = x.shape
    nt = S // TM
    mod_idx = (lambda b, i: (b, 0, 0)) if per_batch else (lambda b, i: (0, 0, 0))
    tok = lambda d: pl.BlockSpec((1, TM, d), lambda b, i: (b, i, 0))
    slot = lambda: pl.BlockSpec((TOP_K, TM), lambda b, i: (0, b * nt + i))
    return pl.pallas_call(
        functools.partial(_post_kernel, gt // TM),
        grid=(B, nt),
        in_specs=[tok(D), tok(D), tok(D), tok(D), tok(D), pl.BlockSpec((1, N_MOD, D), mod_idx),
                  _const_spec((1, D)), _const_spec((D, D)), _const_spec((D, D)), _const_spec((D, D)),
                  _const_spec((D, 2 * LANE)), _const_spec((N_EXPERTS, 1)),
                  _const_spec((D, D_EXPERT)), _const_spec((D, D_EXPERT)), _const_spec((D_EXPERT, D))],
        out_specs=[tok(D),
                   pl.BlockSpec((TM * HP_ROWS, LANE), lambda b, i: (b * nt + i, 0)),
                   slot(), slot(), slot()],
        out_shape=[jax.ShapeDtypeStruct((B, S, D), F32),
                   jax.ShapeDtypeStruct((B * S * HP_ROWS, LANE), F32),
                   jax.ShapeDtypeStruct((TOP_K, B * S), I32),
                   jax.ShapeDtypeStruct((TOP_K, B * S), F32),
                   jax.ShapeDtypeStruct((TOP_K, B * S), I32)],
        scratch_shapes=[pltpu.VMEM((N_EXPERTS, 1), F32)],
        compiler_params=pltpu.CompilerParams(dimension_semantics=("arbitrary", "arbitrary"),
                                             vmem_limit_bytes=VMEM_LIMIT),
        name="post",
    )(oa, ob, ga, gb, x, modsel, w["norm2_g"], w["w_oa"], w["w_ob"], w["w_out"], w["w_router"],
      w["router_bias"], w["w_s1"], w["w_s3"], w["w_s2"])


def _dispatch_rows(idx, wk, rank, n_groups):
    T = idx.shape[1]
    gt = T // n_groups
    experts = jnp.arange(N_EXPERTS, dtype=I32)
    regroup = lambda a: a.reshape(TOP_K, n_groups, gt).transpose(1, 0, 2)
    eid = regroup(idx)
    onehot = (eid[:, None] == experts[None, :, None, None]).astype(I32)
    counts = jnp.sum(onehot, axis=(2, 3))
    padded = (counts + (MB - 1)) // MB * MB
    bounds = jnp.concatenate([jnp.zeros((n_groups, 1), I32), jnp.cumsum(padded, axis=1)], axis=1)
    row = jnp.sum(onehot * bounds[:, :N_EXPERTS, None, None], axis=1) + regroup(rank)
    wtab = jnp.pad(regroup(wk), ((0, 0), (0, 0), (0, gt)))
    return bounds, counts, row.reshape(n_groups, TOP_K * gt), wtab.reshape(n_groups, TOP_K * 2 * gt)


def _moe_kernel(gt, seg_ref, cnt_ref, row_ref, wtab_ref, hp_ref, w1_ref, w3_ref, w2_ref, o_ref,
                tin0_ref, tin1_ref, tout0_ref, tout1_ref, w1b_ref, w3b_ref, w2b_ref, code_ref):
    tin = (tin0_ref, tin1_ref)
    tout = (tout0_ref, tout1_ref)

    def build_row_list():
        def spare_tail(ex, carry):
            def one(p, c):
                code_ref[p] = gt
                return c
            return lax.fori_loop(seg_ref[ex] + cnt_ref[ex], seg_ref[ex + 1], one, carry)

        lax.fori_loop(0, N_EXPERTS, spare_tail, 0)

        def chunk(i, carry):
            t0 = i * LIST_CHUNK
            for u in range(LIST_CHUNK):
                for k in range(TOP_K):
                    code_ref[row_ref[k * gt + t0 + u]] = k * 2 * gt + t0 + u
            return carry

        lax.fori_loop(0, gt // LIST_CHUNK, chunk, 0)

    e = pl.program_id(0)
    n_total = lax.shift_right_logical(seg_ref[N_EXPERTS], MB_LOG2)

    def gather(b, slot):
        r0 = b * MB
        for m in range(MB):
            t = code_ref[r0 + m] & (gt - 1)
            slab = hp_ref[pl.ds(pl.multiple_of(t * HP_ROWS, HP_ROWS), HP_ROWS), :]
            tin[slot][pl.ds(m, HP_ROWS, stride=S_IN), :] = slab

    def scatter(b, slot, live):
        r0 = b * MB
        for b0 in range(0, MB, RMW_BATCH):
            pending = []
            for m in range(b0, b0 + RMW_BATCH):
                code = code_ref[r0 + m]
                off = pl.multiple_of((code & (2 * gt - 1)) * OUT_ROWS, OUT_ROWS)
                upd = tout[slot][pl.ds(m, OUT_ROWS, stride=S_OUT), :] * wtab_ref[code]
                if live is not None:
                    upd = jnp.where(live, upd, 0.0)
                pending.append((off, o_ref[pl.ds(off, OUT_ROWS), :] + upd))
            for off, v in pending:
                o_ref[pl.ds(off, OUT_ROWS), :] = v

    @pl.when(e == 0)
    def _():
        build_row_list()
        o_ref[...] = jnp.zeros_like(o_ref)
        tout1_ref[...] = jnp.zeros_like(tout1_ref)
        gather(0, 0)

    def stages(b, slot):
        gather(jnp.minimum(b + 1, n_total - 1), 1 - slot)
        x = jnp.concatenate([tin[slot][pl.ds(j * S_IN, MB), :] for j in range(HP_ROWS)], axis=-1).astype(BF16)
        act = _silu(_dot(x, w1b_ref[...])) * _dot(x, w3b_ref[...])
        y = _dot(act.astype(BF16), w2b_ref[...])
        for j in range(OUT_ROWS):
            tout[slot][pl.ds(j * S_OUT, MB), :] = y[:, j * LANE:(j + 1) * LANE]
        scatter(jnp.maximum(b - 1, 0), 1 - slot, b > 0)

    def expert(sub, carry):
        ex = e * EXPERTS_PER_STEP + sub
        w1b_ref[...] = w1_ref[sub].astype(BF16)
        w3b_ref[...] = w3_ref[sub].astype(BF16)
        w2b_ref[...] = w2_ref[sub].astype(BF16)
        first = lax.shift_right_logical(seg_ref[ex], MB_LOG2)

        def block(i, c):
            b = first + i
            for slot in range(2):
                @pl.when(b & 1 == slot)
                def _(slot=slot):
                    stages(b, slot)
            return c

        return lax.fori_loop(0, lax.shift_right_logical(seg_ref[ex + 1], MB_LOG2) - first, block, carry)

    lax.fori_loop(0, EXPERTS_PER_STEP, expert, 0)

    @pl.when(e == N_EXPERTS // EXPERTS_PER_STEP - 1)
    def _():
        for slot in range(2):
            @pl.when((n_total - 1) & 1 == slot)
            def _(slot=slot):
                scatter(n_total - 1, slot, None)


def _moe(hp, group, gt, seg, counts, rows, wtab, w):
    n_rows = TOP_K * gt + N_EXPERTS * MB
    grid_spec = pltpu.PrefetchScalarGridSpec(
        num_scalar_prefetch=4,
        grid=(N_EXPERTS // EXPERTS_PER_STEP,),
        in_specs=[pl.BlockSpec((gt * HP_ROWS, LANE), lambda e, *_: (group, 0), pipeline_mode=pl.Buffered(1)),
                  pl.BlockSpec((EXPERTS_PER_STEP, D, D_EXPERT), lambda e, *_: (e, 0, 0)),
                  pl.BlockSpec((EXPERTS_PER_STEP, D, D_EXPERT), lambda e, *_: (e, 0, 0)),
                  pl.BlockSpec((EXPERTS_PER_STEP, D_EXPERT, D), lambda e, *_: (e, 0, 0))],
        out_specs=pl.BlockSpec(((gt + 1) * OUT_ROWS, LANE), lambda e, *_: (0, 0), pipeline_mode=pl.Buffered(1)),
        scratch_shapes=[pltpu.VMEM((HP_ROWS * S_IN, LANE), F32), pltpu.VMEM((HP_ROWS * S_IN, LANE), F32),
                        pltpu.VMEM((OUT_ROWS * S_OUT, LANE), F32), pltpu.VMEM((OUT_ROWS * S_OUT, LANE), F32),
                        pltpu.VMEM((D, D_EXPERT), BF16), pltpu.VMEM((D, D_EXPERT), BF16),
                        pltpu.VMEM((D_EXPERT, D), BF16),
                        pltpu.SMEM((n_rows,), I32)])
    return pl.pallas_call(
        functools.partial(_moe_kernel, gt),
        grid_spec=grid_spec,
        out_shape=jax.ShapeDtypeStruct(((gt + 1) * OUT_ROWS, LANE), F32),
        compiler_params=pltpu.CompilerParams(dimension_semantics=("arbitrary",), vmem_limit_bytes=VMEM_LIMIT),
        name="moe_experts",
    )(seg, counts, rows, wtab, hp, w["w_e1"], w["w_e3"], w["w_e2"])


def _combine_kernel(n_groups, tiles_per_group, *refs):
    r_refs = refs[:n_groups]
    base_ref, mod_ref, o_ref = refs[n_groups:]
    tile = pl.program_id(0) * pl.num_programs(1) + pl.program_id(1)
    g2 = mod_ref[0][5:6]
    n_tok = base_ref.shape[1]
    for g, r_ref in enumerate(r_refs):
        @pl.when(tile // tiles_per_group == g)
        def _(r_ref=r_ref):
            for s in range(OUT_ROWS):
                cols = slice(s * LANE, (s + 1) * LANE)
                o_ref[0, :, cols] = base_ref[0, :, cols] + g2[:, cols] * r_ref[pl.ds(s, n_tok, stride=OUT_ROWS), :]


def _combine(routed, base, modsel, per_batch, gt):
    B, S, _ = base.shape
    nt = S // TM
    tiles_per_group = gt // TM
    mod_idx = (lambda b, i: (b, 0, 0)) if per_batch else (lambda b, i: (0, 0, 0))

    def r_spec(g):
        def idx(b, i):
            tile = b * nt + i
            return (jnp.where(tile // tiles_per_group == g, tile % tiles_per_group, 0), 0)
        return pl.BlockSpec((TM * OUT_ROWS, LANE), idx)

    return pl.pallas_call(
        functools.partial(_combine_kernel, len(routed), tiles_per_group),
        grid=(B, nt),
        in_specs=[r_spec(g) for g in range(len(routed))] + [
            pl.BlockSpec((1, TM, D), lambda b, i: (b, i, 0)),
            pl.BlockSpec((1, N_MOD, D), mod_idx)],
        out_specs=pl.BlockSpec((1, TM, D), lambda b, i: (b, i, 0)),
        out_shape=jax.ShapeDtypeStruct((B, S, D), F32),
        compiler_params=pltpu.CompilerParams(dimension_semantics=("arbitrary", "arbitrary"),
                                             vmem_limit_bytes=VMEM_LIMIT),
        name="moe_combine",
    )(*routed, base, modsel)


def _trunk_pass(x, modsel, per_batch, w, rope_tabs, cache):
    B, S, _ = x.shape
    qa, ka, va, qb, ckv, krp, ga, gb = _project(x, modsel, per_batch, w, rope_tabs)
    if cache is None:
        ka_all, va_all, ckv_all, krp_all = ka, va, ckv, krp
    else:
        ck, cv, cckv, ckr = cache
        ka_all = jnp.concatenate([ck, ka], axis=2)
        va_all = jnp.concatenate([cv, va], axis=2)
        ckv_all = jnp.concatenate([cckv, ckv], axis=1)
        krp_all = jnp.concatenate([_pad64(ckr), krp], axis=1)
    va_ones = jnp.concatenate([va_all.astype(BF16), jnp.ones(va_all.shape, BF16)], axis=-1)
    oa = _attention(qa, ka_all.astype(BF16), va_ones, "attn_gqa")
    ob = _attention_mla(qb, ckv_all, krp_all, w)
    T = B * S
    gt = min(GROUP_TOKENS, T)
    assert T % gt == 0 and gt % TM == 0 and gt & (gt - 1) == 0 and gt >= MB and gt % LIST_CHUNK == 0
    base, hp, idx, wk, rank = _post(oa, ob, ga, gb, x, modsel, per_batch, w, gt)
    seg, counts, rows, wtab = _dispatch_rows(idx, wk, rank, T // gt)
    routed = [_moe(hp, g, gt, seg[g], counts[g], rows[g], wtab[g], w) for g in range(T // gt)]
    y = _combine(routed, base, modsel, per_batch, gt)
    return y, (ka, va, ckv, _unpad64(krp))


def _rope_tables(n_tokens):
    t = jnp.arange(n_tokens)
    row = (t // GRID_W).astype(F32)[:, None]
    col = (t % GRID_W).astype(F32)[:, None]

    def tabs(rot_dim):
        axis_dim = rot_dim // 2
        inv = ROPE_THETA ** (-jnp.arange(0, axis_dim, 2, dtype=F32) / axis_dim)
        ang = jnp.concatenate([row * inv, col * inv], axis=-1)
        return jnp.cos(ang), jnp.sin(ang)

    cos_a, sin_a = tabs(HD_A)
    cos_b, sin_b = tabs(ROPE_B)
    return (jnp.concatenate([cos_a, cos_a], -1), jnp.concatenate([-sin_a, sin_a], -1),
            _pad64(jnp.concatenate([cos_b, cos_b], -1)), _pad64(jnp.concatenate([-sin_b, sin_b], -1)))


def _layer_weights(l, w_in, w_qb, w_kvb, w_router, router_bias, named):
    w = {k: v[l] for k, v in named.items()}
    wi = w_in[l]
    n_front = C_KR + ROPE_B
    w["w_in"] = jnp.pad(wi[:, :n_front].astype(BF16), ((0, 0), (0, C_GA - n_front)))
    w["w_gate"] = wi[:, n_front:].astype(BF16)
    wq = w_qb[l].reshape(Q_LORA, H_B, NOPE_B + ROPE_B)
    w["w_qn"] = wq[:, :, :NOPE_B].reshape(Q_LORA, H_B * LANE).astype(BF16)
    w["w_qr"] = _pad64(wq[:, :, NOPE_B:]).reshape(Q_LORA, H_B * LANE).astype(BF16)
    wkv = w_kvb[l].reshape(KV_LORA, H_B, NOPE_B + VD_B)
    w["w_kvk"] = wkv[:, :, :NOPE_B].reshape(KV_LORA, H_B * LANE).astype(BF16)
    w["w_kvv"] = wkv[:, :, NOPE_B:].reshape(KV_LORA, H_B * LANE).astype(BF16)
    wr = jnp.pad(w_router[l], ((0, 0), (0, LANE - N_EXPERTS)))
    wr_hi = wr.astype(BF16)
    w["w_router"] = jnp.concatenate([wr_hi, (wr - wr_hi.astype(F32)).astype(BF16)], axis=1)
    w["router_bias"] = router_bias[l].reshape(N_EXPERTS, 1)
    for k in ("norm1_g", "norm2_g", "gqa_qn", "gqa_kn", "mla_qa_g", "mla_kva_g", "mla_qn_nope", "mla_kn_nope"):
        w[k] = w[k].reshape(1, -1)
    w["mla_qn_rope"] = _pad64(w["mla_qn_rope"]).reshape(1, LANE)
    w["mla_kn_rope"] = jnp.pad(w["mla_kn_rope"], (0, LANE - ROPE_B)).reshape(1, LANE)
    for k in ("w_oa", "w_ob", "w_out", "w_s1", "w_s3", "w_s2"):
        w[k] = w[k].astype(BF16)
    return w


def kernel(x_prompt, x_sample, c, cache_gqa_k, cache_gqa_v, cache_mla_ckv, cache_mla_krope, c_ctx, w_mod, b_mod, norm1_g, norm2_g, w_in, gqa_qn, gqa_kn, mla_qa_g, mla_kva_g, w_qb, w_kvb, mla_qn_nope, mla_qn_rope, mla_kn_nope, mla_kn_rope, w_oa, w_ob, w_out, w_router, router_bias, w_e1, w_e3, w_e2, w_s1, w_s3, w_s2):
    depth = w_mod.shape[0]
    n_dec = x_sample.shape[0]
    rope_tabs = _rope_tables(x_sample.shape[1])
    cond8 = jnp.concatenate([c_ctx[None, :], c, jnp.zeros((8 - 1 - n_dec, D), F32)], axis=0)
    named = dict(norm1_g=norm1_g, norm2_g=norm2_g, gqa_qn=gqa_qn, gqa_kn=gqa_kn, mla_qa_g=mla_qa_g,
                 mla_kva_g=mla_kva_g, mla_qn_nope=mla_qn_nope, mla_qn_rope=mla_qn_rope, mla_kn_nope=mla_kn_nope,
                 mla_kn_rope=mla_kn_rope, w_oa=w_oa, w_ob=w_ob, w_out=w_out, w_e1=w_e1, w_e3=w_e3, w_e2=w_e2,
                 w_s1=w_s1, w_s3=w_s3, w_s2=w_s2)
    xp, xs = x_prompt, x_sample
    ks, vs, ckvs, krs = [], [], [], []
    for l in range(depth):
        w = _layer_weights(l, w_in, w_qb, w_kvb, w_router, router_bias, named)
        mod = _modulation(cond8, w_mod[l], b_mod[l].reshape(1, -1))
        mod_ctx = mod[0:1].reshape(1, N_MOD, D)
        mod_dec = mod[1:1 + n_dec].reshape(n_dec, N_MOD, D)
        xp, ctx = _trunk_pass(xp, mod_ctx, False, w, None, None)
        ks.append(ctx[0])
        vs.append(ctx[1])
        ckvs.append(ctx[2])
        krs.append(ctx[3])
        cache = (cache_gqa_k[:, l], cache_gqa_v[:, l], cache_mla_ckv[:, l], cache_mla_krope[:, l])
        xs, _ = _trunk_pass(xs, mod_dec, True, w, rope_tabs, cache)
    return (xp, xs, jnp.stack(ks, axis=1), jnp.stack(vs, axis=1), jnp.stack(ckvs, axis=1), jnp.stack(krs, axis=1))
```

```python
import functools
import math

import jax
import jax.numpy as jnp
from jax import lax
from jax.experimental import pallas as pl
from jax.experimental.pallas import tpu as pltpu

F32 = jnp.float32
BF16 = jnp.bfloat16
I32 = jnp.int32
HIGHEST = lax.Precision.HIGHEST

D = 1024
GRID_W = 64
RMS_EPS = 1e-6
ROPE_THETA = 10000.0
N_MOD = 6
H_A, KV_A, HD_A = 8, 2, 128
H_B, Q_LORA, KV_LORA, NOPE_B, ROPE_B, VD_B = 8, 256, 128, 128, 64, 128
N_EXPERTS, TOP_K, N_GROUPS, TOPK_GROUPS = 64, 8, 8, 4
GROUP_SIZE = N_EXPERTS // N_GROUPS
D_EXPERT = 256
ROUTED_SCALE = 2.5
QSCALE_A = HD_A ** -0.5 * math.log2(math.e)
QSCALE_B = (NOPE_B + ROPE_B) ** -0.5 * math.log2(math.e)

LANE = 128
C_QA, C_KA, C_VA, C_CQ, C_CKV, C_KR, C_GA, C_GB, C_END = 0, 1024, 1280, 1536, 1792, 1920, 2048, 3072, 4096
VMEM_LIMIT = 52 * 1024 * 1024

TM = 256
TQ = 256
TL = 256
GROUP_TOKENS = 4096
MB, MB_LOG2 = 128, 7
HP_ROWS = D // LANE
OUT_ROWS = D // LANE
S_IN = MB + 4
S_OUT = MB + 4
RMW_BATCH = 8
LIST_CHUNK = 16
EXPERTS_PER_STEP = 2


def _dot(a, b):
    return jnp.dot(a, b, preferred_element_type=F32)


def _rms(t, n_valid):
    ms = jnp.sum(t * t, axis=-1, keepdims=True) * (1.0 / n_valid)
    return t * lax.rsqrt(ms + RMS_EPS)


def _silu(t):
    return t * jax.nn.sigmoid(t)


def _pad64(a):
    z = jnp.zeros(a.shape[:-1] + (32,), a.dtype)
    return jnp.concatenate([a[..., :32], z, a[..., 32:], z], axis=-1)


def _unpad64(a):
    return jnp.concatenate([a[..., :32], a[..., 64:96]], axis=-1)


def _const_spec(shape):
    nd = len(shape)
    return pl.BlockSpec(shape, lambda *_: (0,) * nd)


def _mod_kernel(c_ref, w_ref, b_ref, o_ref):
    s = _silu(c_ref[...])
    o_ref[...] = jnp.dot(s, w_ref[...], preferred_element_type=F32, precision=HIGHEST) + b_ref[...]


def _modulation(cond8, w_mod, b_mod):
    tn = 1024
    n = w_mod.shape[1]
    return pl.pallas_call(
        _mod_kernel,
        grid=(n // tn,),
        in_specs=[pl.BlockSpec((8, D), lambda j: (0, 0)),
                  pl.BlockSpec((D, tn), lambda j: (0, j)),
                  pl.BlockSpec((1, tn), lambda j: (0, j))],
        out_specs=pl.BlockSpec((8, tn), lambda j: (0, j)),
        out_shape=jax.ShapeDtypeStruct((8, n), F32),
        compiler_params=pltpu.CompilerParams(dimension_semantics=("arbitrary",), vmem_limit_bytes=VMEM_LIMIT),
        name="adaln_mod",
    )(cond8, w_mod, b_mod)


def _proj_kernel(use_rope, *refs):
    (x_ref, mod_ref, g1_ref, win_ref, wgate_ref, wqn_ref, wqr_ref,
     gqn_ref, gkn_ref, gcq_ref, gckv_ref, gnope_ref, gqr_ref, gkr_ref) = refs[:14]
    n_in = 18 if use_rope else 14
    if use_rope:
        ca_ref, sa_ref, cb_ref, sb_ref = refs[14:18]
    qa_ref, ka_ref, va_ref, qb_ref, ckv_ref, kr_ref, ga_ref, gb_ref = refs[n_in:n_in + 8]
    zbuf = refs[n_in + 8:n_in + 10]
    qbuf = refs[n_in + 10:n_in + 12]
    step = pl.program_id(0)

    def rope(t, c_ref, s_ref):
        return t * c_ref[...] + pltpu.roll(t, LANE // 2, 1) * s_ref[...]

    rope_a = (lambda t: rope(t, ca_ref, sa_ref)) if use_rope else (lambda t: t)
    rope_b = (lambda t: rope(t, cb_ref, sb_ref)) if use_rope else (lambda t: t)

    @pl.when(step == 0)
    def _():
        zbuf[1][...] = jnp.zeros_like(zbuf[1])
        qbuf[1][...] = jnp.zeros_like(qbuf[1])

    def matmuls(z_ref, q_ref):
        x = x_ref[0]
        m = mod_ref[0]
        sh1, sc1 = m[0:1], m[1:2]
        hb = ((_rms(x, D) * g1_ref[...]) * (1.0 + sc1) + sh1).astype(BF16)
        for lo, hi in ((C_QA, C_KA), (C_KA, C_CQ), (C_CKV, C_GA)):
            z_ref[:, lo:hi] = _dot(hb, win_ref[:, lo:hi])
        for lo, hi in ((C_GA, C_GB), (C_GB, C_END)):
            z_ref[:, lo:hi] = _dot(hb, wgate_ref[:, lo - C_GA:hi - C_GA])
        cq = (_rms(_dot(hb, win_ref[:, C_CQ:C_CKV]), Q_LORA) * gcq_ref[...]).astype(BF16)
        q_ref[:, 0:H_B * LANE] = _dot(cq, wqn_ref[...])
        q_ref[:, H_B * LANE:2 * H_B * LANE] = _dot(cq, wqr_ref[...])

    def epilogue(z_ref, q_ref):
        for hh in range(H_A):
            t = _rms(z_ref[:, C_QA + hh * HD_A:C_QA + (hh + 1) * HD_A], HD_A) * gqn_ref[...]
            qa_ref[0, hh] = (rope_a(t) * QSCALE_A).astype(BF16)
        for j in range(KV_A):
            t = _rms(z_ref[:, C_KA + j * HD_A:C_KA + (j + 1) * HD_A], HD_A) * gkn_ref[...]
            ka_ref[0, j] = rope_a(t)
            va_ref[0, j] = z_ref[:, C_VA + j * HD_A:C_VA + (j + 1) * HD_A]
        for hh in range(H_B):
            t = _rms(q_ref[:, hh * LANE:(hh + 1) * LANE], NOPE_B) * gnope_ref[...]
            qb_ref[0, hh, :, 0:LANE] = (t * QSCALE_B).astype(BF16)
            t = _rms(q_ref[:, (H_B + hh) * LANE:(H_B + hh + 1) * LANE], ROPE_B) * gqr_ref[...]
            qb_ref[0, hh, :, LANE:2 * LANE] = (rope_b(t) * QSCALE_B).astype(BF16)
        ckv_ref[0] = _rms(z_ref[:, C_CKV:C_KR], KV_LORA) * gckv_ref[...]
        t = _rms(z_ref[:, C_KR:C_GA], ROPE_B) * gkr_ref[...]
        lane = lax.broadcasted_iota(I32, t.shape, 1)
        second = jnp.where((lane >= LANE // 2) & (lane < 3 * LANE // 4), pltpu.roll(t, LANE // 4, 1), 0.0)
        kr_ref[0] = rope_b(jnp.where(lane < LANE // 4, t, second))
        ga_ref[0] = jax.nn.sigmoid(z_ref[:, C_GA:C_GB])
        gb_ref[0] = jax.nn.sigmoid(z_ref[:, C_GB:C_END])

    for parity in range(2):
        @pl.when(step % 2 == parity)
        def _(parity=parity):
            matmuls(zbuf[parity], qbuf[parity])
            epilogue(zbuf[1 - parity], qbuf[1 - parity])


def _project(x, modsel, per_batch, w, rope_tabs):
    B, S, _ = x.shape
    use_rope = rope_tabs is not None
    nt = S // TM
    n_tiles = B * nt
    cur = lambda s: jnp.minimum(s, n_tiles - 1)
    prev = lambda s: jnp.maximum(s - 1, 0)
    mod_idx = (lambda s: (cur(s) // nt, 0, 0)) if per_batch else (lambda s: (0, 0, 0))
    in_specs = [pl.BlockSpec((1, TM, D), lambda s: (cur(s) // nt, cur(s) % nt, 0)),
                pl.BlockSpec((1, N_MOD, D), mod_idx),
                _const_spec((1, D)), _const_spec((D, C_GA)), _const_spec((D, C_END - C_GA)),
                _const_spec((Q_LORA, H_B * LANE)), _const_spec((Q_LORA, H_B * LANE)),
                _const_spec((1, LANE)), _const_spec((1, LANE)), _const_spec((1, Q_LORA)), _const_spec((1, LANE)),
                _const_spec((1, LANE)), _const_spec((1, LANE)), _const_spec((1, LANE))]
    args = [x, modsel, w["norm1_g"], w["w_in"], w["w_gate"], w["w_qn"], w["w_qr"], w["gqa_qn"], w["gqa_kn"], w["mla_qa_g"],
            w["mla_kva_g"], w["mla_qn_nope"], w["mla_qn_rope"], w["mla_kn_rope"]]
    if use_rope:
        in_specs += [pl.BlockSpec((TM, LANE), lambda s: (prev(s) % nt, 0))] * 4
        args += list(rope_tabs)
    tok4 = lambda h, d: pl.BlockSpec((1, h, TM, d), lambda s: (prev(s) // nt, 0, prev(s) % nt, 0))
    tok3 = lambda d: pl.BlockSpec((1, TM, d), lambda s: (prev(s) // nt, prev(s) % nt, 0))
    out_specs = [tok4(H_A, HD_A), tok4(KV_A, HD_A), tok4(KV_A, HD_A), tok4(H_B, 2 * LANE),
                 tok3(KV_LORA), tok3(LANE), tok3(D), tok3(D)]
    out_shape = [jax.ShapeDtypeStruct((B, H_A, S, HD_A), BF16),
                 jax.ShapeDtypeStruct((B, KV_A, S, HD_A), F32),
                 jax.ShapeDtypeStruct((B, KV_A, S, HD_A), F32),
                 jax.ShapeDtypeStruct((B, H_B, S, 2 * LANE), BF16),
                 jax.ShapeDtypeStruct((B, S, KV_LORA), F32),
                 jax.ShapeDtypeStruct((B, S, LANE), F32),
                 jax.ShapeDtypeStruct((B, S, D), F32),
                 jax.ShapeDtypeStruct((B, S, D), F32)]
    return pl.pallas_call(
        functools.partial(_proj_kernel, use_rope),
        grid=(n_tiles + 1,),
        in_specs=in_specs, out_specs=out_specs, out_shape=out_shape,
        scratch_shapes=[pltpu.VMEM((TM, C_END), F32), pltpu.VMEM((TM, C_END), F32),
                        pltpu.VMEM((TM, 2 * H_B * LANE), F32), pltpu.VMEM((TM, 2 * H_B * LANE), F32)],
        compiler_params=pltpu.CompilerParams(dimension_semantics=("arbitrary",), vmem_limit_bytes=VMEM_LIMIT),
        name="proj_rope" if use_rope else "proj",
    )(*args)


def _attend(n_heads, q_ref, key, value, o_ref):
    for hh in range(n_heads):
        v = value(hh)
        dv = v.shape[-1] // 2
        s = lax.dot_general(q_ref[0, hh], key(hh), (((1,), (1,)), ((), ())), preferred_element_type=F32)
        p = jnp.exp2(s - jnp.max(s, axis=-1, keepdims=True))
        o = _dot(p.astype(BF16), v)
        o_ref[0, :, hh * dv:(hh + 1) * dv] = (o[:, :dv] / o[:, dv:]).astype(BF16)


def _attn_kernel(n_heads, group, q_ref, k_ref, v_ref, o_ref):
    _attend(n_heads, q_ref, lambda hh: k_ref[0, hh // group], lambda hh: v_ref[0, hh // group], o_ref)


def _attn_mla_kernel(q_ref, ckv_ref, kr_ref, wk_ref, wv_ref, gk_ref, o_ref, kb_ref, vb_ref):
    @pl.when(pl.program_id(1) == 0)
    def _():
        def chunk(c, carry):
            rows = pl.ds(pl.multiple_of(c * TL, TL), TL)
            cb = ckv_ref[0, rows, :].astype(BF16)
            k = _dot(cb, wk_ref[...])
            v = _dot(cb, wv_ref[...])
            krb = kr_ref[0, rows, :].astype(BF16)
            for hh in range(H_B):
                t = _rms(k[:, hh * LANE:(hh + 1) * LANE], NOPE_B) * gk_ref[...]
                kb_ref[hh, rows, 0:LANE] = t.astype(BF16)
                kb_ref[hh, rows, LANE:2 * LANE] = krb
                vb_ref[hh, rows, 0:VD_B] = v[:, hh * LANE:(hh + 1) * LANE].astype(BF16)
                vb_ref[hh, rows, VD_B:2 * VD_B] = jnp.ones((TL, VD_B), BF16)
            return carry

        lax.fori_loop(0, ckv_ref.shape[1] // TL, chunk, 0)

    _attend(H_B, q_ref, lambda hh: kb_ref[hh], lambda hh: vb_ref[hh], o_ref)


def _attention_mla(q, ckv_all, krp_all, w):
    B, H, S, dk = q.shape
    L = ckv_all.shape[1]
    return pl.pallas_call(
        _attn_mla_kernel,
        grid=(B, S // TQ),
        in_specs=[pl.BlockSpec((1, H, TQ, dk), lambda b, i: (b, 0, i, 0)),
                  pl.BlockSpec((1, L, KV_LORA), lambda b, i: (b, 0, 0)),
                  pl.BlockSpec((1, L, LANE), lambda b, i: (b, 0, 0)),
                  _const_spec((KV_LORA, H_B * LANE)), _const_spec((KV_LORA, H_B * LANE)), _const_spec((1, LANE))],
        out_specs=pl.BlockSpec((1, TQ, H * VD_B), lambda b, i: (b, i, 0)),
        out_shape=jax.ShapeDtypeStruct((B, S, H * VD_B), BF16),
        scratch_shapes=[pltpu.VMEM((H_B, L, 2 * LANE), BF16), pltpu.VMEM((H_B, L, 2 * VD_B), BF16)],
        compiler_params=pltpu.CompilerParams(dimension_semantics=("arbitrary", "arbitrary"),
                                             vmem_limit_bytes=VMEM_LIMIT),
        name="attn_mla",
    )(q, ckv_all, krp_all, w["w_kvk"], w["w_kvv"], w["mla_kn_nope"])


def _attention(q, k, v, name):
    B, H, S, dk = q.shape
    _, Hk, L, dv2 = v.shape
    dv = dv2 // 2
    return pl.pallas_call(
        functools.partial(_attn_kernel, H, H // Hk),
        grid=(B, S // TQ),
        in_specs=[pl.BlockSpec((1, H, TQ, dk), lambda b, i: (b, 0, i, 0)),
                  pl.BlockSpec((1, Hk, L, dk), lambda b, i: (b, 0, 0, 0)),
                  pl.BlockSpec((1, Hk, L, dv2), lambda b, i: (b, 0, 0, 0))],
        out_specs=pl.BlockSpec((1, TQ, H * dv), lambda b, i: (b, i, 0)),
        out_shape=jax.ShapeDtypeStruct((B, S, H * dv), BF16),
        compiler_params=pltpu.CompilerParams(dimension_semantics=("arbitrary", "arbitrary"),
                                             vmem_limit_bytes=VMEM_LIMIT),
        name=name,
    )(q, k, v)


def _route(bi, sc, carry):
    n_tok = bi.shape[1]
    iota8 = lax.broadcasted_iota(I32, (GROUP_SIZE, n_tok), 0)
    neg = -jnp.inf
    grp = [bi[j * GROUP_SIZE:(j + 1) * GROUP_SIZE] for j in range(N_GROUPS)]
    gscore = []
    for g in grp:
        m1 = jnp.max(g, axis=0, keepdims=True)
        first = jnp.min(jnp.where(g == m1, iota8, GROUP_SIZE), axis=0, keepdims=True)
        m2 = jnp.max(jnp.where(iota8 == first, neg, g), axis=0, keepdims=True)
        gscore.append(m1 + m2)
    masked = []
    for j in range(N_GROUPS):
        cnt = jnp.zeros((1, n_tok), I32)
        for j2 in range(N_GROUPS):
            if j2 == j:
                continue
            beats = (gscore[j2] > gscore[j]) if j2 > j else (gscore[j2] >= gscore[j])
            cnt = cnt + beats.astype(I32)
        keep = jnp.broadcast_to(cnt, grp[j].shape) < TOPK_GROUPS
        masked.append(jnp.where(keep, grp[j], neg))
    eid = [(iota8 + j * GROUP_SIZE).astype(F32) for j in range(N_GROUPS)]
    scg = [sc[j * GROUP_SIZE:(j + 1) * GROUP_SIZE] for j in range(N_GROUPS)]
    work = list(masked)
    kept = [jnp.zeros((GROUP_SIZE, n_tok), F32) for _ in range(N_GROUPS)]
    idx_rows = jnp.zeros((TOP_K, n_tok), F32)
    score_rows = jnp.zeros((TOP_K, n_tok), F32)
    for k in range(TOP_K):
        top = work[0]
        for j in range(1, N_GROUPS):
            top = jnp.maximum(top, work[j])
        top = jnp.max(top, axis=0, keepdims=True)
        cand = jnp.where(work[0] == top, eid[0], float(N_EXPERTS))
        for j in range(1, N_GROUPS):
            cand = jnp.minimum(cand, jnp.where(work[j] == top, eid[j], float(N_EXPERTS)))
        chosen = jnp.min(cand, axis=0, keepdims=True)
        score = jnp.zeros((GROUP_SIZE, n_tok), F32)
        for j in range(N_GROUPS):
            hit = eid[j] == chosen
            score = score + jnp.where(hit, scg[j], 0.0)
            kept[j] = kept[j] + jnp.where(hit, 1.0, 0.0)
            work[j] = jnp.where(hit, neg, work[j])
        idx_rows = jnp.where(iota8 == k, chosen, idx_rows)
        score_rows = jnp.where(iota8 == k, jnp.sum(score, axis=0, keepdims=True), score_rows)
    w_rows = score_rows / jnp.sum(score_rows, axis=0, keepdims=True) * ROUTED_SCALE
    kept = jnp.concatenate(kept, axis=0)
    earlier = (lax.broadcasted_iota(I32, (n_tok, n_tok), 0) < lax.broadcasted_iota(I32, (n_tok, n_tok), 1))
    rank = _dot(kept.astype(BF16), earlier.astype(F32).astype(BF16)) + carry
    new_carry = carry + jnp.sum(kept, axis=1, keepdims=True)
    rank_rows = jnp.zeros((TOP_K, n_tok), F32)
    for k in range(TOP_K):
        ra = jnp.zeros((GROUP_SIZE, n_tok), F32)
        for j in range(N_GROUPS):
            ra = ra + jnp.where(eid[j] == idx_rows[k:k + 1], rank[j * GROUP_SIZE:(j + 1) * GROUP_SIZE], 0.0)
        rank_rows = jnp.where(iota8 == k, jnp.sum(ra, axis=0, keepdims=True), rank_rows)
    return idx_rows, w_rows, rank_rows, new_carry


def _post_kernel(tiles_per_group, oa_ref, ob_ref, ga_ref, gb_ref, x_ref, mod_ref, g2n_ref, woa_ref, wob_ref,
                 wout_ref, wr_ref, rb_ref, ws1_ref, ws3_ref, ws2_ref, base_ref, hp_ref, idx_ref, wk_ref, rank_ref,
                 carry_ref):
    tile = pl.program_id(0) * pl.num_programs(1) + pl.program_id(1)

    @pl.when(tile % tiles_per_group == 0)
    def _():
        carry_ref[...] = jnp.zeros_like(carry_ref)

    m = mod_ref[0]
    g1, sh2, sc2, g2 = m[2:3], m[3:4], m[4:5], m[5:6]
    merged = ga_ref[0] * _dot(oa_ref[0], woa_ref[...]) + gb_ref[0] * _dot(ob_ref[0], wob_ref[...])
    x1 = x_ref[0] + g1 * _dot(merged.astype(BF16), wout_ref[...])
    h2 = (_rms(x1, D) * g2n_ref[...]) * (1.0 + sc2) + sh2
    h2b = h2.astype(BF16)
    act = _silu(_dot(h2b, ws1_ref[...])) * _dot(h2b, ws3_ref[...])
    base_ref[0] = x1 + g2 * _dot(act.astype(BF16), ws2_ref[...])
    n_tok = h2.shape[0]
    for j in range(HP_ROWS):
        hp_ref[pl.ds(j, n_tok, stride=HP_ROWS), :] = h2[:, j * LANE:(j + 1) * LANE]
    h2lo = (h2 - h2b.astype(F32)).astype(BF16)
    hw = _dot(h2b, wr_ref[...])
    logits = hw[:, 0:LANE] + (hw[:, LANE:2 * LANE] + _dot(h2lo, wr_ref[:, 0:LANE]))
    sc = jax.nn.sigmoid(logits.T[0:N_EXPERTS])
    idx_rows, w_rows, rank_rows, carry = _route(sc + rb_ref[...], sc, carry_ref[...])
    carry_ref[...] = carry
    idx_ref[...] = idx_rows.astype(I32)
    wk_ref[...] = w_rows
    rank_ref[...] = rank_rows.astype(I32)


def _post(oa, ob, ga, gb, x, modsel, per_batch, w, gt):
    B, S, _ = x.shape
    nt = S // TM
    mod_idx = (lambda b, i: (b, 0, 0)) if per_batch else (lambda b, i: (0, 0, 0))
    tok = lambda d: pl.BlockSpec((1, TM, d), lambda b, i: (b, i, 0))
    slot = lambda: pl.BlockSpec((TOP_K, TM), lambda b, i: (0, b * nt + i))
    return pl.pallas_call(
        functools.partial(_post_kernel, gt // TM),
        grid=(B, nt),
        in_specs=[tok(D), tok(D), tok(D), tok(D), tok(D), pl.BlockSpec((1, N_MOD, D), mod_idx),
                  _const_spec((1, D)), _const_spec((D, D)), _const_spec((D, D)), _const_spec((D, D)),
                  _const_spec((D, 2 * LANE)), _const_spec((N_EXPERTS, 1)),
                  _const_spec((D, D_EXPERT)), _const_spec((D, D_EXPERT)), _const_spec((D_EXPERT, D))],
        out_specs=[tok(D),
                   pl.BlockSpec((TM * HP_ROWS, LANE), lambda b, i: (b * nt + i, 0)),
                   slot(), slot(), slot()],
        out_shape=[jax.ShapeDtypeStruct((B, S, D), F32),
                   jax.ShapeDtypeStruct((B * S * HP_ROWS, LANE), F32),
                   jax.ShapeDtypeStruct((TOP_K, B * S), I32),
                   jax.ShapeDtypeStruct((TOP_K, B * S), F32),
                   jax.ShapeDtypeStruct((TOP_K, B * S), I32)],
        scratch_shapes=[pltpu.VMEM((N_EXPERTS, 1), F32)],
        compiler_params=pltpu.CompilerParams(dimension_semantics=("arbitrary", "arbitrary"),
                                             vmem_limit_bytes=VMEM_LIMIT),
        name="post",
    )(oa, ob, ga, gb, x, modsel, w["norm2_g"], w["w_oa"], w["w_ob"], w["w_out"], w["w_router"],
      w["router_bias"], w["w_s1"], w["w_s3"], w["w_s2"])


def _dispatch_rows(idx, wk, rank, n_groups):
    T = idx.shape[1]
    gt = T // n_groups
    experts = jnp.arange(N_EXPERTS, dtype=I32)
    regroup = lambda a: a.reshape(TOP_K, n_groups, gt).transpose(1, 0, 2)
    eid = regroup(idx)
    onehot = (eid[:, None] == experts[None, :, None, None]).astype(I32)
    counts = jnp.sum(onehot, axis=(2, 3))
    padded = (counts + (MB - 1)) // MB * MB
    bounds = jnp.concatenate([jnp.zeros((n_groups, 1), I32), jnp.cumsum(padded, axis=1)], axis=1)
    row = jnp.sum(onehot * bounds[:, :N_EXPERTS, None, None], axis=1) + regroup(rank)
    wtab = jnp.pad(regroup(wk), ((0, 0), (0, 0), (0, gt)))
    return bounds, counts, row.reshape(n_groups, TOP_K * gt), wtab.reshape(n_groups, TOP_K * 2 * gt)


def _moe_kernel(gt, seg_ref, cnt_ref, row_ref, wtab_ref, hp_ref, w1_ref, w3_ref, w2_ref, o_ref,
                tin0_ref, tin1_ref, act0_ref, act1_ref, tout0_ref, tout1_ref, w1b_ref, w3b_ref, w2b_ref,
                code_ref, state_ref):
    tin = (tin0_ref, tin1_ref)
    act = (act0_ref, act1_ref)
    tout = (tout0_ref, tout1_ref)

    def build_row_list():
        def spare_tail(ex, carry):
            def one(p, c):
                code_ref[p] = gt
                return c
            return lax.fori_loop(seg_ref[ex] + cnt_ref[ex], seg_ref[ex + 1], one, carry)

        lax.fori_loop(0, N_EXPERTS, spare_tail, 0)

        def chunk(i, carry):
            t0 = i * LIST_CHUNK
            for u in range(LIST_CHUNK):
                for k in range(TOP_K):
                    code_ref[row_ref[k * gt + t0 + u]] = k * 2 * gt + t0 + u
            return carry

        lax.fori_loop(0, gt // LIST_CHUNK, chunk, 0)

    e = pl.program_id(0)
    n_total = lax.shift_right_logical(seg_ref[N_EXPERTS], MB_LOG2)

    def gather(b, slot):
        r0 = b * MB
        for m in range(MB):
            t = code_ref[r0 + m] & (gt - 1)
            slab = hp_ref[pl.ds(pl.multiple_of(t * HP_ROWS, HP_ROWS), HP_ROWS), :]
            tin[slot][pl.ds(m, HP_ROWS, stride=S_IN), :] = slab

    def scatter(b, slot, live):
        r0 = b * MB
        for b0 in range(0, MB, RMW_BATCH):
            pending = []
            for m in range(b0, b0 + RMW_BATCH):
                code = code_ref[r0 + m]
                off = pl.multiple_of((code & (2 * gt - 1)) * OUT_ROWS, OUT_ROWS)
                upd = tout[slot][pl.ds(m, OUT_ROWS, stride=S_OUT), :] * wtab_ref[code]
                if live is not None:
                    upd = jnp.where(live, upd, 0.0)
                pending.append((off, o_ref[pl.ds(off, OUT_ROWS), :] + upd))
            for off, v in pending:
                o_ref[pl.ds(off, OUT_ROWS), :] = v

    def down(parity, w2_slot):
        y = _dot(act[parity][...], w2b_ref[w2_slot])
        for j in range(OUT_ROWS):
            tout[parity][pl.ds(j * S_OUT, MB), :] = y[:, j * LANE:(j + 1) * LANE]

    @pl.when(e == 0)
    def _():
        build_row_list()
        state_ref[0] = 0
        o_ref[...] = jnp.zeros_like(o_ref)
        act1_ref[...] = jnp.zeros_like(act1_ref)
        tout0_ref[...] = jnp.zeros_like(tout0_ref)
        w2b_ref[0] = jnp.zeros((D_EXPERT, D), BF16)
        gather(0, 0)

    def stages(b, parity, w2_slot):
        gather(jnp.minimum(b + 1, n_total - 1), 1 - parity)
        x = jnp.concatenate([tin[parity][pl.ds(j * S_IN, MB), :] for j in range(HP_ROWS)], axis=-1).astype(BF16)
        act[parity][...] = (_silu(_dot(x, w1b_ref[...])) * _dot(x, w3b_ref[...])).astype(BF16)
        down(1 - parity, w2_slot)
        scatter(jnp.maximum(b - 2, 0), parity, b >= 2)

    def expert(sub, carry):
        ex = e * EXPERTS_PER_STEP + sub
        first = lax.shift_right_logical(seg_ref[ex], MB_LOG2)
        n_blocks = lax.shift_right_logical(seg_ref[ex + 1], MB_LOG2) - first
        prev_slot = state_ref[0]
        this_slot = 1 - prev_slot
        w1b_ref[...] = w1_ref[sub].astype(BF16)
        w3b_ref[...] = w3_ref[sub].astype(BF16)

        @pl.when(n_blocks > 0)
        def _():
            w2b_ref[this_slot] = w2_ref[sub].astype(BF16)

        def block(i, c):
            b = first + i
            w2_slot = jnp.where(i == 0, prev_slot, this_slot)
            for parity in range(2):
                @pl.when(b & 1 == parity)
                def _(parity=parity):
                    stages(b, parity, w2_slot)
            return c

        carry = lax.fori_loop(0, n_blocks, block, carry)

        @pl.when(n_blocks > 0)
        def _():
            state_ref[0] = this_slot

        return carry

    lax.fori_loop(0, EXPERTS_PER_STEP, expert, 0)

    @pl.when(e == N_EXPERTS // EXPERTS_PER_STEP - 1)
    def _():
        last = n_total - 1
        for parity in range(2):
            @pl.when(last & 1 == parity)
            def _(parity=parity):
                scatter(jnp.maximum(last - 1, 0), 1 - parity, last >= 1)
                down(parity, state_ref[0])
                scatter(last, parity, None)


def _moe(hp, group, gt, seg, counts, rows, wtab, w):
    n_rows = TOP_K * gt + N_EXPERTS * MB
    grid_spec = pltpu.PrefetchScalarGridSpec(
        num_scalar_prefetch=4,
        grid=(N_EXPERTS // EXPERTS_PER_STEP,),
        in_specs=[pl.BlockSpec((gt * HP_ROWS, LANE), lambda e, *_: (group, 0), pipeline_mode=pl.Buffered(1)),
                  pl.BlockSpec((EXPERTS_PER_STEP, D, D_EXPERT), lambda e, *_: (e, 0, 0)),
                  pl.BlockSpec((EXPERTS_PER_STEP, D, D_EXPERT), lambda e, *_: (e, 0, 0)),
                  pl.BlockSpec((EXPERTS_PER_STEP, D_EXPERT, D), lambda e, *_: (e, 0, 0))],
        out_specs=pl.BlockSpec(((gt + 1) * OUT_ROWS, LANE), lambda e, *_: (0, 0), pipeline_mode=pl.Buffered(1)),
        scratch_shapes=[pltpu.VMEM((HP_ROWS * S_IN, LANE), F32), pltpu.VMEM((HP_ROWS * S_IN, LANE), F32),
                        pltpu.VMEM((MB, D_EXPERT), BF16), pltpu.VMEM((MB, D_EXPERT), BF16),
                        pltpu.VMEM((OUT_ROWS * S_OUT, LANE), F32), pltpu.VMEM((OUT_ROWS * S_OUT, LANE), F32),
                        pltpu.VMEM((D, D_EXPERT), BF16), pltpu.VMEM((D, D_EXPERT), BF16),
                        pltpu.VMEM((2, D_EXPERT, D), BF16),
                        pltpu.SMEM((n_rows,), I32), pltpu.SMEM((1,), I32)])
    return pl.pallas_call(
        functools.partial(_moe_kernel, gt),
        grid_spec=grid_spec,
        out_shape=jax.ShapeDtypeStruct(((gt + 1) * OUT_ROWS, LANE), F32),
        compiler_params=pltpu.CompilerParams(dimension_semantics=("arbitrary",), vmem_limit_bytes=VMEM_LIMIT),
        name="moe_experts",
    )(seg, counts, rows, wtab, hp, w["w_e1"], w["w_e3"], w["w_e2"])


def _combine_kernel(n_groups, tiles_per_group, *refs):
    r_refs = refs[:n_groups]
    base_ref, mod_ref, o_ref = refs[n_groups:]
    tile = pl.program_id(0) * pl.num_programs(1) + pl.program_id(1)
    g2 = mod_ref[0][5:6]
    n_tok = base_ref.shape[1]
    for g, r_ref in enumerate(r_refs):
        @pl.when(tile // tiles_per_group == g)
        def _(r_ref=r_ref):
            for s in range(OUT_ROWS):
                cols = slice(s * LANE, (s + 1) * LANE)
                o_ref[0, :, cols] = base_ref[0, :, cols] + g2[:, cols] * r_ref[pl.ds(s, n_tok, stride=OUT_ROWS), :]


def _combine(routed, base, modsel, per_batch, gt):
    B, S, _ = base.shape
    nt = S // TM
    tiles_per_group = gt // TM
    mod_idx = (lambda b, i: (b, 0, 0)) if per_batch else (lambda b, i: (0, 0, 0))

    def r_spec(g):
        def idx(b, i):
            tile = b * nt + i
            return (jnp.where(tile // tiles_per_group == g, tile % tiles_per_group, 0), 0)
        return pl.BlockSpec((TM * OUT_ROWS, LANE), idx)

    return pl.pallas_call(
        functools.partial(_combine_kernel, len(routed), tiles_per_group),
        grid=(B, nt),
        in_specs=[r_spec(g) for g in range(len(routed))] + [
            pl.BlockSpec((1, TM, D), lambda b, i: (b, i, 0)),
            pl.BlockSpec((1, N_MOD, D), mod_idx)],
        out_specs=pl.BlockSpec((1, TM, D), lambda b, i: (b, i, 0)),
        out_shape=jax.ShapeDtypeStruct((B, S, D), F32),
        compiler_params=pltpu.CompilerParams(dimension_semantics=("arbitrary", "arbitrary"),
                                             vmem_limit_bytes=VMEM_LIMIT),
        name="moe_combine",
    )(*routed, base, modsel)


def _trunk_pass(x, modsel, per_batch, w, rope_tabs, cache):
    B, S, _ = x.shape
    qa, ka, va, qb, ckv, krp, ga, gb = _project(x, modsel, per_batch, w, rope_tabs)
    if cache is None:
        ka_all, va_all, ckv_all, krp_all = ka, va, ckv, krp
    else:
        ck, cv, cckv, ckr = cache
        ka_all = jnp.concatenate([ck, ka], axis=2)
        va_all = jnp.concatenate([cv, va], axis=2)
        ckv_all = jnp.concatenate([cckv, ckv], axis=1)
        krp_all = jnp.concatenate([_pad64(ckr), krp], axis=1)
    va_ones = jnp.concatenate([va_all.astype(BF16), jnp.ones(va_all.shape, BF16)], axis=-1)
    oa = _attention(qa, ka_all.astype(BF16), va_ones, "attn_gqa")
    ob = _attention_mla(qb, ckv_all, krp_all, w)
    T = B * S
    gt = min(GROUP_TOKENS, T)
    assert T % gt == 0 and gt % TM == 0 and gt & (gt - 1) == 0 and gt >= MB and gt % LIST_CHUNK == 0
    base, hp, idx, wk, rank = _post(oa, ob, ga, gb, x, modsel, per_batch, w, gt)
    seg, counts, rows, wtab = _dispatch_rows(idx, wk, rank, T // gt)
    routed = [_moe(hp, g, gt, seg[g], counts[g], rows[g], wtab[g], w) for g in range(T // gt)]
    y = _combine(routed, base, modsel, per_batch, gt)
    return y, (ka, va, ckv, _unpad64(krp))


def _rope_tables(n_tokens):
    t = jnp.arange(n_tokens)
    row = (t // GRID_W).astype(F32)[:, None]
    col = (t % GRID_W).astype(F32)[:, None]

    def tabs(rot_dim):
        axis_dim = rot_dim // 2
        inv = ROPE_THETA ** (-jnp.arange(0, axis_dim, 2, dtype=F32) / axis_dim)
        ang = jnp.concatenate([row * inv, col * inv], axis=-1)
        return jnp.cos(ang), jnp.sin(ang)

    cos_a, sin_a = tabs(HD_A)
    cos_b, sin_b = tabs(ROPE_B)
    return (jnp.concatenate([cos_a, cos_a], -1), jnp.concatenate([-sin_a, sin_a], -1),
            _pad64(jnp.concatenate([cos_b, cos_b], -1)), _pad64(jnp.concatenate([-sin_b, sin_b], -1)))


def _layer_weights(l, w_in, w_qb, w_kvb, w_router, router_bias, named):
    w = {k: v[l] for k, v in named.items()}
    wi = w_in[l]
    n_front = C_KR + ROPE_B
    w["w_in"] = jnp.pad(wi[:, :n_front].astype(BF16), ((0, 0), (0, C_GA - n_front)))
    w["w_gate"] = wi[:, n_front:].astype(BF16)
    wq = w_qb[l].reshape(Q_LORA, H_B, NOPE_B + ROPE_B)
    w["w_qn"] = wq[:, :, :NOPE_B].reshape(Q_LORA, H_B * LANE).astype(BF16)
    w["w_qr"] = _pad64(wq[:, :, NOPE_B:]).reshape(Q_LORA, H_B * LANE).astype(BF16)
    wkv = w_kvb[l].reshape(KV_LORA, H_B, NOPE_B + VD_B)
    w["w_kvk"] = wkv[:, :, :NOPE_B].reshape(KV_LORA, H_B * LANE).astype(BF16)
    w["w_kvv"] = wkv[:, :, NOPE_B:].reshape(KV_LORA, H_B * LANE).astype(BF16)
    wr = jnp.pad(w_router[l], ((0, 0), (0, LANE - N_EXPERTS)))
    wr_hi = wr.astype(BF16)
    w["w_router"] = jnp.concatenate([wr_hi, (wr - wr_hi.astype(F32)).astype(BF16)], axis=1)
    w["router_bias"] = router_bias[l].reshape(N_EXPERTS, 1)
    for k in ("norm1_g", "norm2_g", "gqa_qn", "gqa_kn", "mla_qa_g", "mla_kva_g", "mla_qn_nope", "mla_kn_nope"):
        w[k] = w[k].reshape(1, -1)
    w["mla_qn_rope"] = _pad64(w["mla_qn_rope"]).reshape(1, LANE)
    w["mla_kn_rope"] = jnp.pad(w["mla_kn_rope"], (0, LANE - ROPE_B)).reshape(1, LANE)
    for k in ("w_oa", "w_ob", "w_out", "w_s1", "w_s3", "w_s2"):
        w[k] = w[k].astype(BF16)
    return w


def kernel(x_prompt, x_sample, c, cache_gqa_k, cache_gqa_v, cache_mla_ckv, cache_mla_krope, c_ctx, w_mod, b_mod, norm1_g, norm2_g, w_in, gqa_qn, gqa_kn, mla_qa_g, mla_kva_g, w_qb, w_kvb, mla_qn_nope, mla_qn_rope, mla_kn_nope, mla_kn_rope, w_oa, w_ob, w_out, w_router, router_bias, w_e1, w_e3, w_e2, w_s1, w_s3, w_s2):
    depth = w_mod.shape[0]
    n_dec = x_sample.shape[0]
    rope_tabs = _rope_tables(x_sample.shape[1])
    cond8 = jnp.concatenate([c_ctx[None, :], c, jnp.zeros((8 - 1 - n_dec, D), F32)], axis=0)
    named = dict(norm1_g=norm1_g, norm2_g=norm2_g, gqa_qn=gqa_qn, gqa_kn=gqa_kn, mla_qa_g=mla_qa_g,
                 mla_kva_g=mla_kva_g, mla_qn_nope=mla_qn_nope, mla_qn_rope=mla_qn_rope, mla_kn_nope=mla_kn_nope,
                 mla_kn_rope=mla_kn_rope, w_oa=w_oa, w_ob=w_ob, w_out=w_out, w_e1=w_e1, w_e3=w_e3, w_e2=w_e2,
                 w_s1=w_s1, w_s3=w_s3, w_s2=w_s2)
    xp, xs = x_prompt, x_sample
    ks, vs, ckvs, krs = [], [], [], []
    for l in range(depth):
        w = _layer_weights(l, w_in, w_qb, w_kvb, w_router, router_bias, named)
        mod = _modulation(cond8, w_mod[l], b_mod[l].reshape(1, -1))
        mod_ctx = mod[0:1].reshape(1, N_MOD, D)
        mod_dec = mod[1:1 + n_dec].reshape(n_dec, N_MOD, D)
        xp, ctx = _trunk_pass(xp, mod_ctx, False, w, None, None)
        ks.append(ctx[0])
        vs.append(ctx[1])
        ckvs.append(ctx[2])
        krs.append(ctx[3])
        cache = (cache_gqa_k[:, l], cache_gqa_v[:, l], cache_mla_ckv[:, l], cache_mla_krope[:, l])
        xs, _ = _trunk_pass(xs, mod_dec, True, w, rope_tabs, cache)
    return (xp, xs, jnp.stack(ks, axis=1), jnp.stack(vs, axis=1), jnp.stack(ckvs, axis=1), jnp.stack(krs, axis=1))
```

```python
import functools
import math

import jax
import jax.numpy as jnp
import numpy as np
from jax import lax
from jax.experimental import pallas as pl
from jax.experimental.pallas import tpu as pltpu

F32 = jnp.float32
BF16 = jnp.bfloat16
I32 = jnp.int32
HIGHEST = lax.Precision.HIGHEST

D = 1024
GRID_W = 64
RMS_EPS = 1e-6
ROPE_THETA = 10000.0
N_MOD = 6
H_A, KV_A, HD_A = 8, 2, 128
H_B, Q_LORA, KV_LORA, NOPE_B, ROPE_B, VD_B = 8, 256, 128, 128, 64, 128
N_EXPERTS, TOP_K, N_GROUPS, TOPK_GROUPS = 64, 8, 8, 4
GROUP_SIZE = N_EXPERTS // N_GROUPS
D_EXPERT = 256
ROUTED_SCALE = 2.5
QSCALE_A = HD_A ** -0.5 * math.log2(math.e)
QSCALE_B = (NOPE_B + ROPE_B) ** -0.5 * math.log2(math.e)

LANE = 128
C_QA, C_KA, C_VA, C_CQ, C_CKV, C_KR, C_GA, C_GB, C_END = 0, 1024, 1280, 1536, 1792, 1920, 2048, 3072, 4096
VMEM_LIMIT = 52 * 1024 * 1024

TM = 256
TQ = 256
TL = 256
GROUP_TOKENS = 4096
MB, MB_LOG2 = 128, 7
HP_ROWS = D // LANE
OUT_ROWS = D // LANE
S_IN = MB + 4
S_OUT = MB + 4
RMW_BATCH = 8
LIST_CHUNK = 16
EXPERTS_PER_STEP = 2


def _dot(a, b):
    return jnp.dot(a, b, preferred_element_type=F32)


def _rms(t, n_valid):
    ms = jnp.sum(t * t, axis=-1, keepdims=True) * (1.0 / n_valid)
    return t * lax.rsqrt(ms + RMS_EPS)


def _silu(t):
    return t * jax.nn.sigmoid(t)


def _pad64(a):
    z = jnp.zeros(a.shape[:-1] + (32,), a.dtype)
    return jnp.concatenate([a[..., :32], z, a[..., 32:], z], axis=-1)


def _unpad64(a):
    return jnp.concatenate([a[..., :32], a[..., 64:96]], axis=-1)


def _const_spec(shape):
    nd = len(shape)
    return pl.BlockSpec(shape, lambda *_: (0,) * nd)


def _mod_kernel(c_ref, w_ref, b_ref, o_ref):
    s = _silu(c_ref[...])
    o_ref[...] = jnp.dot(s, w_ref[...], preferred_element_type=F32, precision=HIGHEST) + b_ref[...]


def _modulation(cond8, w_mod, b_mod):
    tn = 1024
    n = w_mod.shape[1]
    return pl.pallas_call(
        _mod_kernel,
        grid=(n // tn,),
        in_specs=[pl.BlockSpec((8, D), lambda j: (0, 0)),
                  pl.BlockSpec((D, tn), lambda j: (0, j)),
                  pl.BlockSpec((1, tn), lambda j: (0, j))],
        out_specs=pl.BlockSpec((8, tn), lambda j: (0, j)),
        out_shape=jax.ShapeDtypeStruct((8, n), F32),
        compiler_params=pltpu.CompilerParams(dimension_semantics=("arbitrary",), vmem_limit_bytes=VMEM_LIMIT),
        name="adaln_mod",
    )(cond8, w_mod, b_mod)


def _proj_kernel(use_rope, *refs):
    (x_ref, mod_ref, g1_ref, win_ref, wgate_ref, wqn_ref, wqr_ref,
     gqn_ref, gkn_ref, gcq_ref, gckv_ref, gnope_ref, gqr_ref, gkr_ref) = refs[:14]
    n_in = 18 if use_rope else 14
    if use_rope:
        ca_ref, sa_ref, cb_ref, sb_ref = refs[14:18]
    qa_ref, ka_ref, va_ref, qb_ref, ckv_ref, kr_ref, ga_ref, gb_ref = refs[n_in:n_in + 8]
    zbuf = refs[n_in + 8:n_in + 10]
    qbuf = refs[n_in + 10:n_in + 12]
    step = pl.program_id(0)

    def rope(t, c_ref, s_ref):
        return t * c_ref[...] + pltpu.roll(t, LANE // 2, 1) * s_ref[...]

    rope_a = (lambda t: rope(t, ca_ref, sa_ref)) if use_rope else (lambda t: t)
    rope_b = (lambda t: rope(t, cb_ref, sb_ref)) if use_rope else (lambda t: t)

    @pl.when(step == 0)
    def _():
        zbuf[1][...] = jnp.zeros_like(zbuf[1])
        qbuf[1][...] = jnp.zeros_like(qbuf[1])

    def matmuls(z_ref, q_ref):
        x = x_ref[0]
        m = mod_ref[0]
        sh1, sc1 = m[0:1], m[1:2]
        hb = ((_rms(x, D) * g1_ref[...]) * (1.0 + sc1) + sh1).astype(BF16)
        for lo, hi in ((C_QA, C_KA), (C_KA, C_CQ), (C_CKV, C_GA)):
            z_ref[:, lo:hi] = _dot(hb, win_ref[:, lo:hi])
        for lo, hi in ((C_GA, C_GB), (C_GB, C_END)):
            z_ref[:, lo:hi] = _dot(hb, wgate_ref[:, lo - C_GA:hi - C_GA])
        cq = (_rms(_dot(hb, win_ref[:, C_CQ:C_CKV]), Q_LORA) * gcq_ref[...]).astype(BF16)
        q_ref[:, 0:H_B * LANE] = _dot(cq, wqn_ref[...])
        q_ref[:, H_B * LANE:2 * H_B * LANE] = _dot(cq, wqr_ref[...])

    def epilogue(z_ref, q_ref):
        for hh in range(H_A):
            t = _rms(z_ref[:, C_QA + hh * HD_A:C_QA + (hh + 1) * HD_A], HD_A) * gqn_ref[...]
            qa_ref[0, hh] = (rope_a(t) * QSCALE_A).astype(BF16)
        for j in range(KV_A):
            t = _rms(z_ref[:, C_KA + j * HD_A:C_KA + (j + 1) * HD_A], HD_A) * gkn_ref[...]
            ka_ref[0, j] = rope_a(t)
            va_ref[0, j] = z_ref[:, C_VA + j * HD_A:C_VA + (j + 1) * HD_A]
        for hh in range(H_B):
            t = _rms(q_ref[:, hh * LANE:(hh + 1) * LANE], NOPE_B) * gnope_ref[...]
            qb_ref[0, hh, :, 0:LANE] = (t * QSCALE_B).astype(BF16)
            t = _rms(q_ref[:, (H_B + hh) * LANE:(H_B + hh + 1) * LANE], ROPE_B) * gqr_ref[...]
            qb_ref[0, hh, :, LANE:2 * LANE] = (rope_b(t) * QSCALE_B).astype(BF16)
        ckv_ref[0] = _rms(z_ref[:, C_CKV:C_KR], KV_LORA) * gckv_ref[...]
        t = _rms(z_ref[:, C_KR:C_GA], ROPE_B) * gkr_ref[...]
        lane = lax.broadcasted_iota(I32, t.shape, 1)
        second = jnp.where((lane >= LANE // 2) & (lane < 3 * LANE // 4), pltpu.roll(t, LANE // 4, 1), 0.0)
        kr_ref[0] = rope_b(jnp.where(lane < LANE // 4, t, second))
        ga_ref[0] = jax.nn.sigmoid(z_ref[:, C_GA:C_GB])
        gb_ref[0] = jax.nn.sigmoid(z_ref[:, C_GB:C_END])

    for parity in range(2):
        @pl.when(step % 2 == parity)
        def _(parity=parity):
            matmuls(zbuf[parity], qbuf[parity])
            epilogue(zbuf[1 - parity], qbuf[1 - parity])


def _project(x, modsel, per_batch, w, rope_tabs):
    B, S, _ = x.shape
    use_rope = rope_tabs is not None
    nt = S // TM
    n_tiles = B * nt
    cur = lambda s: jnp.minimum(s, n_tiles - 1)
    prev = lambda s: jnp.maximum(s - 1, 0)
    mod_idx = (lambda s: (cur(s) // nt, 0, 0)) if per_batch else (lambda s: (0, 0, 0))
    in_specs = [pl.BlockSpec((1, TM, D), lambda s: (cur(s) // nt, cur(s) % nt, 0)),
                pl.BlockSpec((1, N_MOD, D), mod_idx),
                _const_spec((1, D)), _const_spec((D, C_GA)), _const_spec((D, C_END - C_GA)),
                _const_spec((Q_LORA, H_B * LANE)), _const_spec((Q_LORA, H_B * LANE)),
                _const_spec((1, LANE)), _const_spec((1, LANE)), _const_spec((1, Q_LORA)), _const_spec((1, LANE)),
                _const_spec((1, LANE)), _const_spec((1, LANE)), _const_spec((1, LANE))]
    args = [x, modsel, w["norm1_g"], w["w_in"], w["w_gate"], w["w_qn"], w["w_qr"], w["gqa_qn"], w["gqa_kn"], w["mla_qa_g"],
            w["mla_kva_g"], w["mla_qn_nope"], w["mla_qn_rope"], w["mla_kn_rope"]]
    if use_rope:
        in_specs += [pl.BlockSpec((TM, LANE), lambda s: (prev(s) % nt, 0))] * 4
        args += list(rope_tabs)
    tok4 = lambda h, d: pl.BlockSpec((1, h, TM, d), lambda s: (prev(s) // nt, 0, prev(s) % nt, 0))
    tok3 = lambda d: pl.BlockSpec((1, TM, d), lambda s: (prev(s) // nt, prev(s) % nt, 0))
    out_specs = [tok4(H_A, HD_A), tok4(KV_A, HD_A), tok4(KV_A, HD_A), tok4(H_B, 2 * LANE),
                 tok3(KV_LORA), tok3(LANE), tok3(D), tok3(D)]
    out_shape = [jax.ShapeDtypeStruct((B, H_A, S, HD_A), BF16),
                 jax.ShapeDtypeStruct((B, KV_A, S, HD_A), F32),
                 jax.ShapeDtypeStruct((B, KV_A, S, HD_A), F32),
                 jax.ShapeDtypeStruct((B, H_B, S, 2 * LANE), BF16),
                 jax.ShapeDtypeStruct((B, S, KV_LORA), F32),
                 jax.ShapeDtypeStruct((B, S, LANE), F32),
                 jax.ShapeDtypeStruct((B, S, D), F32),
                 jax.ShapeDtypeStruct((B, S, D), F32)]
    return pl.pallas_call(
        functools.partial(_proj_kernel, use_rope),
        grid=(n_tiles + 1,),
        in_specs=in_specs, out_specs=out_specs, out_shape=out_shape,
        scratch_shapes=[pltpu.VMEM((TM, C_END), F32), pltpu.VMEM((TM, C_END), F32),
                        pltpu.VMEM((TM, 2 * H_B * LANE), F32), pltpu.VMEM((TM, 2 * H_B * LANE), F32)],
        compiler_params=pltpu.CompilerParams(dimension_semantics=("arbitrary",), vmem_limit_bytes=VMEM_LIMIT),
        name="proj_rope" if use_rope else "proj",
    )(*args)


def _attend(n_heads, q_ref, key, value, o_ref):
    for hh in range(n_heads):
        v = value(hh)
        dv = v.shape[-1] // 2
        s = lax.dot_general(q_ref[0, hh], key(hh), (((1,), (1,)), ((), ())), preferred_element_type=F32)
        p = jnp.exp2(s - jnp.max(s, axis=-1, keepdims=True))
        o = _dot(p.astype(BF16), v)
        o_ref[0, :, hh * dv:(hh + 1) * dv] = (o[:, :dv] / o[:, dv:]).astype(BF16)


def _attn_kernel(n_heads, group, q_ref, k_ref, v_ref, o_ref):
    _attend(n_heads, q_ref, lambda hh: k_ref[0, hh // group], lambda hh: v_ref[0, hh // group], o_ref)


def _attn_mla_kernel(q_ref, ckv_ref, kr_ref, wk_ref, wv_ref, gk_ref, o_ref, kb_ref, vb_ref):
    @pl.when(pl.program_id(1) == 0)
    def _():
        def chunk(c, carry):
            rows = pl.ds(pl.multiple_of(c * TL, TL), TL)
            cb = ckv_ref[0, rows, :].astype(BF16)
            k = _dot(cb, wk_ref[...])
            v = _dot(cb, wv_ref[...])
            krb = kr_ref[0, rows, :].astype(BF16)
            for hh in range(H_B):
                t = _rms(k[:, hh * LANE:(hh + 1) * LANE], NOPE_B) * gk_ref[...]
                kb_ref[hh, rows, 0:LANE] = t.astype(BF16)
                kb_ref[hh, rows, LANE:2 * LANE] = krb
                vb_ref[hh, rows, 0:VD_B] = v[:, hh * LANE:(hh + 1) * LANE].astype(BF16)
                vb_ref[hh, rows, VD_B:2 * VD_B] = jnp.ones((TL, VD_B), BF16)
            return carry

        lax.fori_loop(0, ckv_ref.shape[1] // TL, chunk, 0)

    _attend(H_B, q_ref, lambda hh: kb_ref[hh], lambda hh: vb_ref[hh], o_ref)


def _attention_mla(q, ckv_all, krp_all, w):
    B, H, S, dk = q.shape
    L = ckv_all.shape[1]
    return pl.pallas_call(
        _attn_mla_kernel,
        grid=(B, S // TQ),
        in_specs=[pl.BlockSpec((1, H, TQ, dk), lambda b, i: (b, 0, i, 0)),
                  pl.BlockSpec((1, L, KV_LORA), lambda b, i: (b, 0, 0)),
                  pl.BlockSpec((1, L, LANE), lambda b, i: (b, 0, 0)),
                  _const_spec((KV_LORA, H_B * LANE)), _const_spec((KV_LORA, H_B * LANE)), _const_spec((1, LANE))],
        out_specs=pl.BlockSpec((1, TQ, H * VD_B), lambda b, i: (b, i, 0)),
        out_shape=jax.ShapeDtypeStruct((B, S, H * VD_B), BF16),
        scratch_shapes=[pltpu.VMEM((H_B, L, 2 * LANE), BF16), pltpu.VMEM((H_B, L, 2 * VD_B), BF16)],
        compiler_params=pltpu.CompilerParams(dimension_semantics=("arbitrary", "arbitrary"),
                                             vmem_limit_bytes=VMEM_LIMIT),
        name="attn_mla",
    )(q, ckv_all, krp_all, w["w_kvk"], w["w_kvv"], w["mla_kn_nope"])


def _attention(q, k, v, name):
    B, H, S, dk = q.shape
    _, Hk, L, dv2 = v.shape
    dv = dv2 // 2
    return pl.pallas_call(
        functools.partial(_attn_kernel, H, H // Hk),
        grid=(B, S // TQ),
        in_specs=[pl.BlockSpec((1, H, TQ, dk), lambda b, i: (b, 0, i, 0)),
                  pl.BlockSpec((1, Hk, L, dk), lambda b, i: (b, 0, 0, 0)),
                  pl.BlockSpec((1, Hk, L, dv2), lambda b, i: (b, 0, 0, 0))],
        out_specs=pl.BlockSpec((1, TQ, H * dv), lambda b, i: (b, i, 0)),
        out_shape=jax.ShapeDtypeStruct((B, S, H * dv), BF16),
        compiler_params=pltpu.CompilerParams(dimension_semantics=("arbitrary", "arbitrary"),
                                             vmem_limit_bytes=VMEM_LIMIT),
        name=name,
    )(q, k, v)


def _route(bi, sc, carry):
    n_tok = bi.shape[1]
    iota8 = lax.broadcasted_iota(I32, (GROUP_SIZE, n_tok), 0)
    neg = -jnp.inf
    grp = [bi[j * GROUP_SIZE:(j + 1) * GROUP_SIZE] for j in range(N_GROUPS)]
    gscore = []
    for g in grp:
        m1 = jnp.max(g, axis=0, keepdims=True)
        first = jnp.min(jnp.where(g == m1, iota8, GROUP_SIZE), axis=0, keepdims=True)
        m2 = jnp.max(jnp.where(iota8 == first, neg, g), axis=0, keepdims=True)
        gscore.append(m1 + m2)
    masked = []
    for j in range(N_GROUPS):
        cnt = jnp.zeros((1, n_tok), I32)
        for j2 in range(N_GROUPS):
            if j2 == j:
                continue
            beats = (gscore[j2] > gscore[j]) if j2 > j else (gscore[j2] >= gscore[j])
            cnt = cnt + beats.astype(I32)
        keep = jnp.broadcast_to(cnt, grp[j].shape) < TOPK_GROUPS
        masked.append(jnp.where(keep, grp[j], neg))
    eid = [(iota8 + j * GROUP_SIZE).astype(F32) for j in range(N_GROUPS)]
    scg = [sc[j * GROUP_SIZE:(j + 1) * GROUP_SIZE] for j in range(N_GROUPS)]
    work = list(masked)
    kept = [jnp.zeros((GROUP_SIZE, n_tok), F32) for _ in range(N_GROUPS)]
    idx_rows = jnp.zeros((TOP_K, n_tok), F32)
    score_rows = jnp.zeros((TOP_K, n_tok), F32)
    for k in range(TOP_K):
        top = work[0]
        for j in range(1, N_GROUPS):
            top = jnp.maximum(top, work[j])
        top = jnp.max(top, axis=0, keepdims=True)
        cand = jnp.where(work[0] == top, eid[0], float(N_EXPERTS))
        for j in range(1, N_GROUPS):
            cand = jnp.minimum(cand, jnp.where(work[j] == top, eid[j], float(N_EXPERTS)))
        chosen = jnp.min(cand, axis=0, keepdims=True)
        score = jnp.zeros((GROUP_SIZE, n_tok), F32)
        for j in range(N_GROUPS):
            hit = eid[j] == chosen
            score = score + jnp.where(hit, scg[j], 0.0)
            kept[j] = kept[j] + jnp.where(hit, 1.0, 0.0)
            work[j] = jnp.where(hit, neg, work[j])
        idx_rows = jnp.where(iota8 == k, chosen, idx_rows)
        score_rows = jnp.where(iota8 == k, jnp.sum(score, axis=0, keepdims=True), score_rows)
    w_rows = score_rows / jnp.sum(score_rows, axis=0, keepdims=True) * ROUTED_SCALE
    kept = jnp.concatenate(kept, axis=0)
    earlier = (lax.broadcasted_iota(I32, (n_tok, n_tok), 0) < lax.broadcasted_iota(I32, (n_tok, n_tok), 1))
    rank = _dot(kept.astype(BF16), earlier.astype(F32).astype(BF16)) + carry
    new_carry = carry + jnp.sum(kept, axis=1, keepdims=True)
    rank_rows = jnp.zeros((TOP_K, n_tok), F32)
    for k in range(TOP_K):
        ra = jnp.zeros((GROUP_SIZE, n_tok), F32)
        for j in range(N_GROUPS):
            ra = ra + jnp.where(eid[j] == idx_rows[k:k + 1], rank[j * GROUP_SIZE:(j + 1) * GROUP_SIZE], 0.0)
        rank_rows = jnp.where(iota8 == k, jnp.sum(ra, axis=0, keepdims=True), rank_rows)
    return idx_rows, w_rows, rank_rows, new_carry


def _post_kernel(tiles_per_group, oa_ref, ob_ref, ga_ref, gb_ref, x_ref, mod_ref, g2n_ref, woa_ref, wob_ref,
                 wout_ref, wr_ref, rb_ref, ws1_ref, ws3_ref, ws2_ref, base_ref, hp_ref, idx_ref, wk_ref, rank_ref,
                 carry_ref):
    tile = pl.program_id(0) * pl.num_programs(1) + pl.program_id(1)

    @pl.when(tile % tiles_per_group == 0)
    def _():
        carry_ref[...] = jnp.zeros_like(carry_ref)

    m = mod_ref[0]
    g1, sh2, sc2, g2 = m[2:3], m[3:4], m[4:5], m[5:6]
    merged = ga_ref[0] * _dot(oa_ref[0], woa_ref[...]) + gb_ref[0] * _dot(ob_ref[0], wob_ref[...])
    x1 = x_ref[0] + g1 * _dot(merged.astype(BF16), wout_ref[...])
    h2 = (_rms(x1, D) * g2n_ref[...]) * (1.0 + sc2) + sh2
    h2b = h2.astype(BF16)
    act = _silu(_dot(h2b, ws1_ref[...])) * _dot(h2b, ws3_ref[...])
    base_ref[0] = x1 + g2 * _dot(act.astype(BF16), ws2_ref[...])
    n_tok = h2.shape[0]
    for j in range(HP_ROWS):
        hp_ref[pl.ds(j, n_tok, stride=HP_ROWS), :] = h2[:, j * LANE:(j + 1) * LANE]
    h2lo = (h2 - h2b.astype(F32)).astype(BF16)
    hw = _dot(h2b, wr_ref[...])
    logits = hw[:, 0:LANE] + (hw[:, LANE:2 * LANE] + _dot(h2lo, wr_ref[:, 0:LANE]))
    sc = jax.nn.sigmoid(logits.T[0:N_EXPERTS])
    idx_rows, w_rows, rank_rows, carry = _route(sc + rb_ref[...], sc, carry_ref[...])
    carry_ref[...] = carry
    idx_ref[...] = idx_rows.astype(I32)
    wk_ref[...] = w_rows
    rank_ref[...] = rank_rows.astype(I32)


def _post(oa, ob, ga, gb, x, modsel, per_batch, w, gt):
    B, S, _ = x.shape
    nt = S // TM
    mod_idx = (lambda b, i: (b, 0, 0)) if per_batch else (lambda b, i: (0, 0, 0))
    tok = lambda d: pl.BlockSpec((1, TM, d), lambda b, i: (b, i, 0))
    slot = lambda: pl.BlockSpec((TOP_K, TM), lambda b, i: (0, b * nt + i))
    return pl.pallas_call(
        functools.partial(_post_kernel, gt // TM),
        grid=(B, nt),
        in_specs=[tok(D), tok(D), tok(D), tok(D), tok(D), pl.BlockSpec((1, N_MOD, D), mod_idx),
                  _const_spec((1, D)), _const_spec((D, D)), _const_spec((D, D)), _const_spec((D, D)),
                  _const_spec((D, 2 * LANE)), _const_spec((N_EXPERTS, 1)),
                  _const_spec((D, D_EXPERT)), _const_spec((D, D_EXPERT)), _const_spec((D_EXPERT, D))],
        out_specs=[tok(D),
                   pl.BlockSpec((TM * HP_ROWS, LANE), lambda b, i: (b * nt + i, 0)),
                   slot(), slot(), slot()],
        out_shape=[jax.ShapeDtypeStruct((B, S, D), F32),
                   jax.ShapeDtypeStruct((B * S * HP_ROWS, LANE), F32),
                   jax.ShapeDtypeStruct((TOP_K, B * S), I32),
                   jax.ShapeDtypeStruct((TOP_K, B * S), F32),
                   jax.ShapeDtypeStruct((TOP_K, B * S), I32)],
        scratch_shapes=[pltpu.VMEM((N_EXPERTS, 1), F32)],
        compiler_params=pltpu.CompilerParams(dimension_semantics=("arbitrary", "arbitrary"),
                                             vmem_limit_bytes=VMEM_LIMIT),
        name="post",
    )(oa, ob, ga, gb, x, modsel, w["norm2_g"], w["w_oa"], w["w_ob"], w["w_out"], w["w_router"],
      w["router_bias"], w["w_s1"], w["w_s3"], w["w_s2"])


def _dispatch_rows(idx, wk, rank, n_groups):
    T = idx.shape[1]
    gt = T // n_groups
    experts = jnp.arange(N_EXPERTS, dtype=I32)
    regroup = lambda a: a.reshape(TOP_K, n_groups, gt).transpose(1, 0, 2)
    eid = regroup(idx)
    onehot = (eid[:, None] == experts[None, :, None, None]).astype(I32)
    counts = jnp.sum(onehot, axis=(2, 3))
    padded = (counts + (MB - 1)) // MB * MB
    bounds = jnp.concatenate([jnp.zeros((n_groups, 1), I32), jnp.cumsum(padded, axis=1)], axis=1)
    row = jnp.sum(onehot * bounds[:, :N_EXPERTS, None, None], axis=1) + regroup(rank)
    wtab = jnp.pad(regroup(wk), ((0, 0), (0, 0), (0, gt)))
    return bounds, counts, row.reshape(n_groups, TOP_K * gt), wtab.reshape(n_groups, TOP_K * 2 * gt)


def _moe_kernel(gt, seg_ref, cnt_ref, row_ref, wtab_ref, hp_ref, w1_ref, w3_ref, w2_ref, o_ref,
                tin0_ref, tin1_ref, act0_ref, act1_ref, tout0_ref, tout1_ref, w1b_ref, w3b_ref, w2b_ref,
                code_ref, state_ref):
    tin = (tin0_ref, tin1_ref)
    act = (act0_ref, act1_ref)
    tout = (tout0_ref, tout1_ref)

    def build_row_list():
        def spare_tail(ex, carry):
            def one(p, c):
                code_ref[p] = gt
                return c
            return lax.fori_loop(seg_ref[ex] + cnt_ref[ex], seg_ref[ex + 1], one, carry)

        lax.fori_loop(0, N_EXPERTS, spare_tail, 0)

        def chunk(i, carry):
            t0 = i * LIST_CHUNK
            for u in range(LIST_CHUNK):
                for k in range(TOP_K):
                    code_ref[row_ref[k * gt + t0 + u]] = k * 2 * gt + t0 + u
            return carry

        lax.fori_loop(0, gt // LIST_CHUNK, chunk, 0)

    e = pl.program_id(0)
    n_total = lax.shift_right_logical(seg_ref[N_EXPERTS], MB_LOG2)

    def gather(b, slot):
        r0 = b * MB
        for m in range(MB):
            t = code_ref[r0 + m] & (gt - 1)
            slab = hp_ref[pl.ds(pl.multiple_of(t * HP_ROWS, HP_ROWS), HP_ROWS), :]
            tin[slot][pl.ds(m, HP_ROWS, stride=S_IN), :] = slab

    def scatter(b, slot, live):
        r0 = b * MB
        for b0 in range(0, MB, RMW_BATCH):
            pending = []
            for m in range(b0, b0 + RMW_BATCH):
                code = code_ref[r0 + m]
                off = pl.multiple_of((code & (2 * gt - 1)) * OUT_ROWS, OUT_ROWS)
                upd = tout[slot][pl.ds(m, OUT_ROWS, stride=S_OUT), :] * wtab_ref[code]
                if live is not None:
                    upd = jnp.where(live, upd, 0.0)
                pending.append((off, o_ref[pl.ds(off, OUT_ROWS), :] + upd))
            for off, v in pending:
                o_ref[pl.ds(off, OUT_ROWS), :] = v

    def down(parity, w2_slot):
        y = _dot(act[parity][...], w2b_ref[w2_slot])
        for j in range(OUT_ROWS):
            tout[parity][pl.ds(j * S_OUT, MB), :] = y[:, j * LANE:(j + 1) * LANE]

    @pl.when(e == 0)
    def _():
        build_row_list()
        state_ref[0] = 0
        o_ref[...] = jnp.zeros_like(o_ref)
        act1_ref[...] = jnp.zeros_like(act1_ref)
        tout0_ref[...] = jnp.zeros_like(tout0_ref)
        w2b_ref[0] = jnp.zeros((D_EXPERT, D), BF16)
        gather(0, 0)

    def stages(b, parity, w2_slot):
        gather(jnp.minimum(b + 1, n_total - 1), 1 - parity)
        x = jnp.concatenate([tin[parity][pl.ds(j * S_IN, MB), :] for j in range(HP_ROWS)], axis=-1).astype(BF16)
        act[parity][...] = (_silu(_dot(x, w1b_ref[...])) * _dot(x, w3b_ref[...])).astype(BF16)
        down(1 - parity, w2_slot)
        scatter(jnp.maximum(b - 2, 0), parity, b >= 2)

    def expert(sub, carry):
        ex = e * EXPERTS_PER_STEP + sub
        first = lax.shift_right_logical(seg_ref[ex], MB_LOG2)
        n_blocks = lax.shift_right_logical(seg_ref[ex + 1], MB_LOG2) - first
        prev_slot = state_ref[0]
        this_slot = 1 - prev_slot
        w1b_ref[...] = w1_ref[sub].astype(BF16)
        w3b_ref[...] = w3_ref[sub].astype(BF16)

        @pl.when(n_blocks > 0)
        def _():
            w2b_ref[this_slot] = w2_ref[sub].astype(BF16)

        def block(i, c):
            b = first + i
            w2_slot = jnp.where(i == 0, prev_slot, this_slot)
            for parity in range(2):
                @pl.when(b & 1 == parity)
                def _(parity=parity):
                    stages(b, parity, w2_slot)
            return c

        carry = lax.fori_loop(0, n_blocks, block, carry)

        @pl.when(n_blocks > 0)
        def _():
            state_ref[0] = this_slot

        return carry

    lax.fori_loop(0, EXPERTS_PER_STEP, expert, 0)

    @pl.when(e == N_EXPERTS // EXPERTS_PER_STEP - 1)
    def _():
        last = n_total - 1
        for parity in range(2):
            @pl.when(last & 1 == parity)
            def _(parity=parity):
                scatter(jnp.maximum(last - 1, 0), 1 - parity, last >= 1)
                down(parity, state_ref[0])
                scatter(last, parity, None)


def _moe(hp, group, gt, seg, counts, rows, wtab, w):
    n_rows = TOP_K * gt + N_EXPERTS * MB
    grid_spec = pltpu.PrefetchScalarGridSpec(
        num_scalar_prefetch=4,
        grid=(N_EXPERTS // EXPERTS_PER_STEP,),
        in_specs=[pl.BlockSpec((gt * HP_ROWS, LANE), lambda e, *_: (group, 0), pipeline_mode=pl.Buffered(1)),
                  pl.BlockSpec((EXPERTS_PER_STEP, D, D_EXPERT), lambda e, *_: (e, 0, 0)),
                  pl.BlockSpec((EXPERTS_PER_STEP, D, D_EXPERT), lambda e, *_: (e, 0, 0)),
                  pl.BlockSpec((EXPERTS_PER_STEP, D_EXPERT, D), lambda e, *_: (e, 0, 0))],
        out_specs=pl.BlockSpec(((gt + 1) * OUT_ROWS, LANE), lambda e, *_: (0, 0), pipeline_mode=pl.Buffered(1)),
        scratch_shapes=[pltpu.VMEM((HP_ROWS * S_IN, LANE), F32), pltpu.VMEM((HP_ROWS * S_IN, LANE), F32),
                        pltpu.VMEM((MB, D_EXPERT), BF16), pltpu.VMEM((MB, D_EXPERT), BF16),
                        pltpu.VMEM((OUT_ROWS * S_OUT, LANE), F32), pltpu.VMEM((OUT_ROWS * S_OUT, LANE), F32),
                        pltpu.VMEM((D, D_EXPERT), BF16), pltpu.VMEM((D, D_EXPERT), BF16),
                        pltpu.VMEM((2, D_EXPERT, D), BF16),
                        pltpu.SMEM((n_rows,), I32), pltpu.SMEM((1,), I32)])
    return pl.pallas_call(
        functools.partial(_moe_kernel, gt),
        grid_spec=grid_spec,
        out_shape=jax.ShapeDtypeStruct(((gt + 1) * OUT_ROWS, LANE), F32),
        compiler_params=pltpu.CompilerParams(dimension_semantics=("arbitrary",), vmem_limit_bytes=VMEM_LIMIT),
        name="moe_experts",
    )(seg, counts, rows, wtab, hp, w["w_e1"], w["w_e3"], w["w_e2"])


def _combine_kernel(n_groups, tiles_per_group, *refs):
    r_refs = refs[:n_groups]
    base_ref, mod_ref, o_ref = refs[n_groups:]
    tile = pl.program_id(0) * pl.num_programs(1) + pl.program_id(1)
    g2 = mod_ref[0][5:6]
    n_tok = base_ref.shape[1]
    for g, r_ref in enumerate(r_refs):
        @pl.when(tile // tiles_per_group == g)
        def _(r_ref=r_ref):
            for s in range(OUT_ROWS):
                cols = slice(s * LANE, (s + 1) * LANE)
                o_ref[0, :, cols] = base_ref[0, :, cols] + g2[:, cols] * r_ref[pl.ds(s, n_tok, stride=OUT_ROWS), :]


def _combine(routed, base, modsel, per_batch, gt):
    B, S, _ = base.shape
    nt = S // TM
    tiles_per_group = gt // TM
    mod_idx = (lambda b, i: (b, 0, 0)) if per_batch else (lambda b, i: (0, 0, 0))

    def r_spec(g):
        def idx(b, i):
            tile = b * nt + i
            return (jnp.where(tile // tiles_per_group == g, tile % tiles_per_group, 0), 0)
        return pl.BlockSpec((TM * OUT_ROWS, LANE), idx)

    return pl.pallas_call(
        functools.partial(_combine_kernel, len(routed), tiles_per_group),
        grid=(B, nt),
        in_specs=[r_spec(g) for g in range(len(routed))] + [
            pl.BlockSpec((1, TM, D), lambda b, i: (b, i, 0)),
            pl.BlockSpec((1, N_MOD, D), mod_idx)],
        out_specs=pl.BlockSpec((1, TM, D), lambda b, i: (b, i, 0)),
        out_shape=jax.ShapeDtypeStruct((B, S, D), F32),
        compiler_params=pltpu.CompilerParams(dimension_semantics=("arbitrary", "arbitrary"),
                                             vmem_limit_bytes=VMEM_LIMIT),
        name="moe_combine",
    )(*routed, base, modsel)


def _trunk_pass(x, modsel, per_batch, w, rope_tabs, cache):
    B, S, _ = x.shape
    qa, ka, va, qb, ckv, krp, ga, gb = _project(x, modsel, per_batch, w, rope_tabs)
    if cache is None:
        ka_all, va_all, ckv_all, krp_all = ka, va, ckv, krp
    else:
        ck, cv, cckv, ckr = cache
        ka_all = jnp.concatenate([ck, ka], axis=2)
        va_all = jnp.concatenate([cv, va], axis=2)
        ckv_all = jnp.concatenate([cckv, ckv], axis=1)
        krp_all = jnp.concatenate([_pad64(ckr), krp], axis=1)
    va_ones = jnp.concatenate([va_all.astype(BF16), jnp.ones(va_all.shape, BF16)], axis=-1)
    oa = _attention(qa, ka_all.astype(BF16), va_ones, "attn_gqa")
    ob = _attention_mla(qb, ckv_all, krp_all, w)
    T = B * S
    gt = min(GROUP_TOKENS, T)
    assert T % gt == 0 and gt % TM == 0 and gt & (gt - 1) == 0 and gt >= MB and gt % LIST_CHUNK == 0
    base, hp, idx, wk, rank = _post(oa, ob, ga, gb, x, modsel, per_batch, w, gt)
    seg, counts, rows, wtab = _dispatch_rows(idx, wk, rank, T // gt)
    routed = [_moe(hp, g, gt, seg[g], counts[g], rows[g], wtab[g], w) for g in range(T // gt)]
    y = _combine(routed, base, modsel, per_batch, gt)
    return y, (ka, va, ckv, _unpad64(krp))


def _rope_tables(n_tokens):
    t = np.arange(n_tokens)
    row = (t // GRID_W).astype(np.float32)[:, None]
    col = (t % GRID_W).astype(np.float32)[:, None]

    def tabs(rot_dim):
        axis_dim = rot_dim // 2
        inv = (ROPE_THETA ** (-np.arange(0, axis_dim, 2, dtype=np.float32) / axis_dim)).astype(np.float32)
        ang = np.concatenate([row * inv, col * inv], axis=-1)
        return np.cos(ang), np.sin(ang)

    def pad64(a):
        z = np.zeros(a.shape[:-1] + (32,), a.dtype)
        return np.concatenate([a[..., :32], z, a[..., 32:], z], axis=-1)

    cos_a, sin_a = tabs(HD_A)
    cos_b, sin_b = tabs(ROPE_B)
    return tuple(jnp.asarray(a, F32) for a in (
        np.concatenate([cos_a, cos_a], -1), np.concatenate([-sin_a, sin_a], -1),
        pad64(np.concatenate([cos_b, cos_b], -1)), pad64(np.concatenate([-sin_b, sin_b], -1))))


def _layer_weights(l, w_in, w_qb, w_kvb, w_router, router_bias, named):
    w = {k: v[l] for k, v in named.items()}
    wi = w_in[l]
    n_front = C_KR + ROPE_B
    w["w_in"] = jnp.pad(wi[:, :n_front].astype(BF16), ((0, 0), (0, C_GA - n_front)))
    w["w_gate"] = wi[:, n_front:].astype(BF16)
    wq = w_qb[l].reshape(Q_LORA, H_B, NOPE_B + ROPE_B)
    w["w_qn"] = wq[:, :, :NOPE_B].reshape(Q_LORA, H_B * LANE).astype(BF16)
    w["w_qr"] = _pad64(wq[:, :, NOPE_B:]).reshape(Q_LORA, H_B * LANE).astype(BF16)
    wkv = w_kvb[l].reshape(KV_LORA, H_B, NOPE_B + VD_B)
    w["w_kvk"] = wkv[:, :, :NOPE_B].reshape(KV_LORA, H_B * LANE).astype(BF16)
    w["w_kvv"] = wkv[:, :, NOPE_B:].reshape(KV_LORA, H_B * LANE).astype(BF16)
    wr = jnp.pad(w_router[l], ((0, 0), (0, LANE - N_EXPERTS)))
    wr_hi = wr.astype(BF16)
    w["w_router"] = jnp.concatenate([wr_hi, (wr - wr_hi.astype(F32)).astype(BF16)], axis=1)
    w["router_bias"] = router_bias[l].reshape(N_EXPERTS, 1)
    for k in ("norm1_g", "norm2_g", "gqa_qn", "gqa_kn", "mla_qa_g", "mla_kva_g", "mla_qn_nope", "mla_kn_nope"):
        w[k] = w[k].reshape(1, -1)
    w["mla_qn_rope"] = _pad64(w["mla_qn_rope"]).reshape(1, LANE)
    w["mla_kn_rope"] = jnp.pad(w["mla_kn_rope"], (0, LANE - ROPE_B)).reshape(1, LANE)
    for k in ("w_oa", "w_ob", "w_out", "w_s1", "w_s3", "w_s2"):
        w[k] = w[k].astype(BF16)
    return w


def kernel(x_prompt, x_sample, c, cache_gqa_k, cache_gqa_v, cache_mla_ckv, cache_mla_krope, c_ctx, w_mod, b_mod, norm1_g, norm2_g, w_in, gqa_qn, gqa_kn, mla_qa_g, mla_kva_g, w_qb, w_kvb, mla_qn_nope, mla_qn_rope, mla_kn_nope, mla_kn_rope, w_oa, w_ob, w_out, w_router, router_bias, w_e1, w_e3, w_e2, w_s1, w_s3, w_s2):
    depth = w_mod.shape[0]
    n_dec = x_sample.shape[0]
    rope_tabs = _rope_tables(x_sample.shape[1])
    cond8 = jnp.concatenate([c_ctx[None, :], c, jnp.zeros((8 - 1 - n_dec, D), F32)], axis=0)
    named = dict(norm1_g=norm1_g, norm2_g=norm2_g, gqa_qn=gqa_qn, gqa_kn=gqa_kn, mla_qa_g=mla_qa_g,
                 mla_kva_g=mla_kva_g, mla_qn_nope=mla_qn_nope, mla_qn_rope=mla_qn_rope, mla_kn_nope=mla_kn_nope,
                 mla_kn_rope=mla_kn_rope, w_oa=w_oa, w_ob=w_ob, w_out=w_out, w_e1=w_e1, w_e3=w_e3, w_e2=w_e2,
                 w_s1=w_s1, w_s3=w_s3, w_s2=w_s2)
    xp, xs = x_prompt, x_sample
    ks, vs, ckvs, krs = [], [], [], []
    for l in range(depth):
        w = _layer_weights(l, w_in, w_qb, w_kvb, w_router, router_bias, named)
        mod = _modulation(cond8, w_mod[l], b_mod[l].reshape(1, -1))
        mod_ctx = mod[0:1].reshape(1, N_MOD, D)
        mod_dec = mod[1:1 + n_dec].reshape(n_dec, N_MOD, D)
        xp, ctx = _trunk_pass(xp, mod_ctx, False, w, None, None)
        ks.append(ctx[0])
        vs.append(ctx[1])
        ckvs.append(ctx[2])
        krs.append(ctx[3])
        cache = (cache_gqa_k[:, l], cache_gqa_v[:, l], cache_mla_ckv[:, l], cache_mla_krope[:, l])
        xs, _ = _trunk_pass(xs, mod_dec, True, w, rope_tabs, cache)
    return (xp, xs, jnp.stack(ks, axis=1), jnp.stack(vs, axis=1), jnp.stack(ckvs, axis=1), jnp.stack(krs, axis=1))
```

```python
import functools
import math

import jax
import jax.numpy as jnp
import numpy as np
from jax import lax
from jax.experimental import pallas as pl
from jax.experimental.pallas import tpu as pltpu

F32 = jnp.float32
BF16 = jnp.bfloat16
I32 = jnp.int32
HIGHEST = lax.Precision.HIGHEST

D = 1024
GRID_W = 64
RMS_EPS = 1e-6
ROPE_THETA = 10000.0
N_MOD = 6
H_A, KV_A, HD_A = 8, 2, 128
H_B, Q_LORA, KV_LORA, NOPE_B, ROPE_B, VD_B = 8, 256, 128, 128, 64, 128
N_EXPERTS, TOP_K, N_GROUPS, TOPK_GROUPS = 64, 8, 8, 4
GROUP_SIZE = N_EXPERTS // N_GROUPS
D_EXPERT = 256
ROUTED_SCALE = 2.5
QSCALE_A = HD_A ** -0.5 * math.log2(math.e)
QSCALE_B = (NOPE_B + ROPE_B) ** -0.5 * math.log2(math.e)

LANE = 128
C_QA, C_KA, C_VA, C_CQ, C_CKV, C_KR, C_GA, C_GB, C_END = 0, 1024, 1280, 1536, 1792, 1920, 2048, 3072, 4096
VMEM_LIMIT = 52 * 1024 * 1024

TM = 256
TQ = 256
TL = 256
GROUP_TOKENS = 4096
MB, MB_LOG2 = 128, 7
HP_ROWS = D // LANE
OUT_ROWS = D // LANE
S_IN = MB + 4
S_OUT = MB + 4
RMW_BATCH = 8
LIST_CHUNK = 16
EXPERTS_PER_STEP = 2


def _dot(a, b):
    return jnp.dot(a, b, preferred_element_type=F32)


def _rms(t, n_valid):
    ms = jnp.sum(t * t, axis=-1, keepdims=True) * (1.0 / n_valid)
    return t * lax.rsqrt(ms + RMS_EPS)


def _silu(t):
    return t * jax.nn.sigmoid(t)


def _pad64(a):
    z = jnp.zeros(a.shape[:-1] + (32,), a.dtype)
    return jnp.concatenate([a[..., :32], z, a[..., 32:], z], axis=-1)


def _unpad64(a):
    return jnp.concatenate([a[..., :32], a[..., 64:96]], axis=-1)


def _const_spec(shape):
    nd = len(shape)
    return pl.BlockSpec(shape, lambda *_: (0,) * nd)


def _mod_kernel(c_ref, w_ref, b_ref, o_ref):
    s = _silu(c_ref[...])
    o_ref[...] = jnp.dot(s, w_ref[...], preferred_element_type=F32, precision=HIGHEST) + b_ref[...]


def _modulation(cond8, w_mod, b_mod):
    tn = 1024
    n = w_mod.shape[1]
    return pl.pallas_call(
        _mod_kernel,
        grid=(n // tn,),
        in_specs=[pl.BlockSpec((8, D), lambda j: (0, 0)),
                  pl.BlockSpec((D, tn), lambda j: (0, j)),
                  pl.BlockSpec((1, tn), lambda j: (0, j))],
        out_specs=pl.BlockSpec((8, tn), lambda j: (0, j)),
        out_shape=jax.ShapeDtypeStruct((8, n), F32),
        compiler_params=pltpu.CompilerParams(dimension_semantics=("arbitrary",), vmem_limit_bytes=VMEM_LIMIT),
        name="adaln_mod",
    )(cond8, w_mod, b_mod)


def _proj_kernel(use_rope, *refs):
    (x_ref, mod_ref, g1_ref, win_ref, wgate_ref, wqn_ref, wqr_ref,
     gqn_ref, gkn_ref, gcq_ref, gckv_ref, gnope_ref, gqr_ref, gkr_ref) = refs[:14]
    n_in = 18 if use_rope else 14
    if use_rope:
        ca_ref, sa_ref, cb_ref, sb_ref = refs[14:18]
    qa_ref, ka_ref, va_ref, qb_ref, ckv_ref, kr_ref, ga_ref, gb_ref, kab_ref, vab_ref = refs[n_in:n_in + 10]
    zbuf = refs[n_in + 10:n_in + 12]
    qbuf = refs[n_in + 12:n_in + 14]
    step = pl.program_id(0)

    def rope(t, c_ref, s_ref):
        return t * c_ref[...] + pltpu.roll(t, LANE // 2, 1) * s_ref[...]

    rope_a = (lambda t: rope(t, ca_ref, sa_ref)) if use_rope else (lambda t: t)
    rope_b = (lambda t: rope(t, cb_ref, sb_ref)) if use_rope else (lambda t: t)

    @pl.when(step == 0)
    def _():
        zbuf[1][...] = jnp.zeros_like(zbuf[1])
        qbuf[1][...] = jnp.zeros_like(qbuf[1])

    def matmuls(z_ref, q_ref):
        x = x_ref[0]
        m = mod_ref[0]
        sh1, sc1 = m[0:1], m[1:2]
        hb = ((_rms(x, D) * g1_ref[...]) * (1.0 + sc1) + sh1).astype(BF16)
        for lo, hi in ((C_QA, C_KA), (C_KA, C_CQ), (C_CKV, C_GA)):
            z_ref[:, lo:hi] = _dot(hb, win_ref[:, lo:hi])
        for lo, hi in ((C_GA, C_GB), (C_GB, C_END)):
            z_ref[:, lo:hi] = _dot(hb, wgate_ref[:, lo - C_GA:hi - C_GA])
        cq = (_rms(_dot(hb, win_ref[:, C_CQ:C_CKV]), Q_LORA) * gcq_ref[...]).astype(BF16)
        q_ref[:, 0:H_B * LANE] = _dot(cq, wqn_ref[...])
        q_ref[:, H_B * LANE:2 * H_B * LANE] = _dot(cq, wqr_ref[...])

    def epilogue(z_ref, q_ref):
        for hh in range(H_A):
            t = _rms(z_ref[:, C_QA + hh * HD_A:C_QA + (hh + 1) * HD_A], HD_A) * gqn_ref[...]
            qa_ref[0, hh] = (rope_a(t) * QSCALE_A).astype(BF16)
        for j in range(KV_A):
            t = _rms(z_ref[:, C_KA + j * HD_A:C_KA + (j + 1) * HD_A], HD_A) * gkn_ref[...]
            kk = rope_a(t)
            vv = z_ref[:, C_VA + j * HD_A:C_VA + (j + 1) * HD_A]
            ka_ref[0, j] = kk
            va_ref[0, j] = vv
            kab_ref[0, j] = kk.astype(BF16)
            vab_ref[0, j, :, 0:HD_A] = vv.astype(BF16)
            vab_ref[0, j, :, HD_A:2 * HD_A] = jnp.ones(vv.shape, BF16)
        for hh in range(H_B):
            t = _rms(q_ref[:, hh * LANE:(hh + 1) * LANE], NOPE_B) * gnope_ref[...]
            qb_ref[0, hh, :, 0:LANE] = (t * QSCALE_B).astype(BF16)
            t = _rms(q_ref[:, (H_B + hh) * LANE:(H_B + hh + 1) * LANE], ROPE_B) * gqr_ref[...]
            qb_ref[0, hh, :, LANE:2 * LANE] = (rope_b(t) * QSCALE_B).astype(BF16)
        ckv_ref[0] = _rms(z_ref[:, C_CKV:C_KR], KV_LORA) * gckv_ref[...]
        t = _rms(z_ref[:, C_KR:C_GA], ROPE_B) * gkr_ref[...]
        lane = lax.broadcasted_iota(I32, t.shape, 1)
        second = jnp.where((lane >= LANE // 2) & (lane < 3 * LANE // 4), pltpu.roll(t, LANE // 4, 1), 0.0)
        kr_ref[0] = rope_b(jnp.where(lane < LANE // 4, t, second))
        ga_ref[0] = jax.nn.sigmoid(z_ref[:, C_GA:C_GB])
        gb_ref[0] = jax.nn.sigmoid(z_ref[:, C_GB:C_END])

    for parity in range(2):
        @pl.when(step % 2 == parity)
        def _(parity=parity):
            matmuls(zbuf[parity], qbuf[parity])
            epilogue(zbuf[1 - parity], qbuf[1 - parity])


def _project(x, modsel, per_batch, w, rope_tabs):
    B, S, _ = x.shape
    use_rope = rope_tabs is not None
    nt = S // TM
    n_tiles = B * nt
    cur = lambda s: jnp.minimum(s, n_tiles - 1)
    prev = lambda s: jnp.maximum(s - 1, 0)
    mod_idx = (lambda s: (cur(s) // nt, 0, 0)) if per_batch else (lambda s: (0, 0, 0))
    in_specs = [pl.BlockSpec((1, TM, D), lambda s: (cur(s) // nt, cur(s) % nt, 0)),
                pl.BlockSpec((1, N_MOD, D), mod_idx),
                _const_spec((1, D)), _const_spec((D, C_GA)), _const_spec((D, C_END - C_GA)),
                _const_spec((Q_LORA, H_B * LANE)), _const_spec((Q_LORA, H_B * LANE)),
                _const_spec((1, LANE)), _const_spec((1, LANE)), _const_spec((1, Q_LORA)), _const_spec((1, LANE)),
                _const_spec((1, LANE)), _const_spec((1, LANE)), _const_spec((1, LANE))]
    args = [x, modsel, w["norm1_g"], w["w_in"], w["w_gate"], w["w_qn"], w["w_qr"], w["gqa_qn"], w["gqa_kn"], w["mla_qa_g"],
            w["mla_kva_g"], w["mla_qn_nope"], w["mla_qn_rope"], w["mla_kn_rope"]]
    if use_rope:
        in_specs += [pl.BlockSpec((TM, LANE), lambda s: (prev(s) % nt, 0))] * 4
        args += list(rope_tabs)
    tok4 = lambda h, d: pl.BlockSpec((1, h, TM, d), lambda s: (prev(s) // nt, 0, prev(s) % nt, 0))
    tok3 = lambda d: pl.BlockSpec((1, TM, d), lambda s: (prev(s) // nt, prev(s) % nt, 0))
    out_specs = [tok4(H_A, HD_A), tok4(KV_A, HD_A), tok4(KV_A, HD_A), tok4(H_B, 2 * LANE),
                 tok3(KV_LORA), tok3(LANE), tok3(D), tok3(D), tok4(KV_A, HD_A), tok4(KV_A, 2 * HD_A)]
    out_shape = [jax.ShapeDtypeStruct((B, H_A, S, HD_A), BF16),
                 jax.ShapeDtypeStruct((B, KV_A, S, HD_A), F32),
                 jax.ShapeDtypeStruct((B, KV_A, S, HD_A), F32),
                 jax.ShapeDtypeStruct((B, H_B, S, 2 * LANE), BF16),
                 jax.ShapeDtypeStruct((B, S, KV_LORA), F32),
                 jax.ShapeDtypeStruct((B, S, LANE), F32),
                 jax.ShapeDtypeStruct((B, S, D), F32),
                 jax.ShapeDtypeStruct((B, S, D), F32),
                 jax.ShapeDtypeStruct((B, KV_A, S, HD_A), BF16),
                 jax.ShapeDtypeStruct((B, KV_A, S, 2 * HD_A), BF16)]
    return pl.pallas_call(
        functools.partial(_proj_kernel, use_rope),
        grid=(n_tiles + 1,),
        in_specs=in_specs, out_specs=out_specs, out_shape=out_shape,
        scratch_shapes=[pltpu.VMEM((TM, C_END), F32), pltpu.VMEM((TM, C_END), F32),
                        pltpu.VMEM((TM, 2 * H_B * LANE), F32), pltpu.VMEM((TM, 2 * H_B * LANE), F32)],
        compiler_params=pltpu.CompilerParams(dimension_semantics=("arbitrary",), vmem_limit_bytes=VMEM_LIMIT),
        name="proj_rope" if use_rope else "proj",
    )(*args)


def _attend(n_heads, q_ref, key, value, o_ref):
    for hh in range(n_heads):
        v = value(hh)
        dv = v.shape[-1] // 2
        s = lax.dot_general(q_ref[0, hh], key(hh), (((1,), (1,)), ((), ())), preferred_element_type=F32)
        p = jnp.exp2(s - jnp.max(s, axis=-1, keepdims=True))
        o = _dot(p.astype(BF16), v)
        o_ref[0, :, hh * dv:(hh + 1) * dv] = (o[:, :dv] / o[:, dv:]).astype(BF16)


def _attn_kernel(n_heads, group, q_ref, k_ref, v_ref, o_ref):
    _attend(n_heads, q_ref, lambda hh: k_ref[0, hh // group], lambda hh: v_ref[0, hh // group], o_ref)


def _attn_mla_kernel(q_ref, ckv_ref, kr_ref, wk_ref, wv_ref, gk_ref, o_ref, kb_ref, vb_ref):
    @pl.when(pl.program_id(1) == 0)
    def _():
        def chunk(c, carry):
            rows = pl.ds(pl.multiple_of(c * TL, TL), TL)
            cb = ckv_ref[0, rows, :].astype(BF16)
            k = _dot(cb, wk_ref[...])
            v = _dot(cb, wv_ref[...])
            krb = kr_ref[0, rows, :].astype(BF16)
            for hh in range(H_B):
                t = _rms(k[:, hh * LANE:(hh + 1) * LANE], NOPE_B) * gk_ref[...]
                kb_ref[hh, rows, 0:LANE] = t.astype(BF16)
                kb_ref[hh, rows, LANE:2 * LANE] = krb
                vb_ref[hh, rows, 0:VD_B] = v[:, hh * LANE:(hh + 1) * LANE].astype(BF16)
                vb_ref[hh, rows, VD_B:2 * VD_B] = jnp.ones((TL, VD_B), BF16)
            return carry

        lax.fori_loop(0, ckv_ref.shape[1] // TL, chunk, 0)

    _attend(H_B, q_ref, lambda hh: kb_ref[hh], lambda hh: vb_ref[hh], o_ref)


def _attention_mla(q, ckv_all, krp_all, w):
    B, H, S, dk = q.shape
    L = ckv_all.shape[1]
    return pl.pallas_call(
        _attn_mla_kernel,
        grid=(B, S // TQ),
        in_specs=[pl.BlockSpec((1, H, TQ, dk), lambda b, i: (b, 0, i, 0)),
                  pl.BlockSpec((1, L, KV_LORA), lambda b, i: (b, 0, 0)),
                  pl.BlockSpec((1, L, LANE), lambda b, i: (b, 0, 0)),
                  _const_spec((KV_LORA, H_B * LANE)), _const_spec((KV_LORA, H_B * LANE)), _const_spec((1, LANE))],
        out_specs=pl.BlockSpec((1, TQ, H * VD_B), lambda b, i: (b, i, 0)),
        out_shape=jax.ShapeDtypeStruct((B, S, H * VD_B), BF16),
        scratch_shapes=[pltpu.VMEM((H_B, L, 2 * LANE), BF16), pltpu.VMEM((H_B, L, 2 * VD_B), BF16)],
        compiler_params=pltpu.CompilerParams(dimension_semantics=("arbitrary", "arbitrary"),
                                             vmem_limit_bytes=VMEM_LIMIT),
        name="attn_mla",
    )(q, ckv_all, krp_all, w["w_kvk"], w["w_kvv"], w["mla_kn_nope"])


def _attention(q, k, v, name):
    B, H, S, dk = q.shape
    _, Hk, L, dv2 = v.shape
    dv = dv2 // 2
    return pl.pallas_call(
        functools.partial(_attn_kernel, H, H // Hk),
        grid=(B, S // TQ),
        in_specs=[pl.BlockSpec((1, H, TQ, dk), lambda b, i: (b, 0, i, 0)),
                  pl.BlockSpec((1, Hk, L, dk), lambda b, i: (b, 0, 0, 0)),
                  pl.BlockSpec((1, Hk, L, dv2), lambda b, i: (b, 0, 0, 0))],
        out_specs=pl.BlockSpec((1, TQ, H * dv), lambda b, i: (b, i, 0)),
        out_shape=jax.ShapeDtypeStruct((B, S, H * dv), BF16),
        compiler_params=pltpu.CompilerParams(dimension_semantics=("arbitrary", "arbitrary"),
                                             vmem_limit_bytes=VMEM_LIMIT),
        name=name,
    )(q, k, v)


def _route(bi, sc, carry):
    n_tok = bi.shape[1]
    iota8 = lax.broadcasted_iota(I32, (GROUP_SIZE, n_tok), 0)
    neg = -jnp.inf
    grp = [bi[j * GROUP_SIZE:(j + 1) * GROUP_SIZE] for j in range(N_GROUPS)]
    gscore = []
    for g in grp:
        m1 = jnp.max(g, axis=0, keepdims=True)
        first = jnp.min(jnp.where(g == m1, iota8, GROUP_SIZE), axis=0, keepdims=True)
        m2 = jnp.max(jnp.where(iota8 == first, neg, g), axis=0, keepdims=True)
        gscore.append(m1 + m2)
    masked = []
    for j in range(N_GROUPS):
        cnt = jnp.zeros((1, n_tok), I32)
        for j2 in range(N_GROUPS):
            if j2 == j:
                continue
            beats = (gscore[j2] > gscore[j]) if j2 > j else (gscore[j2] >= gscore[j])
            cnt = cnt + beats.astype(I32)
        keep = jnp.broadcast_to(cnt, grp[j].shape) < TOPK_GROUPS
        masked.append(jnp.where(keep, grp[j], neg))
    eid = [(iota8 + j * GROUP_SIZE).astype(F32) for j in range(N_GROUPS)]
    scg = [sc[j * GROUP_SIZE:(j + 1) * GROUP_SIZE] for j in range(N_GROUPS)]
    work = list(masked)
    kept = [jnp.zeros((GROUP_SIZE, n_tok), F32) for _ in range(N_GROUPS)]
    idx_rows = jnp.zeros((TOP_K, n_tok), F32)
    score_rows = jnp.zeros((TOP_K, n_tok), F32)
    for k in range(TOP_K):
        top = work[0]
        for j in range(1, N_GROUPS):
            top = jnp.maximum(top, work[j])
        top = jnp.max(top, axis=0, keepdims=True)
        cand = jnp.where(work[0] == top, eid[0], float(N_EXPERTS))
        for j in range(1, N_GROUPS):
            cand = jnp.minimum(cand, jnp.where(work[j] == top, eid[j], float(N_EXPERTS)))
        chosen = jnp.min(cand, axis=0, keepdims=True)
        score = jnp.zeros((GROUP_SIZE, n_tok), F32)
        for j in range(N_GROUPS):
            hit = eid[j] == chosen
            score = score + jnp.where(hit, scg[j], 0.0)
            kept[j] = kept[j] + jnp.where(hit, 1.0, 0.0)
            work[j] = jnp.where(hit, neg, work[j])
        idx_rows = jnp.where(iota8 == k, chosen, idx_rows)
        score_rows = jnp.where(iota8 == k, jnp.sum(score, axis=0, keepdims=True), score_rows)
    w_rows = score_rows / jnp.sum(score_rows, axis=0, keepdims=True) * ROUTED_SCALE
    kept = jnp.concatenate(kept, axis=0)
    earlier = (lax.broadcasted_iota(I32, (n_tok, n_tok), 0) < lax.broadcasted_iota(I32, (n_tok, n_tok), 1))
    rank = _dot(kept.astype(BF16), earlier.astype(F32).astype(BF16)) + carry
    new_carry = carry + jnp.sum(kept, axis=1, keepdims=True)
    rank_rows = jnp.zeros((TOP_K, n_tok), F32)
    for k in range(TOP_K):
        ra = jnp.zeros((GROUP_SIZE, n_tok), F32)
        for j in range(N_GROUPS):
            ra = ra + jnp.where(eid[j] == idx_rows[k:k + 1], rank[j * GROUP_SIZE:(j + 1) * GROUP_SIZE], 0.0)
        rank_rows = jnp.where(iota8 == k, jnp.sum(ra, axis=0, keepdims=True), rank_rows)
    return idx_rows, w_rows, rank_rows, new_carry


def _post_kernel(tiles_per_group, oa_ref, ob_ref, ga_ref, gb_ref, x_ref, mod_ref, g2n_ref, woa_ref, wob_ref,
                 wout_ref, wr_ref, rb_ref, ws1_ref, ws3_ref, ws2_ref, base_ref, hp_ref, idx_ref, wk_ref, rank_ref,
                 carry_ref):
    tile = pl.program_id(0) * pl.num_programs(1) + pl.program_id(1)

    @pl.when(tile % tiles_per_group == 0)
    def _():
        carry_ref[...] = jnp.zeros_like(carry_ref)

    m = mod_ref[0]
    g1, sh2, sc2, g2 = m[2:3], m[3:4], m[4:5], m[5:6]
    merged = ga_ref[0] * _dot(oa_ref[0], woa_ref[...]) + gb_ref[0] * _dot(ob_ref[0], wob_ref[...])
    x1 = x_ref[0] + g1 * _dot(merged.astype(BF16), wout_ref[...])
    h2 = (_rms(x1, D) * g2n_ref[...]) * (1.0 + sc2) + sh2
    h2b = h2.astype(BF16)
    act = _silu(_dot(h2b, ws1_ref[...])) * _dot(h2b, ws3_ref[...])
    base_ref[0] = x1 + g2 * _dot(act.astype(BF16), ws2_ref[...])
    n_tok = h2.shape[0]
    for j in range(HP_ROWS):
        hp_ref[pl.ds(j, n_tok, stride=HP_ROWS), :] = h2[:, j * LANE:(j + 1) * LANE]
    h2lo = (h2 - h2b.astype(F32)).astype(BF16)
    hw = _dot(h2b, wr_ref[...])
    logits = hw[:, 0:LANE] + (hw[:, LANE:2 * LANE] + _dot(h2lo, wr_ref[:, 0:LANE]))
    sc = jax.nn.sigmoid(logits.T[0:N_EXPERTS])
    idx_rows, w_rows, rank_rows, carry = _route(sc + rb_ref[...], sc, carry_ref[...])
    carry_ref[...] = carry
    idx_ref[...] = idx_rows.astype(I32)
    wk_ref[...] = w_rows
    rank_ref[...] = rank_rows.astype(I32)


def _post(oa, ob, ga, gb, x, modsel, per_batch, w, gt):
    B, S, _ = x.shape
    nt = S // TM
    mod_idx = (lambda b, i: (b, 0, 0)) if per_batch else (lambda b, i: (0, 0, 0))
    tok = lambda d: pl.BlockSpec((1, TM, d), lambda b, i: (b, i, 0))
    slot = lambda: pl.BlockSpec((TOP_K, TM), lambda b, i: (0, b * nt + i))
    return pl.pallas_call(
        functools.partial(_post_kernel, gt // TM),
        grid=(B, nt),
        in_specs=[tok(D), tok(D), tok(D), tok(D), tok(D), pl.BlockSpec((1, N_MOD, D), mod_idx),
                  _const_spec((1, D)), _const_spec((D, D)), _const_spec((D, D)), _const_spec((D, D)),
                  _const_spec((D, 2 * LANE)), _const_spec((N_EXPERTS, 1)),
                  _const_spec((D, D_EXPERT)), _const_spec((D, D_EXPERT)), _const_spec((D_EXPERT, D))],
        out_specs=[tok(D),
                   pl.BlockSpec((TM * HP_ROWS, LANE), lambda b, i: (b * nt + i, 0)),
                   slot(), slot(), slot()],
        out_shape=[jax.ShapeDtypeStruct((B, S, D), F32),
                   jax.ShapeDtypeStruct((B * S * HP_ROWS, LANE), F32),
                   jax.ShapeDtypeStruct((TOP_K, B * S), I32),
                   jax.ShapeDtypeStruct((TOP_K, B * S), F32),
                   jax.ShapeDtypeStruct((TOP_K, B * S), I32)],
        scratch_shapes=[pltpu.VMEM((N_EXPERTS, 1), F32)],
        compiler_params=pltpu.CompilerParams(dimension_semantics=("arbitrary", "arbitrary"),
                                             vmem_limit_bytes=VMEM_LIMIT),
        name="post",
    )(oa, ob, ga, gb, x, modsel, w["norm2_g"], w["w_oa"], w["w_ob"], w["w_out"], w["w_router"],
      w["router_bias"], w["w_s1"], w["w_s3"], w["w_s2"])


def _dispatch_rows(idx, wk, rank, n_groups):
    T = idx.shape[1]
    gt = T // n_groups
    experts = jnp.arange(N_EXPERTS, dtype=I32)
    regroup = lambda a: a.reshape(TOP_K, n_groups, gt).transpose(1, 0, 2)
    eid = regroup(idx)
    onehot = (eid[:, None] == experts[None, :, None, None]).astype(I32)
    counts = jnp.sum(onehot, axis=(2, 3))
    padded = (counts + (MB - 1)) // MB * MB
    bounds = jnp.concatenate([jnp.zeros((n_groups, 1), I32), jnp.cumsum(padded, axis=1)], axis=1)
    row = jnp.sum(onehot * bounds[:, :N_EXPERTS, None, None], axis=1) + regroup(rank)
    wtab = jnp.pad(regroup(wk), ((0, 0), (0, 0), (0, gt)))
    return bounds, counts, row.reshape(n_groups, TOP_K * gt), wtab.reshape(n_groups, TOP_K * 2 * gt)


def _moe_kernel(gt, seg_ref, cnt_ref, row_ref, wtab_ref, hp_ref, w1_ref, w3_ref, w2_ref, o_ref,
                tin0_ref, tin1_ref, act0_ref, act1_ref, tout0_ref, tout1_ref, w1b_ref, w3b_ref, w2b_ref,
                code_ref, state_ref):
    tin = (tin0_ref, tin1_ref)
    act = (act0_ref, act1_ref)
    tout = (tout0_ref, tout1_ref)

    def build_row_list():
        def spare_tail(ex, carry):
            def one(p, c):
                code_ref[p] = gt
                return c
            return lax.fori_loop(seg_ref[ex] + cnt_ref[ex], seg_ref[ex + 1], one, carry)

        lax.fori_loop(0, N_EXPERTS, spare_tail, 0)

        def chunk(i, carry):
            t0 = i * LIST_CHUNK
            for u in range(LIST_CHUNK):
                for k in range(TOP_K):
                    code_ref[row_ref[k * gt + t0 + u]] = k * 2 * gt + t0 + u
            return carry

        lax.fori_loop(0, gt // LIST_CHUNK, chunk, 0)

    e = pl.program_id(0)
    n_total = lax.shift_right_logical(seg_ref[N_EXPERTS], MB_LOG2)

    def gather(b, slot):
        r0 = b * MB
        for m in range(MB):
            t = code_ref[r0 + m] & (gt - 1)
            slab = hp_ref[pl.ds(pl.multiple_of(t * HP_ROWS, HP_ROWS), HP_ROWS), :]
            tin[slot][pl.ds(m, HP_ROWS, stride=S_IN), :] = slab

    def scatter(b, slot, live):
        r0 = b * MB
        for b0 in range(0, MB, RMW_BATCH):
            pending = []
            for m in range(b0, b0 + RMW_BATCH):
                code = code_ref[r0 + m]
                off = pl.multiple_of((code & (2 * gt - 1)) * OUT_ROWS, OUT_ROWS)
                upd = tout[slot][pl.ds(m, OUT_ROWS, stride=S_OUT), :] * wtab_ref[code]
                if live is not None:
                    upd = jnp.where(live, upd, 0.0)
                pending.append((off, o_ref[pl.ds(off, OUT_ROWS), :] + upd))
            for off, v in pending:
                o_ref[pl.ds(off, OUT_ROWS), :] = v

    def down(parity, w2_slot):
        y = _dot(act[parity][...], w2b_ref[w2_slot])
        for j in range(OUT_ROWS):
            tout[parity][pl.ds(j * S_OUT, MB), :] = y[:, j * LANE:(j + 1) * LANE]

    @pl.when(e == 0)
    def _():
        build_row_list()
        state_ref[0] = 0
        o_ref[...] = jnp.zeros_like(o_ref)
        act1_ref[...] = jnp.zeros_like(act1_ref)
        tout0_ref[...] = jnp.zeros_like(tout0_ref)
        w2b_ref[0] = jnp.zeros((D_EXPERT, D), BF16)
        gather(0, 0)

    def stages(b, parity, w2_slot):
        gather(jnp.minimum(b + 1, n_total - 1), 1 - parity)
        x = jnp.concatenate([tin[parity][pl.ds(j * S_IN, MB), :] for j in range(HP_ROWS)], axis=-1).astype(BF16)
        act[parity][...] = (_silu(_dot(x, w1b_ref[...])) * _dot(x, w3b_ref[...])).astype(BF16)
        down(1 - parity, w2_slot)
        scatter(jnp.maximum(b - 2, 0), parity, b >= 2)

    def expert(sub, carry):
        ex = e * EXPERTS_PER_STEP + sub
        first = lax.shift_right_logical(seg_ref[ex], MB_LOG2)
        n_blocks = lax.shift_right_logical(seg_ref[ex + 1], MB_LOG2) - first
        prev_slot = state_ref[0]
        this_slot = 1 - prev_slot
        w1b_ref[...] = w1_ref[sub].astype(BF16)
        w3b_ref[...] = w3_ref[sub].astype(BF16)

        @pl.when(n_blocks > 0)
        def _():
            w2b_ref[this_slot] = w2_ref[sub].astype(BF16)

        def block(i, c):
            b = first + i
            w2_slot = jnp.where(i == 0, prev_slot, this_slot)
            for parity in range(2):
                @pl.when(b & 1 == parity)
                def _(parity=parity):
                    stages(b, parity, w2_slot)
            return c

        carry = lax.fori_loop(0, n_blocks, block, carry)

        @pl.when(n_blocks > 0)
        def _():
            state_ref[0] = this_slot

        return carry

    lax.fori_loop(0, EXPERTS_PER_STEP, expert, 0)

    @pl.when(e == N_EXPERTS // EXPERTS_PER_STEP - 1)
    def _():
        last = n_total - 1
        for parity in range(2):
            @pl.when(last & 1 == parity)
            def _(parity=parity):
                scatter(jnp.maximum(last - 1, 0), 1 - parity, last >= 1)
                down(parity, state_ref[0])
                scatter(last, parity, None)


def _moe(hp, group, gt, seg, counts, rows, wtab, w):
    n_rows = TOP_K * gt + N_EXPERTS * MB
    grid_spec = pltpu.PrefetchScalarGridSpec(
        num_scalar_prefetch=4,
        grid=(N_EXPERTS // EXPERTS_PER_STEP,),
        in_specs=[pl.BlockSpec((gt * HP_ROWS, LANE), lambda e, *_: (group, 0), pipeline_mode=pl.Buffered(1)),
                  pl.BlockSpec((EXPERTS_PER_STEP, D, D_EXPERT), lambda e, *_: (e, 0, 0)),
                  pl.BlockSpec((EXPERTS_PER_STEP, D, D_EXPERT), lambda e, *_: (e, 0, 0)),
                  pl.BlockSpec((EXPERTS_PER_STEP, D_EXPERT, D), lambda e, *_: (e, 0, 0))],
        out_specs=pl.BlockSpec(((gt + 1) * OUT_ROWS, LANE), lambda e, *_: (0, 0), pipeline_mode=pl.Buffered(1)),
        scratch_shapes=[pltpu.VMEM((HP_ROWS * S_IN, LANE), F32), pltpu.VMEM((HP_ROWS * S_IN, LANE), F32),
                        pltpu.VMEM((MB, D_EXPERT), BF16), pltpu.VMEM((MB, D_EXPERT), BF16),
                        pltpu.VMEM((OUT_ROWS * S_OUT, LANE), F32), pltpu.VMEM((OUT_ROWS * S_OUT, LANE), F32),
                        pltpu.VMEM((D, D_EXPERT), BF16), pltpu.VMEM((D, D_EXPERT), BF16),
                        pltpu.VMEM((2, D_EXPERT, D), BF16),
                        pltpu.SMEM((n_rows,), I32), pltpu.SMEM((1,), I32)])
    return pl.pallas_call(
        functools.partial(_moe_kernel, gt),
        grid_spec=grid_spec,
        out_shape=jax.ShapeDtypeStruct(((gt + 1) * OUT_ROWS, LANE), F32),
        compiler_params=pltpu.CompilerParams(dimension_semantics=("arbitrary",), vmem_limit_bytes=VMEM_LIMIT),
        name="moe_experts",
    )(seg, counts, rows, wtab, hp, w["w_e1"], w["w_e3"], w["w_e2"])


def _combine_kernel(n_groups, tiles_per_group, *refs):
    r_refs = refs[:n_groups]
    base_ref, mod_ref, o_ref = refs[n_groups:]
    tile = pl.program_id(0) * pl.num_programs(1) + pl.program_id(1)
    g2 = mod_ref[0][5:6]
    n_tok = base_ref.shape[1]
    for g, r_ref in enumerate(r_refs):
        @pl.when(tile // tiles_per_group == g)
        def _(r_ref=r_ref):
            for s in range(OUT_ROWS):
                cols = slice(s * LANE, (s + 1) * LANE)
                o_ref[0, :, cols] = base_ref[0, :, cols] + g2[:, cols] * r_ref[pl.ds(s, n_tok, stride=OUT_ROWS), :]


def _combine(routed, base, modsel, per_batch, gt):
    B, S, _ = base.shape
    nt = S // TM
    tiles_per_group = gt // TM
    mod_idx = (lambda b, i: (b, 0, 0)) if per_batch else (lambda b, i: (0, 0, 0))

    def r_spec(g):
        def idx(b, i):
            tile = b * nt + i
            return (jnp.where(tile // tiles_per_group == g, tile % tiles_per_group, 0), 0)
        return pl.BlockSpec((TM * OUT_ROWS, LANE), idx)

    return pl.pallas_call(
        functools.partial(_combine_kernel, len(routed), tiles_per_group),
        grid=(B, nt),
        in_specs=[r_spec(g) for g in range(len(routed))] + [
            pl.BlockSpec((1, TM, D), lambda b, i: (b, i, 0)),
            pl.BlockSpec((1, N_MOD, D), mod_idx)],
        out_specs=pl.BlockSpec((1, TM, D), lambda b, i: (b, i, 0)),
        out_shape=jax.ShapeDtypeStruct((B, S, D), F32),
        compiler_params=pltpu.CompilerParams(dimension_semantics=("arbitrary", "arbitrary"),
                                             vmem_limit_bytes=VMEM_LIMIT),
        name="moe_combine",
    )(*routed, base, modsel)


def _trunk_pass(x, modsel, per_batch, w, rope_tabs, cache):
    B, S, _ = x.shape
    qa, ka, va, qb, ckv, krp, ga, gb, k_bf, v_ones = _project(x, modsel, per_batch, w, rope_tabs)
    if cache is None:
        ckv_all, krp_all = ckv, krp
    else:
        ck, cv, cckv, ckr = cache
        cv_ones = jnp.concatenate([cv.astype(BF16), jnp.ones(cv.shape, BF16)], axis=-1)
        k_bf = jnp.concatenate([ck.astype(BF16), k_bf], axis=2)
        v_ones = jnp.concatenate([cv_ones, v_ones], axis=2)
        ckv_all = jnp.concatenate([cckv, ckv], axis=1)
        krp_all = jnp.concatenate([_pad64(ckr), krp], axis=1)
    oa = _attention(qa, k_bf, v_ones, "attn_gqa")
    ob = _attention_mla(qb, ckv_all, krp_all, w)
    T = B * S
    gt = min(GROUP_TOKENS, T)
    assert T % gt == 0 and gt % TM == 0 and gt & (gt - 1) == 0 and gt >= MB and gt % LIST_CHUNK == 0
    base, hp, idx, wk, rank = _post(oa, ob, ga, gb, x, modsel, per_batch, w, gt)
    seg, counts, rows, wtab = _dispatch_rows(idx, wk, rank, T // gt)
    routed = [_moe(hp, g, gt, seg[g], counts[g], rows[g], wtab[g], w) for g in range(T // gt)]
    y = _combine(routed, base, modsel, per_batch, gt)
    return y, (ka, va, ckv, _unpad64(krp))


def _rope_tables(n_tokens):
    t = np.arange(n_tokens)
    row = (t // GRID_W).astype(np.float32)[:, None]
    col = (t % GRID_W).astype(np.float32)[:, None]

    def tabs(rot_dim):
        axis_dim = rot_dim // 2
        inv = (ROPE_THETA ** (-np.arange(0, axis_dim, 2, dtype=np.float32) / axis_dim)).astype(np.float32)
        ang = np.concatenate([row * inv, col * inv], axis=-1)
        return np.cos(ang), np.sin(ang)

    def pad64(a):
        z = np.zeros(a.shape[:-1] + (32,), a.dtype)
        return np.concatenate([a[..., :32], z, a[..., 32:], z], axis=-1)

    cos_a, sin_a = tabs(HD_A)
    cos_b, sin_b = tabs(ROPE_B)
    return tuple(jnp.asarray(a, F32) for a in (
        np.concatenate([cos_a, cos_a], -1), np.concatenate([-sin_a, sin_a], -1),
        pad64(np.concatenate([cos_b, cos_b], -1)), pad64(np.concatenate([-sin_b, sin_b], -1))))


def _layer_weights(l, w_in, w_qb, w_kvb, w_router, router_bias, named):
    w = {k: v[l] for k, v in named.items()}
    wi = w_in[l]
    n_front = C_KR + ROPE_B
    w["w_in"] = jnp.pad(wi[:, :n_front].astype(BF16), ((0, 0), (0, C_GA - n_front)))
    w["w_gate"] = wi[:, n_front:].astype(BF16)
    wq = w_qb[l].reshape(Q_LORA, H_B, NOPE_B + ROPE_B)
    w["w_qn"] = wq[:, :, :NOPE_B].reshape(Q_LORA, H_B * LANE).astype(BF16)
    w["w_qr"] = _pad64(wq[:, :, NOPE_B:]).reshape(Q_LORA, H_B * LANE).astype(BF16)
    wkv = w_kvb[l].reshape(KV_LORA, H_B, NOPE_B + VD_B)
    w["w_kvk"] = wkv[:, :, :NOPE_B].reshape(KV_LORA, H_B * LANE).astype(BF16)
    w["w_kvv"] = wkv[:, :, NOPE_B:].reshape(KV_LORA, H_B * LANE).astype(BF16)
    wr = jnp.pad(w_router[l], ((0, 0), (0, LANE - N_EXPERTS)))
    wr_hi = wr.astype(BF16)
    w["w_router"] = jnp.concatenate([wr_hi, (wr - wr_hi.astype(F32)).astype(BF16)], axis=1)
    w["router_bias"] = router_bias[l].reshape(N_EXPERTS, 1)
    for k in ("norm1_g", "norm2_g", "gqa_qn", "gqa_kn", "mla_qa_g", "mla_kva_g", "mla_qn_nope", "mla_kn_nope"):
        w[k] = w[k].reshape(1, -1)
    w["mla_qn_rope"] = _pad64(w["mla_qn_rope"]).reshape(1, LANE)
    w["mla_kn_rope"] = jnp.pad(w["mla_kn_rope"], (0, LANE - ROPE_B)).reshape(1, LANE)
    for k in ("w_oa", "w_ob", "w_out", "w_s1", "w_s3", "w_s2"):
        w[k] = w[k].astype(BF16)
    return w


def kernel(x_prompt, x_sample, c, cache_gqa_k, cache_gqa_v, cache_mla_ckv, cache_mla_krope, c_ctx, w_mod, b_mod, norm1_g, norm2_g, w_in, gqa_qn, gqa_kn, mla_qa_g, mla_kva_g, w_qb, w_kvb, mla_qn_nope, mla_qn_rope, mla_kn_nope, mla_kn_rope, w_oa, w_ob, w_out, w_router, router_bias, w_e1, w_e3, w_e2, w_s1, w_s3, w_s2):
    depth = w_mod.shape[0]
    n_dec = x_sample.shape[0]
    rope_tabs = _rope_tables(x_sample.shape[1])
    cond8 = jnp.concatenate([c_ctx[None, :], c, jnp.zeros((8 - 1 - n_dec, D), F32)], axis=0)
    named = dict(norm1_g=norm1_g, norm2_g=norm2_g, gqa_qn=gqa_qn, gqa_kn=gqa_kn, mla_qa_g=mla_qa_g,
                 mla_kva_g=mla_kva_g, mla_qn_nope=mla_qn_nope, mla_qn_rope=mla_qn_rope, mla_kn_nope=mla_kn_nope,
                 mla_kn_rope=mla_kn_rope, w_oa=w_oa, w_ob=w_ob, w_out=w_out, w_e1=w_e1, w_e3=w_e3, w_e2=w_e2,
                 w_s1=w_s1, w_s3=w_s3, w_s2=w_s2)
    xp, xs = x_prompt, x_sample
    ks, vs, ckvs, krs = [], [], [], []
    for l in range(depth):
        w = _layer_weights(l, w_in, w_qb, w_kvb, w_router, router_bias, named)
        mod = _modulation(cond8, w_mod[l], b_mod[l].reshape(1, -1))
        mod_ctx = mod[0:1].reshape(1, N_MOD, D)
        mod_dec = mod[1:1 + n_dec].reshape(n_dec, N_MOD, D)
        xp, ctx = _trunk_pass(xp, mod_ctx, False, w, None, None)
        ks.append(ctx[0])
        vs.append(ctx[1])
        ckvs.append(ctx[2])
        krs.append(ctx[3])
        cache = (cache_gqa_k[:, l], cache_gqa_v[:, l], cache_mla_ckv[:, l], cache_mla_krope[:, l])
        xs, _ = _trunk_pass(xs, mod_dec, True, w, rope_tabs, cache)
    return (xp, xs, jnp.stack(ks, axis=1), jnp.stack(vs, axis=1), jnp.stack(ckvs, axis=1), jnp.stack(krs, axis=1))
```
